```python
import jax
import jax.numpy as jnp
from jax import lax
import numpy as np

D_MODEL = 2048
BATCH = 4
SEQ = 4096
DEPTH = 4

CTX_LEN = 256
GRID_W = 64
HEAD_DIM = 128
N_ATTN_HEADS = 12
N_KV_HEADS = 4
Q_PER_KV = N_ATTN_HEADS // N_KV_HEADS
ATTN_WIDTH = N_ATTN_HEADS * HEAD_DIM
KV_WIDTH = N_KV_HEADS * HEAD_DIM
ATTN_SCALE = HEAD_DIM ** -0.5
ROPE_THETA = 10000.0
Q_BLOCK = 128
FOURIER_WIDTH = D_MODEL - ATTN_WIDTH
N_FOURIER_GROUPS = 4
FOURIER_GROUP = FOURIER_WIDTH // N_FOURIER_GROUPS
Q_END = ATTN_WIDTH
K_END = Q_END + KV_WIDTH
V_END = K_END + KV_WIDTH
IN_WIDTH = V_END + FOURIER_WIDTH
RWKV_HEAD = 64
RWKV_HEADS = D_MODEL // RWKV_HEAD
DECAY_LORA = 96
ICLR_LORA = 96
VRES_LORA = 64
GATE_LORA = 256
N_DIRS = 2
GN_EPS = 64e-5
D_FF = 5632
CONV_W = 3
NORM_EPS = 1e-6
N_EVEN = (DEPTH + 1) // 2
N_ODD = DEPTH // 2
N_VRES = N_ODD - 1

kernel_name = 'hybrid_attn_fourier_rwkv7_dit_trunk'


def rms_norm(x, g):
    xf = x.astype(jnp.float32)
    y = xf * lax.rsqrt(jnp.mean(xf * xf, axis=-1, keepdims=True) + NORM_EPS)
    return (y * g.astype(jnp.float32)).astype(x.dtype)


def modulate(h, shift, scale):
    return h * (1 + scale) + shift


def axial_rope_tables(n_tokens):
    ROWS = n_tokens // GRID_W
    row = jnp.repeat(jnp.arange(ROWS, dtype=jnp.float32), GRID_W)
    col = jnp.tile(jnp.arange(GRID_W, dtype=jnp.float32), ROWS)
    axis_dim = HEAD_DIM // 2
    inv_freq = ROPE_THETA ** (-jnp.arange(0, axis_dim, 2, dtype=jnp.float32) / axis_dim)
    ang = jnp.concatenate([row[:, None] * inv_freq, col[:, None] * inv_freq], axis=-1)
    return jnp.cos(ang), jnp.sin(ang)


def apply_rope(x, cos, sin):
    xf = x.astype(jnp.float32).reshape(*x.shape[:-1], HEAD_DIM // 2, 2)
    x0, x1 = xf[..., 0], xf[..., 1]
    c = cos[None, :, None, :]
    s = sin[None, :, None, :]
    out = jnp.stack([x0 * c - x1 * s, x0 * s + x1 * c], axis=-1)
    return out.reshape(x.shape).astype(x.dtype)


def gqa_attend(q, k, v):
    s = jnp.einsum('bqngd,bsnd->bngqs', q, k).astype(jnp.float32) * ATTN_SCALE
    p = jax.nn.softmax(s, axis=-1).astype(v.dtype)
    return jnp.einsum('bngqs,bsnd->bqngd', p, v)


def fourier_mix(f):
    bsz, n, _ = f.shape
    fg = f.astype(jnp.float32).reshape(bsz, n, N_FOURIER_GROUPS, FOURIER_GROUP)
    out = jnp.fft.fft2(fg, axes=(1, 3), norm='ortho').real
    return out.astype(f.dtype).reshape(bsz, n, FOURIER_WIDTH)


def attn_fourier_mixer(h_ctx, h_lat, rope_cos, rope_sin, w_in, w_out, q_g, k_g):
    def project(h):
        bsz, n, _ = h.shape
        u = h @ w_in
        q = rms_norm(u[..., :Q_END].reshape(bsz, n, N_ATTN_HEADS, HEAD_DIM), q_g)
        k = rms_norm(u[..., Q_END:K_END].reshape(bsz, n, N_KV_HEADS, HEAD_DIM), k_g)
        v = u[..., K_END:V_END].reshape(bsz, n, N_KV_HEADS, HEAD_DIM)
        return q, k, v, u[..., V_END:]

    def group(q):
        return q.reshape(*q.shape[:2], N_KV_HEADS, Q_PER_KV, HEAD_DIM)

    bsz, n_ctx, _ = h_ctx.shape
    n_lat = h_lat.shape[1]
    q_c, k_c, v_c, f_c = project(h_ctx)
    q_l, k_l, v_l, f_l = project(h_lat)
    q_l = apply_rope(q_l, rope_cos, rope_sin)
    k_l = apply_rope(k_l, rope_cos, rope_sin)
    o_c = gqa_attend(group(q_c), k_c, v_c).reshape(bsz, n_ctx, ATTN_WIDTH)
    k_all = jnp.concatenate([k_c, k_l], axis=1)
    v_all = jnp.concatenate([v_c, v_l], axis=1)
    n_blk = n_lat // Q_BLOCK
    q_blocks = jnp.moveaxis(group(q_l).reshape(bsz, n_blk, Q_BLOCK, N_KV_HEADS, Q_PER_KV, HEAD_DIM), 1, 0)
    o_l = lax.map(lambda qb: gqa_attend(qb, k_all, v_all), q_blocks)
    o_l = jnp.moveaxis(o_l, 0, 1).reshape(bsz, n_lat, ATTN_WIDTH)
    y_c = jnp.concatenate([o_c, fourier_mix(f_c)], axis=-1) @ w_out
    y_l = jnp.concatenate([o_l, fourier_mix(f_l)], axis=-1) @ w_out
    return y_c, y_l


def centred_shift(x):
    xp = jnp.pad(x, ((0, 0), (1, 1), (0, 0)))
    return 0.5 * (xp[:, :-2] + xp[:, 2:]) - x


def wkv7_scan(r, w, k, v, a, b):
    _, bsz, nh, hd = r.shape

    def step(state, inp):
        r_t, w_t, k_t, v_t, a_t, b_t = inp
        sa = jnp.einsum('bhvk,bhk->bhv', state, a_t)
        state = (state * w_t[:, :, None, :] + sa[..., None] * b_t[:, :, None, :]
                 + v_t[..., None] * k_t[:, :, None, :])
        return state, jnp.einsum('bhvk,bhk->bhv', state, r_t)

    _, y = lax.scan(step, jnp.zeros((bsz, nh, hd, hd), jnp.float32), (r, w, k, v, a, b))
    return y


def rwkv7_mixer(h_ctx, h_lat, v_first, vres, mu, w_r, w_k, w_v, w_o, decay_w0, decay_w1, decay_w2,
                iclr_a0, iclr_a1, iclr_a2, gate_g1, gate_g2, k_k, k_a, r_k, lnx_g, lnx_b):
    bsz, n_ctx, _ = h_ctx.shape
    n_tot = n_ctx + h_lat.shape[1]
    h = jnp.concatenate([h_ctx, h_lat], axis=1)
    dx = jnp.concatenate([centred_shift(h_ctx), centred_shift(h_lat)], axis=1)
    x_r, x_w, x_k, x_v, x_a, x_g = [h + dx * mu[m] for m in range(6)]
    r = x_r @ w_r
    k = x_k @ w_k
    v = x_v @ w_v
    if vres is None:
        v_first = v
    else:
        v0, v1, v2 = vres
        v = v + (v_first - v) * jax.nn.sigmoid(v0 + (x_v @ v1) @ v2)
    g = jax.nn.sigmoid(x_g @ gate_g1) @ gate_g2

    def heads(t):
        return t.astype(jnp.float32).reshape(bsz, n_tot, RWKV_HEADS, RWKV_HEAD)

    def seg_flip(t):
        return jnp.concatenate([jnp.flip(t[:, :n_ctx], 1), jnp.flip(t[:, n_ctx:], 1)], axis=1)

    r_h, v_h, k_h = heads(r), heads(v), heads(k)
    kk = heads(k * k_k)
    kk = kk / jnp.maximum(jnp.linalg.norm(kk, axis=-1, keepdims=True), 1e-12)
    ka_h = k_a.astype(jnp.float32).reshape(RWKV_HEADS, RWKV_HEAD)
    rk_h = r_k.astype(jnp.float32)
    ys, bonuses = [], []
    for d in range(N_DIRS):
        w_log = -jax.nn.softplus(-(decay_w0[d] + jnp.tanh(x_w @ decay_w1[d]) @ decay_w2[d])) - 0.5
        decay = jnp.exp(-jnp.exp(heads(w_log)))
        a = jax.nn.sigmoid(heads(iclr_a0[d] + (x_a @ iclr_a1[d]) @ iclr_a2[d]))
        k_d = k_h * (1 + (a - 1) * ka_h)
        seqs = (r_h, decay, k_d, v_h, -kk, kk * a)
        if d == 1:
            seqs = tuple(seg_flip(t) for t in seqs)
        y = jnp.moveaxis(wkv7_scan(*(jnp.moveaxis(t, 1, 0) for t in seqs)), 0, 1)
        if d == 1:
            y = seg_flip(y)
        ys.append(y)
        bonuses.append(jnp.sum(r_h * k_d * rk_h, axis=-1, keepdims=True) * v_h)
    wkv = ys[0] + ys[1]
    mean = jnp.mean(wkv, axis=-1, keepdims=True)
    var = jnp.var(wkv, axis=-1, keepdims=True)
    normed = ((wkv - mean) * lax.rsqrt(var + GN_EPS)).reshape(bsz, n_tot, D_MODEL) * lnx_g + lnx_b
    bonus = (bonuses[0] + bonuses[1]).reshape(bsz, n_tot, D_MODEL)
    out = ((normed + bonus) * g).astype(h.dtype) @ w_o
    return out[:, :n_ctx], out[:, n_ctx:], v_first


def conv_gated_ffn(h, w_up, conv_w, conv_b, w_down):
    n = h.shape[1]
    u = jnp.pad(h @ w_up, ((0, 0), (CONV_W // 2, CONV_W // 2), (0, 0)))
    u = sum((u[:, t:t + n] * conv_w[t] for t in range(CONV_W)), conv_b)
    gate, val = jnp.split(u, 2, axis=-1)
    return (jax.nn.silu(gate) * val) @ w_down


def setup_inputs(seed: int = 0) -> dict:
    key = jax.random.key(seed)
    ks = iter(jax.random.split(key, 48))
    f32 = jnp.float32
    D = D_MODEL

    def nrm(shape, scale):
        return scale * jax.random.normal(next(ks), shape, f32)

    def uni(shape, lo, hi):
        return jax.random.uniform(next(ks), shape, f32, lo, hi)

    return {
        'x': nrm((BATCH, SEQ, D), 1.0),
        'c': nrm((BATCH, D), 1.0),
        'ctx': nrm((BATCH, CTX_LEN, D), 1.0),
        'c_ctx': nrm((D,), 1.0),
        'w_mod': nrm((DEPTH, D, 6 * D), 0.5 * D ** -0.5),
        'b_mod': nrm((DEPTH, 6 * D), 0.01),
        'norm1_g': 1.0 + nrm((DEPTH, D), 0.05),
        'norm2_g': 1.0 + nrm((DEPTH, D), 0.05),
        'attn_w_in': nrm((N_EVEN, D, IN_WIDTH), D ** -0.5),
        'attn_w_out': nrm((N_EVEN, D, D), D ** -0.5),
        'q_norm_g': 1.0 + nrm((N_EVEN, HEAD_DIM), 0.05),
        'k_norm_g': 1.0 + nrm((N_EVEN, HEAD_DIM), 0.05),
        'rwkv_mu': uni((N_ODD, 6, D), 0.0, 1.0),
        'rwkv_w_r': nrm((N_ODD, D, D), D ** -0.5),
        'rwkv_w_k': nrm((N_ODD, D, D), D ** -0.5),
        'rwkv_w_v': nrm((N_ODD, D, D), D ** -0.5),
        'rwkv_w_o': nrm((N_ODD, D, D), D ** -0.5),
        'rwkv_decay_w0': uni((N_ODD, N_DIRS, D), -6.0, 1.0),
        'rwkv_decay_w1': nrm((N_ODD, N_DIRS, D, DECAY_LORA), D ** -0.5),
        'rwkv_decay_w2': nrm((N_ODD, N_DIRS, DECAY_LORA, D), 0.1 * DECAY_LORA ** -0.5),
        'rwkv_iclr_a0': nrm((N_ODD, N_DIRS, D), 0.1),
        'rwkv_iclr_a1': nrm((N_ODD, N_DIRS, D, ICLR_LORA), D ** -0.5),
        'rwkv_iclr_a2': nrm((N_ODD, N_DIRS, ICLR_LORA, D), 0.1 * ICLR_LORA ** -0.5),
        'rwkv_gate_g1': nrm((N_ODD, D, GATE_LORA), D ** -0.5),
        'rwkv_gate_g2': nrm((N_ODD, GATE_LORA, D), GATE_LORA ** -0.5),
        'rwkv_k_k': 0.85 + nrm((N_ODD, D), 0.05),
        'rwkv_k_a': 1.0 + nrm((N_ODD, D), 0.05),
        'rwkv_r_k': nrm((N_ODD, RWKV_HEADS, RWKV_HEAD), 0.1),
        'rwkv_lnx_g': 1.0 + nrm((N_ODD, D), 0.05),
        'rwkv_lnx_b': nrm((N_ODD, D), 0.01),
        'rwkv_vres_v0': 1.0 + nrm((N_VRES, D), 0.1),
        'rwkv_vres_v1': nrm((N_VRES, D, VRES_LORA), D ** -0.5),
        'rwkv_vres_v2': nrm((N_VRES, VRES_LORA, D), 0.1 * VRES_LORA ** -0.5),
        'ffn_w_up': nrm((DEPTH, D, 2 * D_FF), D ** -0.5),
        'ffn_conv_w': nrm((DEPTH, CONV_W, 2 * D_FF), CONV_W ** -0.5),
        'ffn_conv_b': nrm((DEPTH, 2 * D_FF), 0.01),
        'ffn_w_down': nrm((DEPTH, D_FF, D), D_FF ** -0.5),
        'final_norm_g': 1.0 + nrm((D,), 0.05),
    }


def reference(x, c, ctx, c_ctx, w_mod, b_mod, norm1_g, norm2_g, attn_w_in, attn_w_out, q_norm_g, k_norm_g,
              rwkv_mu, rwkv_w_r, rwkv_w_k, rwkv_w_v, rwkv_w_o, rwkv_decay_w0, rwkv_decay_w1, rwkv_decay_w2,
              rwkv_iclr_a0, rwkv_iclr_a1, rwkv_iclr_a2, rwkv_gate_g1, rwkv_gate_g2, rwkv_k_k, rwkv_k_a, rwkv_r_k,
              rwkv_lnx_g, rwkv_lnx_b, rwkv_vres_v0, rwkv_vres_v1, rwkv_vres_v2,
              ffn_w_up, ffn_conv_w, ffn_conv_b, ffn_w_down, final_norm_g):
    rope_cos, rope_sin = axial_rope_tables(x.shape[1])
    silu_c = jax.nn.silu(c)
    silu_cc = jax.nn.silu(c_ctx)
    x_lat, x_ctx = x, ctx
    v_first = None
    for i in range(DEPTH):
        last = i == DEPTH - 1
        j = i // 2
        m_lat = jnp.split((silu_c @ w_mod[i] + b_mod[i])[:, None, :], 6, axis=-1)
        m_ctx = jnp.split((silu_cc @ w_mod[i] + b_mod[i])[None, None, :], 6, axis=-1)
        h_lat = modulate(rms_norm(x_lat, norm1_g[i]), m_lat[0], m_lat[1])
        h_ctx = modulate(rms_norm(x_ctx, norm1_g[i]), m_ctx[0], m_ctx[1])
        if i % 2 == 0:
            y_ctx, y_lat = attn_fourier_mixer(h_ctx, h_lat, rope_cos, rope_sin, attn_w_in[j], attn_w_out[j],
                                              q_norm_g[j], k_norm_g[j])
        else:
            vres = None if j == 0 else (rwkv_vres_v0[j - 1], rwkv_vres_v1[j - 1], rwkv_vres_v2[j - 1])
            y_ctx, y_lat, v_first = rwkv7_mixer(
                h_ctx, h_lat, v_first, vres, rwkv_mu[j], rwkv_w_r[j], rwkv_w_k[j], rwkv_w_v[j], rwkv_w_o[j],
                rwkv_decay_w0[j], rwkv_decay_w1[j], rwkv_decay_w2[j], rwkv_iclr_a0[j], rwkv_iclr_a1[j],
                rwkv_iclr_a2[j], rwkv_gate_g1[j], rwkv_gate_g2[j], rwkv_k_k[j], rwkv_k_a[j], rwkv_r_k[j],
                rwkv_lnx_g[j], rwkv_lnx_b[j])
        x_lat = x_lat + m_lat[2] * y_lat
        h_lat = modulate(rms_norm(x_lat, norm2_g[i]), m_lat[3], m_lat[4])
        x_lat = x_lat + m_lat[5] * conv_gated_ffn(h_lat, ffn_w_up[i], ffn_conv_w[i], ffn_conv_b[i], ffn_w_down[i])
        if not last:
            x_ctx = x_ctx + m_ctx[2] * y_ctx
            h_ctx = modulate(rms_norm(x_ctx, norm2_g[i]), m_ctx[3], m_ctx[4])
            x_ctx = x_ctx + m_ctx[5] * conv_gated_ffn(h_ctx, ffn_w_up[i], ffn_conv_w[i], ffn_conv_b[i],
                                                       ffn_w_down[i])
    return rms_norm(x_lat, final_norm_g)
```

```python
import collections
import functools
import math

import jax
import jax.numpy as jnp
from jax import lax
from jax.experimental import pallas as pl
from jax.experimental.pallas import tpu as pltpu

HEAD_DIM = 128
N_ATTN_HEADS = 12
N_KV_HEADS = 4
GRID_W = 64
ROPE_THETA = 10000.0
N_FOURIER_GROUPS = 4
RWKV_HEAD = 64
CONV_W = 3
NORM_EPS = 1e-6
GN_EPS = 64e-5

LANES = 128
BF16_ROWS = 16
VMEM_LIMIT = 56 * 1024 * 1024
WKV_CHUNK = 64

_F32 = jnp.float32
_BF16 = jnp.bfloat16
_HI = lax.Precision.HIGHEST

Layout = collections.namedtuple("Layout", "B n nc lat_rows rows")


def _cparams(*sem):
    return pltpu.CompilerParams(dimension_semantics=sem, vmem_limit_bytes=VMEM_LIMIT)


def _pick(total, pref):
    t = min(pref, total)
    while total % t:
        t -= 1
    return t


def _seg_index(lay, tm):
    def f(i):
        r0 = i * tm
        return jnp.where(r0 < lay.lat_rows, r0 // lay.n, lay.B)
    return f


def _seg_masks(i, tm, shape, lay):
    t = lax.broadcasted_iota(jnp.int32, shape, 0)
    r0 = i * tm
    is_lat = r0 < lay.lat_rows
    if tm >= lay.nc:
        pc = lax.rem(t, lay.nc)
    else:
        pc = t + lax.rem(r0 - lay.lat_rows, lay.nc)
    pos = jnp.where(is_lat, t + lax.rem(r0, lay.n), pc)
    last = jnp.where(is_lat, lay.n - 1, lay.nc - 1)
    return pos != 0, pos != last


def _normmod(x, g, shift, scale):
    ms = jnp.mean(x * x, axis=-1, keepdims=True)
    return (x * lax.rsqrt(ms + NORM_EPS)) * g * (1.0 + scale) + shift


def _sigmoid(x):
    return 1.0 / (1.0 + jnp.exp(-x))


def _dot(a, b, prec=None):
    return jnp.dot(a, b, preferred_element_type=_F32, precision=prec)


def _dot_nt(a, b, prec=None):
    return lax.dot_general(a, b, (((1,), (1,)), ((), ())), preferred_element_type=_F32, precision=prec)


def _dot_tn(a, b, prec=None):
    return lax.dot_general(a, b, (((0,), (0,)), ((), ())), preferred_element_type=_F32, precision=prec)


def _mod_kernel(c_ref, w_ref, b_ref, o_ref):
    c = c_ref[...]
    s = c * _sigmoid(c)
    o_ref[...] = _dot(s, w_ref[...], _HI) + b_ref[...]


def _modulation(cvec, w_mod, b_mod):
    depth, D, N = w_mod.shape
    rows = cvec.shape[0]
    tn = _pick(N, 512)
    return pl.pallas_call(
        _mod_kernel,
        grid=(depth, N // tn),
        in_specs=[
            pl.BlockSpec((rows, D), lambda l, j: (0, 0)),
            pl.BlockSpec((None, D, tn), lambda l, j: (l, 0, j)),
            pl.BlockSpec((None, 1, tn), lambda l, j: (l, 0, j)),
        ],
        out_specs=pl.BlockSpec((None, rows, tn), lambda l, j: (l, 0, j)),
        out_shape=jax.ShapeDtypeStruct((depth, rows, N), _F32),
        compiler_params=_cparams("parallel", "parallel"),
        name="adaln_modulation",
    )(cvec, w_mod, b_mod.reshape(depth, 1, N))


def _linear_kernel(act, has_bias, *refs):
    x_ref, w_ref = refs[0], refs[1]
    o_ref = refs[-1]
    acc = _dot(x_ref[...], w_ref[...])
    if has_bias:
        acc = acc + refs[2][...]
    if act == "tanh":
        acc = jnp.tanh(acc)
    elif act == "sigmoid":
        acc = _sigmoid(acc)
    o_ref[...] = acc.astype(o_ref.dtype)


def _linear(x, w, bias=None, act=None, out_dtype=_F32, tm=512, tn=2048, name="linear"):
    M, K = x.shape
    N = w.shape[1]
    tm, tn = _pick(M, tm), _pick(N, tn)
    in_specs = [pl.BlockSpec((tm, K), lambda i, j: (i, 0)), pl.BlockSpec((K, tn), lambda i, j: (0, j))]
    args = [x, w]
    if bias is not None:
        in_specs.append(pl.BlockSpec((1, tn), lambda i, j: (0, j)))
        args.append(bias.reshape(1, N))
    return pl.pallas_call(
        functools.partial(_linear_kernel, act, bias is not None),
        grid=(M // tm, N // tn),
        in_specs=in_specs,
        out_specs=pl.BlockSpec((tm, tn), lambda i, j: (i, j)),
        out_shape=jax.ShapeDtypeStruct((M, N), out_dtype),
        compiler_params=_cparams("parallel", "arbitrary"),
        name=name,
    )(*args)


def _attn_in_kernel(n_qk_tiles, n_q_tiles, x_ref, g_ref, sh_ref, sc_ref, w_ref, qg_ref, kg_ref, cos_ref, sin_ref,
                    o_ref, h_ref):
    j = pl.program_id(1)

    @pl.when(j == 0)
    def _():
        h_ref[...] = _normmod(x_ref[...], g_ref[...], sh_ref[...], sc_ref[...]).astype(_BF16)

    acc = _dot(h_ref[...], w_ref[...])
    tn = acc.shape[1]

    @pl.when(j < n_qk_tiles)
    def _():
        gain = jnp.where(j < n_q_tiles, qg_ref[...], kg_ref[...])
        cos, sin = cos_ref[...], sin_ref[...]
        for hd in range(tn // HEAD_DIM):
            a = acc[:, hd * HEAD_DIM:(hd + 1) * HEAD_DIM]
            y = a * lax.rsqrt(jnp.mean(a * a, axis=-1, keepdims=True) + NORM_EPS) * gain
            y = y * cos + pltpu.roll(y, HEAD_DIM // 2, 1) * sin
            o_ref[:, hd * HEAD_DIM:(hd + 1) * HEAD_DIM] = y.astype(o_ref.dtype)

    @pl.when(j >= n_qk_tiles)
    def _():
        o_ref[...] = acc.astype(o_ref.dtype)


def _attn_in_proj(x, g, shift, scale, w, qg, kg, cos, sin, lay, tm=512):
    M, D = x.shape
    N = w.shape[1]
    tm = _pick(lay.n, tm)
    tm = _pick(lay.B * lay.nc, tm)
    attn_w = N_ATTN_HEADS * HEAD_DIM
    kv_w = N_KV_HEADS * HEAD_DIM
    tn = math.gcd(math.gcd(attn_w, kv_w), 512)
    seg = _seg_index(lay, tm)
    vec = lambda: pl.BlockSpec((None, 1, D), lambda i, j: (seg(i), 0, 0))
    return pl.pallas_call(
        functools.partial(_attn_in_kernel, (attn_w + kv_w) // tn, attn_w // tn),
        grid=(M // tm, N // tn),
        in_specs=[
            pl.BlockSpec((tm, D), lambda i, j: (i, 0)),
            pl.BlockSpec((1, D), lambda i, j: (0, 0)),
            vec(), vec(),
            pl.BlockSpec((D, tn), lambda i, j: (0, j)),
            pl.BlockSpec((1, HEAD_DIM), lambda i, j: (0, 0)),
            pl.BlockSpec((1, HEAD_DIM), lambda i, j: (0, 0)),
            pl.BlockSpec((tm, HEAD_DIM), lambda i, j: (i, 0)),
            pl.BlockSpec((tm, HEAD_DIM), lambda i, j: (i, 0)),
        ],
        out_specs=pl.BlockSpec((tm, tn), lambda i, j: (i, j)),
        out_shape=jax.ShapeDtypeStruct((M, N), _BF16),
        scratch_shapes=[pltpu.VMEM((tm, D), _BF16)],
        compiler_params=_cparams("parallel", "arbitrary"),
        name="attn_in_proj",
    )(x, g, shift, scale, w, qg, kg, cos, sin)


def _attn_kernel(n_lat_tiles, g_per_kv, q_ref, kc_ref, vc_ref, kl_ref, vl_ref, o_ref):
    qi = pl.program_id(2)
    tq = q_ref.shape[0]
    q = jnp.concatenate([q_ref[:, g * HEAD_DIM:(g + 1) * HEAD_DIM] for g in range(g_per_kv)], axis=0)
    s_c = _dot_nt(q, kc_ref[...])

    def finish(o):
        for g in range(g_per_kv):
            o_ref[:, g * HEAD_DIM:(g + 1) * HEAD_DIM] = o[g * tq:(g + 1) * tq].astype(o_ref.dtype)

    @pl.when(qi < n_lat_tiles)
    def _():
        s_l = _dot_nt(q, kl_ref[...])
        m = jnp.maximum(jnp.max(s_c, axis=-1, keepdims=True), jnp.max(s_l, axis=-1, keepdims=True))
        p_c = jnp.exp(s_c - m)
        p_l = jnp.exp(s_l - m)
        l = jnp.sum(p_c, axis=-1, keepdims=True) + jnp.sum(p_l, axis=-1, keepdims=True)
        o = _dot(p_c.astype(_BF16), vc_ref[...]) + _dot(p_l.astype(_BF16), vl_ref[...])
        finish(o / l)

    @pl.when(qi >= n_lat_tiles)
    def _():
        m = jnp.max(s_c, axis=-1, keepdims=True)
        p_c = jnp.exp(s_c - m)
        l = jnp.sum(p_c, axis=-1, keepdims=True)
        finish(_dot(p_c.astype(_BF16), vc_ref[...]) / l)


def _attention(u, lay, tq=128):
    M = u.shape[0]
    g_per_kv = N_ATTN_HEADS // N_KV_HEADS
    qw = g_per_kv * HEAD_DIM
    attn_w = N_ATTN_HEADS * HEAD_DIM
    tq = _pick(lay.nc, tq)
    nlt, nct = lay.n // tq, lay.nc // tq
    k_blk = attn_w // HEAD_DIM
    v_blk = k_blk + N_KV_HEADS
    ctx_blk0 = lay.lat_rows // lay.nc

    def q_map(b, kv, qi):
        row = jnp.where(qi < nlt, b * nlt + qi, lay.lat_rows // tq + b * nct + (qi - nlt))
        return (row, kv)

    return pl.pallas_call(
        functools.partial(_attn_kernel, nlt, g_per_kv),
        grid=(lay.B, N_KV_HEADS, nlt + nct),
        in_specs=[
            pl.BlockSpec((tq, qw), q_map),
            pl.BlockSpec((lay.nc, HEAD_DIM), lambda b, kv, qi: (ctx_blk0 + b, k_blk + kv)),
            pl.BlockSpec((lay.nc, HEAD_DIM), lambda b, kv, qi: (ctx_blk0 + b, v_blk + kv)),
            pl.BlockSpec((lay.n, HEAD_DIM), lambda b, kv, qi: (b, k_blk + kv)),
            pl.BlockSpec((lay.n, HEAD_DIM), lambda b, kv, qi: (b, v_blk + kv)),
        ],
        out_specs=pl.BlockSpec((tq, qw), q_map),
        out_shape=jax.ShapeDtypeStruct((M, attn_w), _BF16),
        compiler_params=_cparams("parallel", "parallel", "arbitrary"),
        name="gqa_attention",
    )(u, u, u, u, u)


def _dft_kernel(f_ref, cc_ref, sc_ref, cn_ref, sn_ref, *refs):
    o_ref, a_ref, b_ref = refs[-3], refs[-2], refs[-1]

    @pl.when(pl.program_id(1) == 0)
    def _():
        f = f_ref[...]
        a_ref[...] = _dot(f, cc_ref[...]).astype(_BF16)
        b_ref[...] = _dot(f, sc_ref[...]).astype(_BF16)

    o_ref[...] = (_dot(cn_ref[...], a_ref[...]) - _dot(sn_ref[...], b_ref[...])).astype(o_ref.dtype)


def _dft_tables(n, group, n_groups):
    def cs(m):
        k = jnp.arange(m, dtype=jnp.int32)
        ang = (2.0 * math.pi / m) * ((k[:, None] * k[None, :]) % m).astype(_F32)
        return jnp.cos(ang), jnp.sin(ang)
    cn, sn = cs(n)
    scale = 1.0 / math.sqrt(n * group)
    cg, sg = cs(group)
    eye = jnp.eye(n_groups, dtype=_F32)
    return ((cn * scale).astype(_BF16), (sn * scale).astype(_BF16),
            jnp.kron(eye, cg).astype(_BF16), jnp.kron(eye, sg).astype(_BF16))


def _fourier_mix(u, lay, fw, prev=None, ctx=False, tm=512):
    M, N = u.shape
    n = lay.nc if ctx else lay.n
    tm = _pick(n, tm)
    nt = n // tm
    cn, sn, cc, sc = _dft_tables(n, fw // N_FOURIER_GROUPS, N_FOURIER_GROUPS)
    f_blk = N // fw - 1
    seg0 = lay.lat_rows // n if ctx else 0
    row0 = lay.lat_rows // tm if ctx else 0
    in_specs = [
        pl.BlockSpec((n, fw), lambda b, i: (seg0 + b, f_blk)),
        pl.BlockSpec((fw, fw), lambda b, i: (0, 0)),
        pl.BlockSpec((fw, fw), lambda b, i: (0, 0)),
        pl.BlockSpec((tm, n), lambda b, i: (i, 0)),
        pl.BlockSpec((tm, n), lambda b, i: (i, 0)),
    ]
    args = [u, cc, sc, cn, sn]
    aliases = {}
    if prev is not None:
        in_specs.append(pl.BlockSpec(memory_space=pl.ANY))
        args.append(prev)
        aliases = {5: 0}
    return pl.pallas_call(
        _dft_kernel,
        grid=(lay.B, nt),
        in_specs=in_specs,
        out_specs=pl.BlockSpec((tm, fw), lambda b, i: (row0 + b * nt + i, 0)),
        out_shape=jax.ShapeDtypeStruct((M, fw), _BF16),
        scratch_shapes=[pltpu.VMEM((n, fw), _BF16), pltpu.VMEM((n, fw), _BF16)],
        input_output_aliases=aliases,
        compiler_params=_cparams("parallel", "arbitrary"),
        name="fourier_mix_ctx" if ctx else "fourier_mix_lat",
    )(*args)


def _out_proj_kernel(n_lhs, *refs):
    lhs = refs[:n_lhs]
    ws = refs[n_lhs:2 * n_lhs]
    x_ref, gate_ref, o_ref = refs[2 * n_lhs], refs[2 * n_lhs + 1], refs[2 * n_lhs + 2]
    acc = _dot(lhs[0][...], ws[0][...])
    for a, w in zip(lhs[1:], ws[1:]):
        acc = acc + _dot(a[...], w[...])
    o_ref[...] = x_ref[...] + gate_ref[...] * acc


def _out_proj(lhs, ws, x, gate, lay, tm=512):
    M, D = x.shape
    tm = _pick(lay.n, tm)
    tm = _pick(lay.B * lay.nc, tm)
    seg = _seg_index(lay, tm)
    in_specs = [pl.BlockSpec((tm, a.shape[1]), lambda i: (i, 0)) for a in lhs]
    in_specs += [pl.BlockSpec(w.shape, lambda i: (0, 0)) for w in ws]
    in_specs += [pl.BlockSpec((tm, D), lambda i: (i, 0)), pl.BlockSpec((None, 1, D), lambda i: (seg(i), 0, 0))]
    return pl.pallas_call(
        functools.partial(_out_proj_kernel, len(lhs)),
        grid=(M // tm,),
        in_specs=in_specs,
        out_specs=pl.BlockSpec((tm, D), lambda i: (i, 0)),
        out_shape=jax.ShapeDtypeStruct((M, D), _F32),
        compiler_params=_cparams("parallel"),
        name="out_proj_residual",
    )(*lhs, *ws, x, gate)


def _ffn_kernel(lay, tm, halo, x_ref, xp_ref, xn_ref, g_ref, sh_ref, sc_ref, gate_ref, wg_ref, wv_ref,
                cwg_ref, cwv_ref, cbg_ref, cbv_ref, wd_ref, o_ref, h_ref, acc_ref, mp_ref, mn_ref):
    i, j = pl.program_id(0), pl.program_id(1)
    ext = tm + 2 * halo

    @pl.when(j == 0)
    def _():
        g, sh, sc = g_ref[...], sh_ref[...], sc_ref[...]
        h_ref[0:halo, :] = _normmod(xp_ref[...], g, sh, sc).astype(_BF16)
        h_ref[halo:halo + tm, :] = _normmod(x_ref[...], g, sh, sc).astype(_BF16)
        h_ref[halo + tm:ext, :] = _normmod(xn_ref[...], g, sh, sc).astype(_BF16)
        acc_ref[...] = jnp.zeros_like(acc_ref)
        has_prev, has_next = _seg_masks(i, tm, mp_ref.shape, lay)
        mp_ref[...] = has_prev.astype(_F32)
        mn_ref[...] = has_next.astype(_F32)

    h = h_ref[...]
    mp, mn = mp_ref[...], mn_ref[...]

    def conv(w_ref, cw_ref, cb_ref):
        u = _dot(h, w_ref[...])
        cw = cw_ref[...]
        up = pltpu.roll(u, 1, 0)[halo:halo + tm]
        un = pltpu.roll(u, ext - 1, 0)[halo:halo + tm]
        return up * mp * cw[0:1] + u[halo:halo + tm] * cw[1:2] + un * mn * cw[2:3] + cb_ref[...]

    gt = conv(wg_ref, cwg_ref, cbg_ref)
    vl = conv(wv_ref, cwv_ref, cbv_ref)
    act = (gt * _sigmoid(gt) * vl).astype(_BF16)
    acc_ref[...] += _dot(act, wd_ref[...])

    @pl.when(j == pl.num_programs(1) - 1)
    def _():
        o_ref[...] = x_ref[...] + gate_ref[...] * acc_ref[...]


def _ffn(x, g, shift, scale, gate, w_up, conv_w, conv_b, w_down, lay, n_rows, tm=512, tf=512):
    M, D = x.shape
    F = w_down.shape[0]
    tm = _pick(lay.n, tm)
    tm = _pick(lay.B * lay.nc, tm)
    tf = _pick(F, tf)
    nf = F // tf
    halo = BF16_ROWS
    per = tm // halo
    seg = _seg_index(lay, tm)
    vec = lambda: pl.BlockSpec((None, 1, D), lambda i, j: (seg(i), 0, 0))
    out = pl.pallas_call(
        functools.partial(_ffn_kernel, lay, tm, halo),
        grid=(n_rows // tm, nf),
        in_specs=[
            pl.BlockSpec((tm, D), lambda i, j: (i, 0)),
            pl.BlockSpec((halo, D), lambda i, j: (jnp.maximum(i * per - 1, 0), 0)),
            pl.BlockSpec((halo, D), lambda i, j: (jnp.minimum((i + 1) * per, M // halo - 1), 0)),
            pl.BlockSpec((1, D), lambda i, j: (0, 0)),
            vec(), vec(), vec(),
            pl.BlockSpec((D, tf), lambda i, j: (0, j)),
            pl.BlockSpec((D, tf), lambda i, j: (0, nf + j)),
            pl.BlockSpec((CONV_W, tf), lambda i, j: (0, j)),
            pl.BlockSpec((CONV_W, tf), lambda i, j: (0, nf + j)),
            pl.BlockSpec((1, tf), lambda i, j: (0, j)),
            pl.BlockSpec((1, tf), lambda i, j: (0, nf + j)),
            pl.BlockSpec((tf, D), lambda i, j: (j, 0)),
        ],
        out_specs=pl.BlockSpec((tm, D), lambda i, j: (i, 0)),
        out_shape=jax.ShapeDtypeStruct((n_rows, D), _F32),
        scratch_shapes=[
            pltpu.VMEM((tm + 2 * halo, D), _BF16),
            pltpu.VMEM((tm, D), _F32),
            pltpu.VMEM((tm, tf), _F32),
            pltpu.VMEM((tm, tf), _F32),
        ],
        compiler_params=_cparams("parallel", "arbitrary"),
        name="conv_gated_ffn",
    )(x, x, x, g, shift, scale, gate, w_up, w_up, conv_w, conv_w, conv_b, conv_b, w_down)
    return out


def _shift_mix_kernel(lay, tm, halo, x_ref, xp_ref, xn_ref, g_ref, sh_ref, sc_ref, mu_ref, *o_refs):
    i = pl.program_id(0)
    g, sh, sc = g_ref[...], sh_ref[...], sc_ref[...]
    h = _normmod(x_ref[...], g, sh, sc)
    hp = _normmod(xp_ref[...], g, sh, sc)
    hn = _normmod(xn_ref[...], g, sh, sc)
    ext = jnp.concatenate([hp, h, hn], axis=0)
    n_ext = tm + 2 * halo
    has_prev, has_next = _seg_masks(i, tm, h.shape, lay)
    prev = jnp.where(has_prev, pltpu.roll(ext, 1, 0)[halo:halo + tm], 0.0)
    nxt = jnp.where(has_next, pltpu.roll(ext, n_ext - 1, 0)[halo:halo + tm], 0.0)
    dx = 0.5 * (prev + nxt) - h
    mu = mu_ref[...]
    for m, o_ref in enumerate(o_refs):
        o_ref[...] = (h + dx * mu[m:m + 1]).astype(o_ref.dtype)


def _shift_mix(x, g, shift, scale, mu, lay, tm=256):
    M, D = x.shape
    tm = _pick(lay.n, tm)
    tm = _pick(lay.B * lay.nc, tm)
    halo = 8
    per = tm // halo
    seg = _seg_index(lay, tm)
    vec = lambda: pl.BlockSpec((None, 1, D), lambda i: (seg(i), 0, 0))
    n_mix = mu.shape[0]
    return pl.pallas_call(
        functools.partial(_shift_mix_kernel, lay, tm, halo),
        grid=(M // tm,),
        in_specs=[
            pl.BlockSpec((tm, D), lambda i: (i, 0)),
            pl.BlockSpec((halo, D), lambda i: (jnp.maximum(i * per - 1, 0), 0)),
            pl.BlockSpec((halo, D), lambda i: (jnp.minimum((i + 1) * per, M // halo - 1), 0)),
            pl.BlockSpec((1, D), lambda i: (0, 0)),
            vec(), vec(),
            pl.BlockSpec((n_mix, D), lambda i: (0, 0)),
        ],
        out_specs=[pl.BlockSpec((tm, D), lambda i: (i, 0)) for _ in range(n_mix)],
        out_shape=[jax.ShapeDtypeStruct((M, D), _BF16) for _ in range(n_mix)],
        compiler_params=_cparams("parallel"),
        name="rwkv_shift_mix",
    )(x, x, x, g, shift, scale, mu)


def _vres_kernel(x_ref, w_ref, b_ref, v_ref, vf_ref, o_ref):
    gate = _sigmoid(_dot(x_ref[...], w_ref[...]) + b_ref[...])
    v = v_ref[...]
    o_ref[...] = v + (vf_ref[...] - v) * gate


def _vres(xl, w2, v0, v, v_first, tm=512):
    M, K = xl.shape
    D = w2.shape[1]
    tm = _pick(M, tm)
    return pl.pallas_call(
        _vres_kernel,
        grid=(M // tm,),
        in_specs=[
            pl.BlockSpec((tm, K), lambda i: (i, 0)),
            pl.BlockSpec((K, D), lambda i: (0, 0)),
            pl.BlockSpec((1, D), lambda i: (0, 0)),
            pl.BlockSpec((tm, D), lambda i: (i, 0)),
            pl.BlockSpec((tm, D), lambda i: (i, 0)),
        ],
        out_specs=pl.BlockSpec((tm, D), lambda i: (i, 0)),
        out_shape=jax.ShapeDtypeStruct((M, D), _F32),
        compiler_params=_cparams("parallel"),
        name="rwkv_value_residual",
    )(xl, w2, v0.reshape(1, D), v, v_first)


def _wkv_kernel(reverse, final, C, LW, prec, r_ref, k_ref, v_ref, wp_ref, ap_ref, kk_ref, ka_ref, rk_ref, *refs):
    if final:
        y0_ref, b0_ref, g_ref, lng_ref, lnb_ref, o_ref, s_ref = refs
    else:
        y_ref, bon_ref, s_ref = refs
    j = pl.program_id(2)

    @pl.when(j == 0)
    def _():
        s_ref[...] = jnp.zeros_like(s_ref)

    r, k, v = r_ref[...], k_ref[...], v_ref[...]
    wp, ap = wp_ref[...], ap_ref[...]

    row = lax.broadcasted_iota(jnp.int32, (C, C), 0)
    col = lax.broadcasted_iota(jnp.int32, (C, C), 1)
    tri = jnp.where((col >= row) if reverse else (col <= row), 1.0, 0.0).astype(_F32)
    row2 = lax.broadcasted_iota(jnp.int32, (C, 2 * C), 0)
    col2 = lax.broadcasted_iota(jnp.int32, (C, 2 * C), 1)
    col2 = jnp.where(col2 >= C, col2 - C, col2)
    strict2 = (col2 > row2) if reverse else (col2 < row2)
    incl2 = (col2 >= row2) if reverse else (col2 <= row2)
    hr = lax.broadcasted_iota(jnp.int32, (LW, LW), 0) // RWKV_HEAD
    hc = lax.broadcasted_iota(jnp.int32, (LW, LW), 1) // RWKV_HEAD
    head_ones = jnp.where(hr == hc, 1.0, 0.0).astype(_F32)
    pr = lax.broadcasted_iota(jnp.int32, (LANES, LANES), 0) // RWKV_HEAD
    pc = lax.broadcasted_iota(jnp.int32, (LANES, LANES), 1) // RWKV_HEAD
    same_head = pr == pc
    first_head = lax.broadcasted_iota(jnp.int32, (C, LANES), 1) < RWKV_HEAD
    eye = jnp.where(row == col, 1.0, 0.0).astype(_F32)

    sp = jnp.maximum(-wp, 0.0) + jnp.log(1.0 + jnp.exp(-jnp.abs(wp)))
    lw = -jnp.exp(-sp - 0.5)
    cl = _dot(tri, lw, _HI)
    total = cl[0:1] if reverse else cl[C - 1:C]

    a = _sigmoid(ap)
    kk = k * kk_ref[...]
    nrm = jnp.sqrt(_dot(kk * kk, head_ones, _HI))
    kk = kk / jnp.maximum(nrm, 1e-12)
    kd = k * (1.0 + (a - 1.0) * ka_ref[...])
    bonus = _dot(r * kd * rk_ref[...], head_ones, _HI) * v
    b = kk * a

    e_neg = jnp.exp(-cl)
    e_rem = jnp.exp(total - cl)
    At = -kk * jnp.exp(cl - lw)
    Rt = r * jnp.exp(cl)
    Bt, Kt = b * e_neg, kd * e_neg
    Bp, Kp = b * e_rem, kd * e_rem
    decay_total = jnp.exp(total)

    ys = []
    for p in range(LW // LANES):
        sl = slice(p * LANES, (p + 1) * LANES)
        At_p, Rt_p, V_p = At[:, sl], Rt[:, sl], v[:, sl]
        S0 = s_ref[p]
        AS = _dot_nt(At_p, S0, prec)
        RS = _dot_nt(Rt_p, S0, prec)
        BK = jnp.concatenate([Bt[:, sl], Kt[:, sl]], axis=0)
        ZV0 = jnp.concatenate([jnp.zeros_like(V_p), V_p], axis=0)
        Zs, Ys = [], []
        for hh in range(LANES // RWKV_HEAD):
            mh = first_head if hh == 0 else jnp.logical_not(first_head)
            AR = jnp.concatenate([jnp.where(mh, At_p, 0.0), jnp.where(mh, Rt_p, 0.0)], axis=0)
            M = _dot_nt(AR, BK, prec)
            M_top = jnp.where(strict2, M[:C], 0.0)
            M_bot = jnp.where(incl2, M[C:], 0.0)
            Lab = M_top[:, :C]
            T = eye + Lab
            Pw = Lab
            for _ in range(int(math.log2(C)) - 1):
                Pw = _dot(Pw, Pw, prec)
                T = T + _dot(T, Pw, prec)
            X = AS + _dot(M_top, ZV0, prec)
            Zh = _dot(T, X, prec)
            Yh = _dot(M_bot, jnp.concatenate([Zh, V_p], axis=0), prec)
            Zs.append(Zh)
            Ys.append(Yh)
        Z = jnp.where(first_head, Zs[0], Zs[1])
        ys.append(RS + jnp.where(first_head, Ys[0], Ys[1]))
        dS = _dot_tn(jnp.concatenate([Z, V_p], axis=0), jnp.concatenate([Bp[:, sl], Kp[:, sl]], axis=0), prec)
        s_ref[p] = S0 * decay_total[:, sl] + jnp.where(same_head, dS, 0.0)
    y = jnp.concatenate(ys, axis=1) if len(ys) > 1 else ys[0]

    if final:
        wkv = y + y0_ref[...]
        inv = 1.0 / RWKV_HEAD
        mean = _dot(wkv, head_ones, _HI) * inv
        cen = wkv - mean
        var = _dot(cen * cen, head_ones, _HI) * inv
        normed = cen * lax.rsqrt(var + GN_EPS) * lng_ref[...] + lnb_ref[...]
        o_ref[...] = ((normed + bonus + b0_ref[...]) * g_ref[...]).astype(o_ref.dtype)
    else:
        y_ref[...] = y
        bon_ref[...] = bonus


def _wkv(r, k, v, wpre, apre, k_k, k_a, r_k, lay, reverse, final_args=None, prec=_HI, lw=256):
    M, D = r.shape
    C = _pick(lay.nc, WKV_CHUNK)
    lw = _pick(D, lw)
    ncc, ncl = lay.nc // C, lay.n // C
    ctx_blk0 = lay.lat_rows // C

    def row_blk(b, j):
        if reverse:
            return jnp.where(j < ncc, ctx_blk0 + b * ncc + (ncc - 1 - j), b * ncl + (ncl - 1 - (j - ncc)))
        return jnp.where(j < ncc, ctx_blk0 + b * ncc + j, b * ncl + (j - ncc))

    tile = lambda: pl.BlockSpec((C, lw), lambda b, h, j: (row_blk(b, j), h))
    vec = lambda: pl.BlockSpec((1, lw), lambda b, h, j: (0, h))
    in_specs = [tile() for _ in range(5)] + [vec() for _ in range(3)]
    args = [r, k, v, wpre, apre, k_k.reshape(1, D), k_a.reshape(1, D), r_k.reshape(1, D)]
    final = final_args is not None
    if final:
        y0, b0, g, lng, lnb = final_args
        in_specs += [tile(), tile(), tile(), vec(), vec()]
        args += [y0, b0, g, lng.reshape(1, D), lnb.reshape(1, D)]
        out_specs = tile()
        out_shape = jax.ShapeDtypeStruct((M, D), _BF16)
    else:
        out_specs = [tile(), tile()]
        out_shape = [jax.ShapeDtypeStruct((M, D), _F32), jax.ShapeDtypeStruct((M, D), _F32)]
    return pl.pallas_call(
        functools.partial(_wkv_kernel, reverse, final, C, lw, prec),
        grid=(lay.B, D // lw, ncc + ncl),
        in_specs=in_specs,
        out_specs=out_specs,
        out_shape=out_shape,
        scratch_shapes=[pltpu.VMEM((lw // LANES, LANES, LANES), _F32)],
        compiler_params=_cparams("parallel", "parallel", "arbitrary"),
        name="wkv7_scan_rev" if reverse else "wkv7_scan_fwd",
    )(*args)


def _final_norm_kernel(x_ref, g_ref, o_ref):
    x = x_ref[...]
    o_ref[...] = x * lax.rsqrt(jnp.mean(x * x, axis=-1, keepdims=True) + NORM_EPS) * g_ref[...]


def _final_norm(x, g, n_rows, tm=512):
    D = x.shape[1]
    tm = _pick(n_rows, tm)
    return pl.pallas_call(
        _final_norm_kernel,
        grid=(n_rows // tm,),
        in_specs=[pl.BlockSpec((tm, D), lambda i: (i, 0)), pl.BlockSpec((1, D), lambda i: (0, 0))],
        out_specs=pl.BlockSpec((tm, D), lambda i: (i, 0)),
        out_shape=jax.ShapeDtypeStruct((n_rows, D), _F32),
        compiler_params=_cparams("parallel"),
        name="final_rms_norm",
    )(x, g.reshape(1, D))


def _rope_tables(lay):
    n = lay.n
    t = jnp.arange(n, dtype=jnp.int32)
    row = (t // GRID_W).astype(_F32)
    col = (t % GRID_W).astype(_F32)
    axis_dim = HEAD_DIM // 2
    inv_freq = ROPE_THETA ** (-jnp.arange(0, axis_dim, 2, dtype=_F32) / axis_dim)
    ang = jnp.concatenate([row[:, None] * inv_freq, col[:, None] * inv_freq], axis=-1)
    cos, sin = jnp.cos(ang), jnp.sin(ang)
    cos_l = jnp.concatenate([cos, cos], axis=-1)
    sin_l = jnp.concatenate([-sin, sin], axis=-1)
    n_ctx_rows = lay.B * lay.nc
    cos_f = jnp.concatenate([jnp.tile(cos_l, (lay.B, 1)), jnp.ones((n_ctx_rows, HEAD_DIM), _F32)], axis=0)
    sin_f = jnp.concatenate([jnp.tile(sin_l, (lay.B, 1)), jnp.zeros((n_ctx_rows, HEAD_DIM), _F32)], axis=0)
    return cos_f, sin_f


def _deinterleave_heads(w, n_heads):
    lead = w.shape[:-1]
    w = w.reshape(*lead, n_heads, HEAD_DIM // 2, 2)
    w = jnp.swapaxes(w, -1, -2)
    return w.reshape(*lead, n_heads * HEAD_DIM)


def _pad_cols(w, mult=LANES):
    pad = (-w.shape[-1]) % mult
    return jnp.pad(w, [(0, 0)] * (w.ndim - 1) + [(0, pad)]) if pad else w


def _pad_rows(w, mult=LANES):
    pad = (-w.shape[-2]) % mult
    return jnp.pad(w, [(0, 0)] * (w.ndim - 2) + [(0, pad), (0, 0)]) if pad else w


def kernel(x, c, ctx, c_ctx, w_mod, b_mod, norm1_g, norm2_g, attn_w_in, attn_w_out, q_norm_g, k_norm_g, rwkv_mu, rwkv_w_r, rwkv_w_k, rwkv_w_v, rwkv_w_o, rwkv_decay_w0, rwkv_decay_w1, rwkv_decay_w2, rwkv_iclr_a0, rwkv_iclr_a1, rwkv_iclr_a2, rwkv_gate_g1, rwkv_gate_g2, rwkv_k_k, rwkv_k_a, rwkv_r_k, rwkv_lnx_g, rwkv_lnx_b, rwkv_vres_v0, rwkv_vres_v1, rwkv_vres_v2, ffn_w_up, ffn_conv_w, ffn_conv_b, ffn_w_down, final_norm_g):
    B, n, D = x.shape
    nc = ctx.shape[1]
    depth = w_mod.shape[0]
    lay = Layout(B, n, nc, B * n, B * (n + nc))
    attn_w = N_ATTN_HEADS * HEAD_DIM
    kv_w = N_KV_HEADS * HEAD_DIM
    fw = D - attn_w

    xs = jnp.concatenate([x.reshape(B * n, D), ctx.reshape(B * nc, D)], axis=0)

    cvec = jnp.concatenate([c, c_ctx[None, :], jnp.zeros((7 - B % 8, D), _F32)], axis=0)
    mods = _modulation(cvec, w_mod, b_mod)[:, :B + 1]
    mods = mods.reshape(depth, B + 1, 6, 1, D)
    cos_f, sin_f = _rope_tables(lay)
    qg_scale = HEAD_DIM ** -0.5

    v_first = None
    for i in range(depth):
        last = i == depth - 1
        jl = i // 2
        m = [mods[i, :, t] for t in range(6)]
        g1 = norm1_g[i].reshape(1, D)
        if i % 2 == 0:
            w_in = attn_w_in[jl]
            w_in = jnp.concatenate([
                _deinterleave_heads(w_in[:, :attn_w], N_ATTN_HEADS),
                _deinterleave_heads(w_in[:, attn_w:attn_w + kv_w], N_KV_HEADS),
                w_in[:, attn_w + kv_w:]], axis=1).astype(_BF16)
            qg = (_deinterleave_heads(q_norm_g[jl], 1) * qg_scale).reshape(1, HEAD_DIM)
            kg = _deinterleave_heads(k_norm_g[jl], 1).reshape(1, HEAD_DIM)
            u = _attn_in_proj(xs, g1, m[0], m[1], w_in, qg, kg, cos_f, sin_f, lay)
            o = _attention(u, lay)
            fm = _fourier_mix(u, lay, fw)
            fm = _fourier_mix(u, lay, fw, prev=fm, ctx=True)
            w_out = attn_w_out[jl].astype(_BF16)
            xs = _out_proj([o, fm], [w_out[:attn_w], w_out[attn_w:]], xs, m[2], lay)
        else:
            bf = lambda w: w.astype(_BF16)
            x_r, x_w, x_k, x_v, x_a, x_g = _shift_mix(xs, g1, m[0], m[1], rwkv_mu[jl], lay)
            r = _linear(x_r, bf(rwkv_w_r[jl]), name="rwkv_r")
            k = _linear(x_k, bf(rwkv_w_k[jl]), name="rwkv_k")
            v = _linear(x_v, bf(rwkv_w_v[jl]), name="rwkv_v")
            if jl == 0:
                v_first = v
            else:
                vl = _linear(x_v, bf(_pad_cols(rwkv_vres_v1[jl - 1])), out_dtype=_BF16, name="rwkv_vres_lora")
                v = _vres(vl, bf(_pad_rows(rwkv_vres_v2[jl - 1])), rwkv_vres_v0[jl - 1], v, v_first)
            gl = _linear(x_g, bf(_pad_cols(rwkv_gate_g1[jl])), act="sigmoid", out_dtype=_BF16, name="rwkv_gate_lora")
            g = _linear(gl, bf(_pad_rows(rwkv_gate_g2[jl])), name="rwkv_gate")
            pre = []
            for d in range(2):
                dl = _linear(x_w, bf(_pad_cols(rwkv_decay_w1[jl, d])), act="tanh", out_dtype=_BF16,
                             name="rwkv_decay_lora")
                wpre = _linear(dl, bf(_pad_rows(rwkv_decay_w2[jl, d])), bias=rwkv_decay_w0[jl, d], name="rwkv_decay")
                al = _linear(x_a, bf(_pad_cols(rwkv_iclr_a1[jl, d])), out_dtype=_BF16, name="rwkv_iclr_lora")
                apre = _linear(al, bf(_pad_rows(rwkv_iclr_a2[jl, d])), bias=rwkv_iclr_a0[jl, d], name="rwkv_iclr")
                pre.append((wpre, apre))
            kk_, ka_, rk_ = rwkv_k_k[jl], rwkv_k_a[jl], rwkv_r_k[jl].reshape(D)
            y0, bon0 = _wkv(r, k, v, pre[0][0], pre[0][1], kk_, ka_, rk_, lay, reverse=False)
            z = _wkv(r, k, v, pre[1][0], pre[1][1], kk_, ka_, rk_, lay, reverse=True,
                     final_args=(y0, bon0, g, rwkv_lnx_g[jl], rwkv_lnx_b[jl]))
            xs = _out_proj([z], [bf(rwkv_w_o[jl])], xs, m[2], lay)
        n_rows = lay.lat_rows if last else lay.rows
        ffn_out = _ffn(xs, norm2_g[i].reshape(1, D), m[3], m[4], m[5], ffn_w_up[i].astype(_BF16), ffn_conv_w[i],
                       ffn_conv_b[i].reshape(1, -1), ffn_w_down[i].astype(_BF16), lay, n_rows)
        xs = ffn_out
    out = _final_norm(xs, final_norm_g, lay.lat_rows)
    return out.reshape(B, n, D)
```

```python
import collections
import functools
import math

import jax
import jax.numpy as jnp
from jax import lax
from jax.experimental import pallas as pl
from jax.experimental.pallas import tpu as pltpu

HEAD_DIM = 128
N_ATTN_HEADS = 12
N_KV_HEADS = 4
GRID_W = 64
ROPE_THETA = 10000.0
N_FOURIER_GROUPS = 4
RWKV_HEAD = 64
CONV_W = 3
NORM_EPS = 1e-6
GN_EPS = 64e-5

LANES = 128
BF16_ROWS = 16
VMEM_LIMIT = 56 * 1024 * 1024
WKV_CHUNK = 64

_F32 = jnp.float32
_BF16 = jnp.bfloat16
_HI = lax.Precision.HIGHEST

Layout = collections.namedtuple("Layout", "B n nc lat_rows rows")


def _cparams(*sem):
    return pltpu.CompilerParams(dimension_semantics=sem, vmem_limit_bytes=VMEM_LIMIT)


def _pick(total, pref):
    t = min(pref, total)
    while total % t:
        t -= 1
    return t


def _seg_index(lay, tm):
    def f(i):
        r0 = i * tm
        return jnp.where(r0 < lay.lat_rows, r0 // lay.n, lay.B)
    return f


def _seg_masks(i, tm, shape, lay):
    t = lax.broadcasted_iota(jnp.int32, shape, 0)
    r0 = i * tm
    is_lat = r0 < lay.lat_rows
    if tm >= lay.nc:
        pc = lax.rem(t, lay.nc)
    else:
        pc = t + lax.rem(r0 - lay.lat_rows, lay.nc)
    pos = jnp.where(is_lat, t + lax.rem(r0, lay.n), pc)
    last = jnp.where(is_lat, lay.n - 1, lay.nc - 1)
    return pos != 0, pos != last


def _normmod(x, g, shift, scale):
    ms = jnp.mean(x * x, axis=-1, keepdims=True)
    return (x * lax.rsqrt(ms + NORM_EPS)) * g * (1.0 + scale) + shift


def _sigmoid(x):
    return 1.0 / (1.0 + jnp.exp(-x))


def _dot(a, b, prec=None):
    return jnp.dot(a, b, preferred_element_type=_F32, precision=prec)


def _dot_nt(a, b, prec=None):
    return lax.dot_general(a, b, (((1,), (1,)), ((), ())), preferred_element_type=_F32, precision=prec)


def _dot_tn(a, b, prec=None):
    return lax.dot_general(a, b, (((0,), (0,)), ((), ())), preferred_element_type=_F32, precision=prec)


def _ones_dot(ones, x, parts, left=False):
    acc = None
    rem = x
    for i in range(parts):
        piece = rem.astype(_BF16)
        if i + 1 < parts:
            rem = rem - piece.astype(_F32)
        d = _dot(ones, piece) if left else _dot(piece, ones)
        acc = d if acc is None else acc + d
    return acc


def _mod_kernel(c_ref, w_ref, b_ref, o_ref):
    c = c_ref[...]
    s = c * _sigmoid(c)
    o_ref[...] = _dot(s, w_ref[...], _HI) + b_ref[...]


def _modulation(cvec, w_mod, b_mod):
    depth, D, N = w_mod.shape
    rows = cvec.shape[0]
    tn = _pick(N, 512)
    return pl.pallas_call(
        _mod_kernel,
        grid=(depth, N // tn),
        in_specs=[
            pl.BlockSpec((rows, D), lambda l, j: (0, 0)),
            pl.BlockSpec((None, D, tn), lambda l, j: (l, 0, j)),
            pl.BlockSpec((None, 1, tn), lambda l, j: (l, 0, j)),
        ],
        out_specs=pl.BlockSpec((None, rows, tn), lambda l, j: (l, 0, j)),
        out_shape=jax.ShapeDtypeStruct((depth, rows, N), _F32),
        compiler_params=_cparams("parallel", "parallel"),
        name="adaln_modulation",
    )(cvec, w_mod, b_mod.reshape(depth, 1, N))


def _linear_kernel(act, has_bias, *refs):
    x_ref, w_ref = refs[0], refs[1]
    o_ref = refs[-1]
    acc = _dot(x_ref[...], w_ref[...])
    if has_bias:
        acc = acc + refs[2][...]
    if act == "tanh":
        acc = jnp.tanh(acc)
    elif act == "sigmoid":
        acc = _sigmoid(acc)
    o_ref[...] = acc.astype(o_ref.dtype)


def _linear(x, w, bias=None, act=None, out_dtype=_F32, tm=512, tn=2048, name="linear"):
    M, K = x.shape
    N = w.shape[1]
    tm, tn = _pick(M, tm), _pick(N, tn)
    in_specs = [pl.BlockSpec((tm, K), lambda i, j: (i, 0)), pl.BlockSpec((K, tn), lambda i, j: (0, j))]
    args = [x, w]
    if bias is not None:
        in_specs.append(pl.BlockSpec((1, tn), lambda i, j: (0, j)))
        args.append(bias.reshape(1, N))
    return pl.pallas_call(
        functools.partial(_linear_kernel, act, bias is not None),
        grid=(M // tm, N // tn),
        in_specs=in_specs,
        out_specs=pl.BlockSpec((tm, tn), lambda i, j: (i, j)),
        out_shape=jax.ShapeDtypeStruct((M, N), out_dtype),
        compiler_params=_cparams("parallel", "arbitrary"),
        name=name,
    )(*args)


def _attn_in_kernel(n_qk_tiles, n_q_tiles, x_ref, g_ref, sh_ref, sc_ref, w_ref, qg_ref, kg_ref, cos_ref, sin_ref,
                    o_ref, h_ref):
    j = pl.program_id(1)

    @pl.when(j == 0)
    def _():
        h_ref[...] = _normmod(x_ref[...], g_ref[...], sh_ref[...], sc_ref[...]).astype(_BF16)

    acc = _dot(h_ref[...], w_ref[...])
    tn = acc.shape[1]

    @pl.when(j < n_qk_tiles)
    def _():
        gain = jnp.where(j < n_q_tiles, qg_ref[...], kg_ref[...])
        cos, sin = cos_ref[...], sin_ref[...]
        for hd in range(tn // HEAD_DIM):
            a = acc[:, hd * HEAD_DIM:(hd + 1) * HEAD_DIM]
            y = a * lax.rsqrt(jnp.mean(a * a, axis=-1, keepdims=True) + NORM_EPS) * gain
            y = y * cos + pltpu.roll(y, HEAD_DIM // 2, 1) * sin
            o_ref[:, hd * HEAD_DIM:(hd + 1) * HEAD_DIM] = y.astype(o_ref.dtype)

    @pl.when(j >= n_qk_tiles)
    def _():
        o_ref[...] = acc.astype(o_ref.dtype)


def _attn_in_proj(x, g, shift, scale, w, qg, kg, cos, sin, lay, tm=512):
    M, D = x.shape
    N = w.shape[1]
    tm = _pick(lay.n, tm)
    tm = _pick(lay.B * lay.nc, tm)
    attn_w = N_ATTN_HEADS * HEAD_DIM
    kv_w = N_KV_HEADS * HEAD_DIM
    tn = math.gcd(math.gcd(attn_w, kv_w), 512)
    seg = _seg_index(lay, tm)
    vec = lambda: pl.BlockSpec((None, 1, D), lambda i, j: (seg(i), 0, 0))
    return pl.pallas_call(
        functools.partial(_attn_in_kernel, (attn_w + kv_w) // tn, attn_w // tn),
        grid=(M // tm, N // tn),
        in_specs=[
            pl.BlockSpec((tm, D), lambda i, j: (i, 0)),
            pl.BlockSpec((1, D), lambda i, j: (0, 0)),
            vec(), vec(),
            pl.BlockSpec((D, tn), lambda i, j: (0, j)),
            pl.BlockSpec((1, HEAD_DIM), lambda i, j: (0, 0)),
            pl.BlockSpec((1, HEAD_DIM), lambda i, j: (0, 0)),
            pl.BlockSpec((tm, HEAD_DIM), lambda i, j: (i, 0)),
            pl.BlockSpec((tm, HEAD_DIM), lambda i, j: (i, 0)),
        ],
        out_specs=pl.BlockSpec((tm, tn), lambda i, j: (i, j)),
        out_shape=jax.ShapeDtypeStruct((M, N), _BF16),
        scratch_shapes=[pltpu.VMEM((tm, D), _BF16)],
        compiler_params=_cparams("parallel", "arbitrary"),
        name="attn_in_proj",
    )(x, g, shift, scale, w, qg, kg, cos, sin)


def _attn_kernel(n_lat_tiles, g_per_kv, q_ref, kc_ref, vc_ref, kl_ref, vl_ref, o_ref):
    qi = pl.program_id(2)
    tq = q_ref.shape[0]
    q = jnp.concatenate([q_ref[:, g * HEAD_DIM:(g + 1) * HEAD_DIM] for g in range(g_per_kv)], axis=0)
    s_c = _dot_nt(q, kc_ref[...])

    def finish(o):
        for g in range(g_per_kv):
            o_ref[:, g * HEAD_DIM:(g + 1) * HEAD_DIM] = o[g * tq:(g + 1) * tq].astype(o_ref.dtype)

    @pl.when(qi < n_lat_tiles)
    def _():
        s_l = _dot_nt(q, kl_ref[...])
        m = jnp.maximum(jnp.max(s_c, axis=-1, keepdims=True), jnp.max(s_l, axis=-1, keepdims=True))
        p_c = jnp.exp(s_c - m)
        p_l = jnp.exp(s_l - m)
        l = jnp.sum(p_c, axis=-1, keepdims=True) + jnp.sum(p_l, axis=-1, keepdims=True)
        o = _dot(p_c.astype(_BF16), vc_ref[...]) + _dot(p_l.astype(_BF16), vl_ref[...])
        finish(o / l)

    @pl.when(qi >= n_lat_tiles)
    def _():
        m = jnp.max(s_c, axis=-1, keepdims=True)
        p_c = jnp.exp(s_c - m)
        l = jnp.sum(p_c, axis=-1, keepdims=True)
        finish(_dot(p_c.astype(_BF16), vc_ref[...]) / l)


def _attention(u, lay, tq=128):
    M = u.shape[0]
    g_per_kv = N_ATTN_HEADS // N_KV_HEADS
    qw = g_per_kv * HEAD_DIM
    attn_w = N_ATTN_HEADS * HEAD_DIM
    tq = _pick(lay.nc, tq)
    nlt, nct = lay.n // tq, lay.nc // tq
    k_blk = attn_w // HEAD_DIM
    v_blk = k_blk + N_KV_HEADS
    ctx_blk0 = lay.lat_rows // lay.nc

    def q_map(b, kv, qi):
        row = jnp.where(qi < nlt, b * nlt + qi, lay.lat_rows // tq + b * nct + (qi - nlt))
        return (row, kv)

    return pl.pallas_call(
        functools.partial(_attn_kernel, nlt, g_per_kv),
        grid=(lay.B, N_KV_HEADS, nlt + nct),
        in_specs=[
            pl.BlockSpec((tq, qw), q_map),
            pl.BlockSpec((lay.nc, HEAD_DIM), lambda b, kv, qi: (ctx_blk0 + b, k_blk + kv)),
            pl.BlockSpec((lay.nc, HEAD_DIM), lambda b, kv, qi: (ctx_blk0 + b, v_blk + kv)),
            pl.BlockSpec((lay.n, HEAD_DIM), lambda b, kv, qi: (b, k_blk + kv)),
            pl.BlockSpec((lay.n, HEAD_DIM), lambda b, kv, qi: (b, v_blk + kv)),
        ],
        out_specs=pl.BlockSpec((tq, qw), q_map),
        out_shape=jax.ShapeDtypeStruct((M, attn_w), _BF16),
        compiler_params=_cparams("parallel", "parallel", "arbitrary"),
        name="gqa_attention",
    )(u, u, u, u, u)


def _dft_kernel(f_ref, cc_ref, sc_ref, cn_ref, sn_ref, *refs):
    o_ref, a_ref, b_ref = refs[-3], refs[-2], refs[-1]

    @pl.when(pl.program_id(1) == 0)
    def _():
        f = f_ref[...]
        a_ref[...] = _dot(f, cc_ref[...]).astype(_BF16)
        b_ref[...] = _dot(f, sc_ref[...]).astype(_BF16)

    o_ref[...] = (_dot(cn_ref[...], a_ref[...]) - _dot(sn_ref[...], b_ref[...])).astype(o_ref.dtype)


def _dft_tables(n, group, n_groups):
    def cs(m):
        k = jnp.arange(m, dtype=jnp.int32)
        ang = (2.0 * math.pi / m) * ((k[:, None] * k[None, :]) % m).astype(_F32)
        return jnp.cos(ang), jnp.sin(ang)
    cn, sn = cs(n)
    scale = 1.0 / math.sqrt(n * group)
    cg, sg = cs(group)
    eye = jnp.eye(n_groups, dtype=_F32)
    return ((cn * scale).astype(_BF16), (sn * scale).astype(_BF16),
            jnp.kron(eye, cg).astype(_BF16), jnp.kron(eye, sg).astype(_BF16))


def _fourier_mix(u, lay, fw, prev=None, ctx=False, tm=512):
    M, N = u.shape
    n = lay.nc if ctx else lay.n
    tm = _pick(n, tm)
    nt = n // tm
    cn, sn, cc, sc = _dft_tables(n, fw // N_FOURIER_GROUPS, N_FOURIER_GROUPS)
    f_blk = N // fw - 1
    seg0 = lay.lat_rows // n if ctx else 0
    row0 = lay.lat_rows // tm if ctx else 0
    in_specs = [
        pl.BlockSpec((n, fw), lambda b, i: (seg0 + b, f_blk)),
        pl.BlockSpec((fw, fw), lambda b, i: (0, 0)),
        pl.BlockSpec((fw, fw), lambda b, i: (0, 0)),
        pl.BlockSpec((tm, n), lambda b, i: (i, 0)),
        pl.BlockSpec((tm, n), lambda b, i: (i, 0)),
    ]
    args = [u, cc, sc, cn, sn]
    aliases = {}
    if prev is not None:
        in_specs.append(pl.BlockSpec(memory_space=pl.ANY))
        args.append(prev)
        aliases = {5: 0}
    return pl.pallas_call(
        _dft_kernel,
        grid=(lay.B, nt),
        in_specs=in_specs,
        out_specs=pl.BlockSpec((tm, fw), lambda b, i: (row0 + b * nt + i, 0)),
        out_shape=jax.ShapeDtypeStruct((M, fw), _BF16),
        scratch_shapes=[pltpu.VMEM((n, fw), _BF16), pltpu.VMEM((n, fw), _BF16)],
        input_output_aliases=aliases,
        compiler_params=_cparams("parallel", "arbitrary"),
        name="fourier_mix_ctx" if ctx else "fourier_mix_lat",
    )(*args)


def _out_proj_kernel(n_lhs, *refs):
    lhs = refs[:n_lhs]
    ws = refs[n_lhs:2 * n_lhs]
    x_ref, gate_ref, o_ref = refs[2 * n_lhs], refs[2 * n_lhs + 1], refs[2 * n_lhs + 2]
    acc = _dot(lhs[0][...], ws[0][...])
    for a, w in zip(lhs[1:], ws[1:]):
        acc = acc + _dot(a[...], w[...])
    o_ref[...] = x_ref[...] + gate_ref[...] * acc


def _out_proj(lhs, ws, x, gate, lay, tm=512):
    M, D = x.shape
    tm = _pick(lay.n, tm)
    tm = _pick(lay.B * lay.nc, tm)
    seg = _seg_index(lay, tm)
    in_specs = [pl.BlockSpec((tm, a.shape[1]), lambda i: (i, 0)) for a in lhs]
    in_specs += [pl.BlockSpec(w.shape, lambda i: (0, 0)) for w in ws]
    in_specs += [pl.BlockSpec((tm, D), lambda i: (i, 0)), pl.BlockSpec((None, 1, D), lambda i: (seg(i), 0, 0))]
    return pl.pallas_call(
        functools.partial(_out_proj_kernel, len(lhs)),
        grid=(M // tm,),
        in_specs=in_specs,
        out_specs=pl.BlockSpec((tm, D), lambda i: (i, 0)),
        out_shape=jax.ShapeDtypeStruct((M, D), _F32),
        compiler_params=_cparams("parallel"),
        name="out_proj_residual",
    )(*lhs, *ws, x, gate)


def _ffn_kernel(lay, tm, halo, x_ref, xp_ref, xn_ref, g_ref, sh_ref, sc_ref, gate_ref, wg_ref, wv_ref,
                cwg_ref, cwv_ref, cbg_ref, cbv_ref, wd_ref, o_ref, h_ref, acc_ref, mp_ref, mn_ref):
    i, j = pl.program_id(0), pl.program_id(1)
    ext = tm + 2 * halo

    @pl.when(j == 0)
    def _():
        g, sh, sc = g_ref[...], sh_ref[...], sc_ref[...]
        h_ref[0:halo, :] = _normmod(xp_ref[...], g, sh, sc).astype(_BF16)
        h_ref[halo:halo + tm, :] = _normmod(x_ref[...], g, sh, sc).astype(_BF16)
        h_ref[halo + tm:ext, :] = _normmod(xn_ref[...], g, sh, sc).astype(_BF16)
        acc_ref[...] = jnp.zeros_like(acc_ref)
        has_prev, has_next = _seg_masks(i, tm, mp_ref.shape, lay)
        mp_ref[...] = has_prev.astype(_F32)
        mn_ref[...] = has_next.astype(_F32)

    h = h_ref[...]
    mp, mn = mp_ref[...], mn_ref[...]

    def conv(w_ref, cw_ref, cb_ref):
        u = _dot(h, w_ref[...])
        cw = cw_ref[...]
        up = pltpu.roll(u, 1, 0)[halo:halo + tm]
        un = pltpu.roll(u, ext - 1, 0)[halo:halo + tm]
        return up * mp * cw[0:1] + u[halo:halo + tm] * cw[1:2] + un * mn * cw[2:3] + cb_ref[...]

    gt = conv(wg_ref, cwg_ref, cbg_ref)
    vl = conv(wv_ref, cwv_ref, cbv_ref)
    act = (gt * _sigmoid(gt) * vl).astype(_BF16)
    acc_ref[...] += _dot(act, wd_ref[...])

    @pl.when(j == pl.num_programs(1) - 1)
    def _():
        o_ref[...] = x_ref[...] + gate_ref[...] * acc_ref[...]


def _ffn(x, g, shift, scale, gate, w_up, conv_w, conv_b, w_down, lay, n_rows, tm=512, tf=512):
    M, D = x.shape
    F = w_down.shape[0]
    tm = _pick(lay.n, tm)
    tm = _pick(lay.B * lay.nc, tm)
    tf = _pick(F, tf)
    nf = F // tf
    halo = BF16_ROWS
    per = tm // halo
    seg = _seg_index(lay, tm)
    vec = lambda: pl.BlockSpec((None, 1, D), lambda i, j: (seg(i), 0, 0))
    out = pl.pallas_call(
        functools.partial(_ffn_kernel, lay, tm, halo),
        grid=(n_rows // tm, nf),
        in_specs=[
            pl.BlockSpec((tm, D), lambda i, j: (i, 0)),
            pl.BlockSpec((halo, D), lambda i, j: (jnp.maximum(i * per - 1, 0), 0)),
            pl.BlockSpec((halo, D), lambda i, j: (jnp.minimum((i + 1) * per, M // halo - 1), 0)),
            pl.BlockSpec((1, D), lambda i, j: (0, 0)),
            vec(), vec(), vec(),
            pl.BlockSpec((D, tf), lambda i, j: (0, j)),
            pl.BlockSpec((D, tf), lambda i, j: (0, nf + j)),
            pl.BlockSpec((CONV_W, tf), lambda i, j: (0, j)),
            pl.BlockSpec((CONV_W, tf), lambda i, j: (0, nf + j)),
            pl.BlockSpec((1, tf), lambda i, j: (0, j)),
            pl.BlockSpec((1, tf), lambda i, j: (0, nf + j)),
            pl.BlockSpec((tf, D), lambda i, j: (j, 0)),
        ],
        out_specs=pl.BlockSpec((tm, D), lambda i, j: (i, 0)),
        out_shape=jax.ShapeDtypeStruct((n_rows, D), _F32),
        scratch_shapes=[
            pltpu.VMEM((tm + 2 * halo, D), _BF16),
            pltpu.VMEM((tm, D), _F32),
            pltpu.VMEM((tm, tf), _F32),
            pltpu.VMEM((tm, tf), _F32),
        ],
        compiler_params=_cparams("parallel", "arbitrary"),
        name="conv_gated_ffn",
    )(x, x, x, g, shift, scale, gate, w_up, w_up, conv_w, conv_w, conv_b, conv_b, w_down)
    return out


def _shift_mix_kernel(lay, tm, halo, x_ref, xp_ref, xn_ref, g_ref, sh_ref, sc_ref, mu_ref, *o_refs):
    i = pl.program_id(0)
    g, sh, sc = g_ref[...], sh_ref[...], sc_ref[...]
    h = _normmod(x_ref[...], g, sh, sc)
    hp = _normmod(xp_ref[...], g, sh, sc)
    hn = _normmod(xn_ref[...], g, sh, sc)
    ext = jnp.concatenate([hp, h, hn], axis=0)
    n_ext = tm + 2 * halo
    has_prev, has_next = _seg_masks(i, tm, h.shape, lay)
    prev = jnp.where(has_prev, pltpu.roll(ext, 1, 0)[halo:halo + tm], 0.0)
    nxt = jnp.where(has_next, pltpu.roll(ext, n_ext - 1, 0)[halo:halo + tm], 0.0)
    dx = 0.5 * (prev + nxt) - h
    mu = mu_ref[...]
    for m, o_ref in enumerate(o_refs):
        o_ref[...] = (h + dx * mu[m:m + 1]).astype(o_ref.dtype)


def _shift_mix(x, g, shift, scale, mu, lay, tm=256):
    M, D = x.shape
    tm = _pick(lay.n, tm)
    tm = _pick(lay.B * lay.nc, tm)
    halo = 8
    per = tm // halo
    seg = _seg_index(lay, tm)
    vec = lambda: pl.BlockSpec((None, 1, D), lambda i: (seg(i), 0, 0))
    n_mix = mu.shape[0]
    return pl.pallas_call(
        functools.partial(_shift_mix_kernel, lay, tm, halo),
        grid=(M // tm,),
        in_specs=[
            pl.BlockSpec((tm, D), lambda i: (i, 0)),
            pl.BlockSpec((halo, D), lambda i: (jnp.maximum(i * per - 1, 0), 0)),
            pl.BlockSpec((halo, D), lambda i: (jnp.minimum((i + 1) * per, M // halo - 1), 0)),
            pl.BlockSpec((1, D), lambda i: (0, 0)),
            vec(), vec(),
            pl.BlockSpec((n_mix, D), lambda i: (0, 0)),
        ],
        out_specs=[pl.BlockSpec((tm, D), lambda i: (i, 0)) for _ in range(n_mix)],
        out_shape=[jax.ShapeDtypeStruct((M, D), _BF16) for _ in range(n_mix)],
        compiler_params=_cparams("parallel"),
        name="rwkv_shift_mix",
    )(x, x, x, g, shift, scale, mu)


def _vres_kernel(x_ref, w_ref, b_ref, v_ref, vf_ref, o_ref):
    gate = _sigmoid(_dot(x_ref[...], w_ref[...]) + b_ref[...])
    v = v_ref[...]
    o_ref[...] = v + (vf_ref[...] - v) * gate


def _vres(xl, w2, v0, v, v_first, tm=512):
    M, K = xl.shape
    D = w2.shape[1]
    tm = _pick(M, tm)
    return pl.pallas_call(
        _vres_kernel,
        grid=(M // tm,),
        in_specs=[
            pl.BlockSpec((tm, K), lambda i: (i, 0)),
            pl.BlockSpec((K, D), lambda i: (0, 0)),
            pl.BlockSpec((1, D), lambda i: (0, 0)),
            pl.BlockSpec((tm, D), lambda i: (i, 0)),
            pl.BlockSpec((tm, D), lambda i: (i, 0)),
        ],
        out_specs=pl.BlockSpec((tm, D), lambda i: (i, 0)),
        out_shape=jax.ShapeDtypeStruct((M, D), _F32),
        compiler_params=_cparams("parallel"),
        name="rwkv_value_residual",
    )(xl, w2, v0.reshape(1, D), v, v_first)


def _wkv_kernel(reverse, final, C, LW, prec, r_ref, k_ref, v_ref, wp_ref, ap_ref, kk_ref, ka_ref, rk_ref, *refs):
    if final:
        y0_ref, b0_ref, g_ref, lng_ref, lnb_ref, o_ref, s_ref = refs
    else:
        y_ref, bon_ref, s_ref = refs
    j = pl.program_id(2)

    @pl.when(j == 0)
    def _():
        s_ref[...] = jnp.zeros_like(s_ref)

    row = lax.broadcasted_iota(jnp.int32, (C, C), 0)
    col = lax.broadcasted_iota(jnp.int32, (C, C), 1)
    tri = jnp.where((col >= row) if reverse else (col <= row), 1.0, 0.0).astype(_BF16)
    row2 = lax.broadcasted_iota(jnp.int32, (C, 2 * C), 0)
    col2 = lax.broadcasted_iota(jnp.int32, (C, 2 * C), 1)
    col2 = jnp.where(col2 >= C, col2 - C, col2)
    strict2 = (col2 > row2) if reverse else (col2 < row2)
    incl2 = (col2 >= row2) if reverse else (col2 <= row2)
    pr = lax.broadcasted_iota(jnp.int32, (LANES, LANES), 0) // RWKV_HEAD
    pc = lax.broadcasted_iota(jnp.int32, (LANES, LANES), 1) // RWKV_HEAD
    same_head = pr == pc
    head_ones = jnp.where(same_head, 1.0, 0.0).astype(_BF16)
    first_head = lax.broadcasted_iota(jnp.int32, (C, LANES), 1) < RWKV_HEAD
    eye = jnp.where(row == col, 1.0, 0.0).astype(_F32)
    n_pairs = LW // LANES
    per_pair = LANES // RWKV_HEAD
    pairs = range(n_pairs)
    heads = [(p, hh) for p in pairs for hh in range(per_pair)]
    sls = [slice(p * LANES, (p + 1) * LANES) for p in pairs]

    def cast(x):
        return x.astype(_BF16) if prec is None else x

    def mm(a, b):
        return _dot(cast(a), cast(b), prec)

    def mm_nt(a, b):
        return _dot_nt(cast(a), cast(b), prec)

    def mm_tn(a, b):
        return _dot_tn(cast(a), cast(b), prec)

    v = [v_ref[:, sl] for sl in sls]
    lw = []
    for sl in sls:
        wp = wp_ref[:, sl]
        sp = jnp.maximum(-wp, 0.0) + jnp.log(1.0 + jnp.exp(-jnp.abs(wp)))
        lw.append(-jnp.exp(-sp - 0.5))
    cl = [_ones_dot(tri, x, 3, left=True) for x in lw]
    total = [c[0:1] if reverse else c[C - 1:C] for c in cl]

    a = [_sigmoid(ap_ref[:, sl]) for sl in sls]
    kk = [k_ref[:, sl] * kk_ref[:, sl] for sl in sls]
    nrm = [jnp.sqrt(_ones_dot(head_ones, x * x, 2)) for x in kk]
    kk = [x / jnp.maximum(n, 1e-12) for x, n in zip(kk, nrm)]
    kd = [k_ref[:, sl] * (1.0 + (a[p] - 1.0) * ka_ref[:, sl]) for p, sl in enumerate(sls)]
    bonus = [_ones_dot(head_ones, r_ref[:, sl] * kd[p] * rk_ref[:, sl], 2) * v[p] for p, sl in enumerate(sls)]
    b = [kk[p] * a[p] for p in pairs]

    At = [cast(-kk[p] * jnp.exp(cl[p] - lw[p])) for p in pairs]
    Rt = [cast(r_ref[:, sl] * jnp.exp(cl[p])) for p, sl in enumerate(sls)]
    BK, BKp = [], []
    for p in pairs:
        e_neg = jnp.exp(-cl[p])
        e_rem = jnp.exp(total[p] - cl[p])
        BK.append(cast(jnp.concatenate([b[p] * e_neg, kd[p] * e_neg], axis=0)))
        BKp.append(cast(jnp.concatenate([b[p] * e_rem, kd[p] * e_rem], axis=0)))
    vb = [cast(x) for x in v]
    ZV0 = [jnp.concatenate([jnp.zeros_like(x), x], axis=0) for x in vb]
    S0 = [s_ref[p] for p in pairs]
    S0b = [cast(s) for s in S0]

    zero = jnp.zeros_like(At[0])
    M = []
    for p, hh in heads:
        mh = first_head if hh == 0 else jnp.logical_not(first_head)
        AR = jnp.concatenate([jnp.where(mh, At[p], zero), jnp.where(mh, Rt[p], zero)], axis=0)
        M.append(_dot_nt(AR, BK[p], prec))
    AS = [_dot_nt(At[p], S0b[p], prec) for p in pairs]
    RS = [_dot_nt(Rt[p], S0b[p], prec) for p in pairs]
    M_top = [jnp.where(strict2, m[:C], 0.0) for m in M]
    M_bot = [cast(jnp.where(incl2, m[C:], 0.0)) for m in M]

    Pw = [m[:, :C] for m in M_top]
    T = [eye + x for x in Pw]
    for _ in range(int(math.log2(C)) - 1):
        Pwb = [cast(x) for x in Pw]
        Pw = [_dot(x, x, prec) for x in Pwb]
        T = [t + mm(t, x) for t, x in zip(T, Pw)]

    X = [AS[p] + mm(M_top[i], ZV0[p]) for i, (p, hh) in enumerate(heads)]
    Zh = [mm(t, x) for t, x in zip(T, X)]
    Yh = [_dot(M_bot[i], jnp.concatenate([cast(Zh[i]), vb[p]], axis=0), prec) for i, (p, hh) in enumerate(heads)]
    Z = [jnp.where(first_head, Zh[per_pair * p], Zh[per_pair * p + 1]) for p in pairs]
    y = [RS[p] + jnp.where(first_head, Yh[per_pair * p], Yh[per_pair * p + 1]) for p in pairs]
    dS = [_dot_tn(jnp.concatenate([cast(Z[p]), vb[p]], axis=0), BKp[p], prec) for p in pairs]
    s_ref[...] = jnp.stack([S0[p] * jnp.exp(total[p]) + jnp.where(same_head, dS[p], 0.0) for p in pairs], axis=0)

    if final:
        inv = 1.0 / RWKV_HEAD
        wkv = [y[p] + y0_ref[:, sl] for p, sl in enumerate(sls)]
        cen = [x - _ones_dot(head_ones, x, 3) * inv for x in wkv]
        var = [_ones_dot(head_ones, x * x, 2) * inv for x in cen]
        outs = []
        for p, sl in enumerate(sls):
            normed = cen[p] * lax.rsqrt(var[p] + GN_EPS) * lng_ref[:, sl] + lnb_ref[:, sl]
            outs.append(((normed + bonus[p] + b0_ref[:, sl]) * g_ref[:, sl]).astype(o_ref.dtype))
        o_ref[...] = jnp.concatenate(outs, axis=1)
    else:
        y_ref[...] = jnp.concatenate(y, axis=1)
        bon_ref[...] = jnp.concatenate(bonus, axis=1)


def _wkv_kernel_depth_first(reverse, final, C, LW, prec, r_ref, k_ref, v_ref, wp_ref, ap_ref, kk_ref, ka_ref, rk_ref,
                            *refs):
    if final:
        y0_ref, b0_ref, g_ref, lng_ref, lnb_ref, o_ref, s_ref = refs
    else:
        y_ref, bon_ref, s_ref = refs
    j = pl.program_id(2)

    @pl.when(j == 0)
    def _():
        s_ref[...] = jnp.zeros_like(s_ref)

    row = lax.broadcasted_iota(jnp.int32, (C, C), 0)
    col = lax.broadcasted_iota(jnp.int32, (C, C), 1)
    tri = jnp.where((col >= row) if reverse else (col <= row), 1.0, 0.0).astype(_BF16)
    row2 = lax.broadcasted_iota(jnp.int32, (C, 2 * C), 0)
    col2 = lax.broadcasted_iota(jnp.int32, (C, 2 * C), 1)
    col2 = jnp.where(col2 >= C, col2 - C, col2)
    strict2 = (col2 > row2) if reverse else (col2 < row2)
    incl2 = (col2 >= row2) if reverse else (col2 <= row2)
    pr = lax.broadcasted_iota(jnp.int32, (LANES, LANES), 0) // RWKV_HEAD
    pc = lax.broadcasted_iota(jnp.int32, (LANES, LANES), 1) // RWKV_HEAD
    same_head = pr == pc
    head_ones = jnp.where(same_head, 1.0, 0.0).astype(_BF16)
    first_head = lax.broadcasted_iota(jnp.int32, (C, LANES), 1) < RWKV_HEAD
    eye = jnp.where(row == col, 1.0, 0.0).astype(_F32)

    def mm(a, b):
        return _dot(a.astype(_BF16), b.astype(_BF16)) if prec is None else _dot(a, b, prec)

    def mm_nt(a, b):
        return _dot_nt(a.astype(_BF16), b.astype(_BF16)) if prec is None else _dot_nt(a, b, prec)

    def mm_tn(a, b):
        return _dot_tn(a.astype(_BF16), b.astype(_BF16)) if prec is None else _dot_tn(a, b, prec)

    new_states, outs, bons = [], [], []
    for p in range(LW // LANES):
        sl = slice(p * LANES, (p + 1) * LANES)
        r, k, v = r_ref[:, sl], k_ref[:, sl], v_ref[:, sl]
        wp, ap = wp_ref[:, sl], ap_ref[:, sl]

        sp = jnp.maximum(-wp, 0.0) + jnp.log(1.0 + jnp.exp(-jnp.abs(wp)))
        lw = -jnp.exp(-sp - 0.5)
        cl = _ones_dot(tri, lw, 3, left=True)
        total = cl[0:1] if reverse else cl[C - 1:C]

        a = _sigmoid(ap)
        kk = k * kk_ref[:, sl]
        nrm = jnp.sqrt(_ones_dot(head_ones, kk * kk, 2))
        kk = kk / jnp.maximum(nrm, 1e-12)
        kd = k * (1.0 + (a - 1.0) * ka_ref[:, sl])
        bonus = _ones_dot(head_ones, r * kd * rk_ref[:, sl], 2) * v
        b = kk * a

        e_neg = jnp.exp(-cl)
        e_rem = jnp.exp(total - cl)
        At = -kk * jnp.exp(cl - lw)
        Rt = r * jnp.exp(cl)
        S0 = s_ref[p]
        AS = mm_nt(At, S0)
        RS = mm_nt(Rt, S0)
        BK = jnp.concatenate([b * e_neg, kd * e_neg], axis=0)
        ZV0 = jnp.concatenate([jnp.zeros_like(v), v], axis=0)
        Zs, Ys = [], []
        for hh in range(LANES // RWKV_HEAD):
            mh = first_head if hh == 0 else jnp.logical_not(first_head)
            AR = jnp.concatenate([jnp.where(mh, At, 0.0), jnp.where(mh, Rt, 0.0)], axis=0)
            M = mm_nt(AR, BK)
            M_top = jnp.where(strict2, M[:C], 0.0)
            M_bot = jnp.where(incl2, M[C:], 0.0)
            Lab = M_top[:, :C]
            T = eye + Lab
            Pw = Lab
            for _ in range(int(math.log2(C)) - 1):
                Pw = mm(Pw, Pw)
                T = T + mm(T, Pw)
            X = AS + mm(M_top, ZV0)
            Zh = mm(T, X)
            Yh = mm(M_bot, jnp.concatenate([Zh, v], axis=0))
            Zs.append(Zh)
            Ys.append(Yh)
        Z = jnp.where(first_head, Zs[0], Zs[1])
        y = RS + jnp.where(first_head, Ys[0], Ys[1])
        dS = mm_tn(jnp.concatenate([Z, v], axis=0), jnp.concatenate([b * e_rem, kd * e_rem], axis=0))
        new_states.append(S0 * jnp.exp(total) + jnp.where(same_head, dS, 0.0))

        if final:
            wkv = y + y0_ref[:, sl]
            inv = 1.0 / RWKV_HEAD
            cen = wkv - _ones_dot(head_ones, wkv, 3) * inv
            var = _ones_dot(head_ones, cen * cen, 2) * inv
            normed = cen * lax.rsqrt(var + GN_EPS) * lng_ref[:, sl] + lnb_ref[:, sl]
            outs.append(((normed + bonus + b0_ref[:, sl]) * g_ref[:, sl]).astype(o_ref.dtype))
        else:
            outs.append(y)
            bons.append(bonus)

    s_ref[...] = jnp.stack(new_states, axis=0)
    if final:
        o_ref[...] = jnp.concatenate(outs, axis=1)
    else:
        y_ref[...] = jnp.concatenate(outs, axis=1)
        bon_ref[...] = jnp.concatenate(bons, axis=1)


def _wkv(r, k, v, wpre, apre, k_k, k_a, r_k, lay, reverse, final_args=None, prec=None, lw=1024):
    M, D = r.shape
    C = _pick(lay.nc, WKV_CHUNK)
    lw = _pick(D, lw)
    ncc, ncl = lay.nc // C, lay.n // C
    ctx_blk0 = lay.lat_rows // C

    def row_blk(b, j):
        if reverse:
            return jnp.where(j < ncc, ctx_blk0 + b * ncc + (ncc - 1 - j), b * ncl + (ncl - 1 - (j - ncc)))
        return jnp.where(j < ncc, ctx_blk0 + b * ncc + j, b * ncl + (j - ncc))

    tile = lambda: pl.BlockSpec((C, lw), lambda b, h, j: (row_blk(b, j), h))
    vec = lambda: pl.BlockSpec((1, lw), lambda b, h, j: (0, h))
    in_specs = [tile() for _ in range(5)] + [vec() for _ in range(3)]
    args = [r, k, v, wpre, apre, k_k.reshape(1, D), k_a.reshape(1, D), r_k.reshape(1, D)]
    final = final_args is not None
    if final:
        y0, b0, g, lng, lnb = final_args
        in_specs += [tile(), tile(), tile(), vec(), vec()]
        args += [y0, b0, g, lng.reshape(1, D), lnb.reshape(1, D)]
        out_specs = tile()
        out_shape = jax.ShapeDtypeStruct((M, D), _BF16)
    else:
        out_specs = [tile(), tile()]
        out_shape = [jax.ShapeDtypeStruct((M, D), _F32), jax.ShapeDtypeStruct((M, D), _F32)]
    return pl.pallas_call(
        functools.partial(_wkv_kernel, reverse, final, C, lw, prec),
        grid=(lay.B, D // lw, ncc + ncl),
        in_specs=in_specs,
        out_specs=out_specs,
        out_shape=out_shape,
        scratch_shapes=[pltpu.VMEM((lw // LANES, LANES, LANES), _F32)],
        compiler_params=_cparams("parallel", "parallel", "arbitrary"),
        name="wkv7_scan_rev" if reverse else "wkv7_scan_fwd",
    )(*args)


def _final_norm_kernel(x_ref, g_ref, o_ref):
    x = x_ref[...]
    o_ref[...] = x * lax.rsqrt(jnp.mean(x * x, axis=-1, keepdims=True) + NORM_EPS) * g_ref[...]


def _final_norm(x, g, n_rows, tm=512):
    D = x.shape[1]
    tm = _pick(n_rows, tm)
    return pl.pallas_call(
        _final_norm_kernel,
        grid=(n_rows // tm,),
        in_specs=[pl.BlockSpec((tm, D), lambda i: (i, 0)), pl.BlockSpec((1, D), lambda i: (0, 0))],
        out_specs=pl.BlockSpec((tm, D), lambda i: (i, 0)),
        out_shape=jax.ShapeDtypeStruct((n_rows, D), _F32),
        compiler_params=_cparams("parallel"),
        name="final_rms_norm",
    )(x, g.reshape(1, D))


def _rope_tables(lay):
    n = lay.n
    t = jnp.arange(n, dtype=jnp.int32)
    row = (t // GRID_W).astype(_F32)
    col = (t % GRID_W).astype(_F32)
    axis_dim = HEAD_DIM // 2
    inv_freq = ROPE_THETA ** (-jnp.arange(0, axis_dim, 2, dtype=_F32) / axis_dim)
    ang = jnp.concatenate([row[:, None] * inv_freq, col[:, None] * inv_freq], axis=-1)
    cos, sin = jnp.cos(ang), jnp.sin(ang)
    cos_l = jnp.concatenate([cos, cos], axis=-1)
    sin_l = jnp.concatenate([-sin, sin], axis=-1)
    n_ctx_rows = lay.B * lay.nc
    cos_f = jnp.concatenate([jnp.tile(cos_l, (lay.B, 1)), jnp.ones((n_ctx_rows, HEAD_DIM), _F32)], axis=0)
    sin_f = jnp.concatenate([jnp.tile(sin_l, (lay.B, 1)), jnp.zeros((n_ctx_rows, HEAD_DIM), _F32)], axis=0)
    return cos_f, sin_f


def _deinterleave_heads(w, n_heads):
    lead = w.shape[:-1]
    w = w.reshape(*lead, n_heads, HEAD_DIM // 2, 2)
    w = jnp.swapaxes(w, -1, -2)
    return w.reshape(*lead, n_heads * HEAD_DIM)


def _pad_cols(w, mult=LANES):
    pad = (-w.shape[-1]) % mult
    return jnp.pad(w, [(0, 0)] * (w.ndim - 1) + [(0, pad)]) if pad else w


def _pad_rows(w, mult=LANES):
    pad = (-w.shape[-2]) % mult
    return jnp.pad(w, [(0, 0)] * (w.ndim - 2) + [(0, pad), (0, 0)]) if pad else w


def kernel(x, c, ctx, c_ctx, w_mod, b_mod, norm1_g, norm2_g, attn_w_in, attn_w_out, q_norm_g, k_norm_g, rwkv_mu, rwkv_w_r, rwkv_w_k, rwkv_w_v, rwkv_w_o, rwkv_decay_w0, rwkv_decay_w1, rwkv_decay_w2, rwkv_iclr_a0, rwkv_iclr_a1, rwkv_iclr_a2, rwkv_gate_g1, rwkv_gate_g2, rwkv_k_k, rwkv_k_a, rwkv_r_k, rwkv_lnx_g, rwkv_lnx_b, rwkv_vres_v0, rwkv_vres_v1, rwkv_vres_v2, ffn_w_up, ffn_conv_w, ffn_conv_b, ffn_w_down, final_norm_g):
    B, n, D = x.shape
    nc = ctx.shape[1]
    depth = w_mod.shape[0]
    lay = Layout(B, n, nc, B * n, B * (n + nc))
    attn_w = N_ATTN_HEADS * HEAD_DIM
    kv_w = N_KV_HEADS * HEAD_DIM
    fw = D - attn_w

    xs = jnp.concatenate([x.reshape(B * n, D), ctx.reshape(B * nc, D)], axis=0)

    cvec = jnp.concatenate([c, c_ctx[None, :], jnp.zeros((7 - B % 8, D), _F32)], axis=0)
    mods = _modulation(cvec, w_mod, b_mod)[:, :B + 1]
    mods = mods.reshape(depth, B + 1, 6, 1, D)
    cos_f, sin_f = _rope_tables(lay)
    qg_scale = HEAD_DIM ** -0.5

    v_first = None
    for i in range(depth):
        last = i == depth - 1
        jl = i // 2
        m = [mods[i, :, t] for t in range(6)]
        g1 = norm1_g[i].reshape(1, D)
        if i % 2 == 0:
            w_in = attn_w_in[jl]
            w_in = jnp.concatenate([
                _deinterleave_heads(w_in[:, :attn_w], N_ATTN_HEADS),
                _deinterleave_heads(w_in[:, attn_w:attn_w + kv_w], N_KV_HEADS),
                w_in[:, attn_w + kv_w:]], axis=1).astype(_BF16)
            qg = (_deinterleave_heads(q_norm_g[jl], 1) * qg_scale).reshape(1, HEAD_DIM)
            kg = _deinterleave_heads(k_norm_g[jl], 1).reshape(1, HEAD_DIM)
            u = _attn_in_proj(xs, g1, m[0], m[1], w_in, qg, kg, cos_f, sin_f, lay)
            o = _attention(u, lay)
            fm = _fourier_mix(u, lay, fw)
            fm = _fourier_mix(u, lay, fw, prev=fm, ctx=True)
            w_out = attn_w_out[jl].astype(_BF16)
            xs = _out_proj([o, fm], [w_out[:attn_w], w_out[attn_w:]], xs, m[2], lay)
        else:
            bf = lambda w: w.astype(_BF16)
            x_r, x_w, x_k, x_v, x_a, x_g = _shift_mix(xs, g1, m[0], m[1], rwkv_mu[jl], lay)
            r = _linear(x_r, bf(rwkv_w_r[jl]), name="rwkv_r")
            k = _linear(x_k, bf(rwkv_w_k[jl]), name="rwkv_k")
            v = _linear(x_v, bf(rwkv_w_v[jl]), name="rwkv_v")
            if jl == 0:
                v_first = v
            else:
                vl = _linear(x_v, bf(_pad_cols(rwkv_vres_v1[jl - 1])), out_dtype=_BF16, name="rwkv_vres_lora")
                v = _vres(vl, bf(_pad_rows(rwkv_vres_v2[jl - 1])), rwkv_vres_v0[jl - 1], v, v_first)
            gl = _linear(x_g, bf(_pad_cols(rwkv_gate_g1[jl])), act="sigmoid", out_dtype=_BF16, name="rwkv_gate_lora")
            g = _linear(gl, bf(_pad_rows(rwkv_gate_g2[jl])), name="rwkv_gate")
            pre = []
            for d in range(2):
                dl = _linear(x_w, bf(_pad_cols(rwkv_decay_w1[jl, d])), act="tanh", out_dtype=_BF16,
                             name="rwkv_decay_lora")
                wpre = _linear(dl, bf(_pad_rows(rwkv_decay_w2[jl, d])), bias=rwkv_decay_w0[jl, d], name="rwkv_decay")
                al = _linear(x_a, bf(_pad_cols(rwkv_iclr_a1[jl, d])), out_dtype=_BF16, name="rwkv_iclr_lora")
                apre = _linear(al, bf(_pad_rows(rwkv_iclr_a2[jl, d])), bias=rwkv_iclr_a0[jl, d], name="rwkv_iclr")
                pre.append((wpre, apre))
            kk_, ka_, rk_ = rwkv_k_k[jl], rwkv_k_a[jl], rwkv_r_k[jl].reshape(D)
            y0, bon0 = _wkv(r, k, v, pre[0][0], pre[0][1], kk_, ka_, rk_, lay, reverse=False)
            z = _wkv(r, k, v, pre[1][0], pre[1][1], kk_, ka_, rk_, lay, reverse=True,
                     final_args=(y0, bon0, g, rwkv_lnx_g[jl], rwkv_lnx_b[jl]))
            xs = _out_proj([z], [bf(rwkv_w_o[jl])], xs, m[2], lay)
        n_rows = lay.lat_rows if last else lay.rows
        ffn_out = _ffn(xs, norm2_g[i].reshape(1, D), m[3], m[4], m[5], ffn_w_up[i].astype(_BF16), ffn_conv_w[i],
                       ffn_conv_b[i].reshape(1, -1), ffn_w_down[i].astype(_BF16), lay, n_rows)
        xs = ffn_out
    out = _final_norm(xs, final_norm_g, lay.lat_rows)
    return out.reshape(B, n, D)
```

```python
import collections
import functools
import math

import jax
import jax.numpy as jnp
from jax import lax
from jax.experimental import pallas as pl
from jax.experimental.pallas import tpu as pltpu

HEAD_DIM = 128
N_ATTN_HEADS = 12
N_KV_HEADS = 4
GRID_W = 64
ROPE_THETA = 10000.0
N_FOURIER_GROUPS = 4
RWKV_HEAD = 64
CONV_W = 3
NORM_EPS = 1e-6
GN_EPS = 64e-5

LANES = 128
BF16_ROWS = 16
VMEM_LIMIT = 56 * 1024 * 1024
WKV_CHUNK = 64
WKV_GROUP = 256

_F32 = jnp.float32
_BF16 = jnp.bfloat16
_HI = lax.Precision.HIGHEST

Layout = collections.namedtuple("Layout", "B n nc lat_rows rows")


def _cparams(*sem):
    return pltpu.CompilerParams(dimension_semantics=sem, vmem_limit_bytes=VMEM_LIMIT)


def _pick(total, pref):
    t = min(pref, total)
    while total % t:
        t -= 1
    return t


def _seg_index(lay, tm):
    def f(i):
        r0 = i * tm
        return jnp.where(r0 < lay.lat_rows, r0 // lay.n, lay.B)
    return f


def _seg_masks(i, tm, shape, lay):
    t = lax.broadcasted_iota(jnp.int32, shape, 0)
    r0 = i * tm
    is_lat = r0 < lay.lat_rows
    if tm >= lay.nc:
        pc = lax.rem(t, lay.nc)
    else:
        pc = t + lax.rem(r0 - lay.lat_rows, lay.nc)
    pos = jnp.where(is_lat, t + lax.rem(r0, lay.n), pc)
    last = jnp.where(is_lat, lay.n - 1, lay.nc - 1)
    return pos != 0, pos != last


def _normmod(x, g, shift, scale):
    ms = jnp.mean(x * x, axis=-1, keepdims=True)
    return (x * lax.rsqrt(ms + NORM_EPS)) * g * (1.0 + scale) + shift


def _sigmoid(x):
    return 1.0 / (1.0 + jnp.exp(-x))


def _dot(a, b, prec=None):
    return jnp.dot(a, b, preferred_element_type=_F32, precision=prec)


def _dot_nt(a, b, prec=None):
    return lax.dot_general(a, b, (((1,), (1,)), ((), ())), preferred_element_type=_F32, precision=prec)


def _dot_tn(a, b, prec=None):
    return lax.dot_general(a, b, (((0,), (0,)), ((), ())), preferred_element_type=_F32, precision=prec)


def _ones_dot(ones, x, parts, left=False):
    acc = None
    rem = x
    for i in range(parts):
        piece = rem.astype(_BF16)
        if i + 1 < parts:
            rem = rem - piece.astype(_F32)
        d = _dot(ones, piece) if left else _dot(piece, ones)
        acc = d if acc is None else acc + d
    return acc


def _mod_kernel(c_ref, w_ref, b_ref, o_ref):
    c = c_ref[...]
    s = c * _sigmoid(c)
    o_ref[...] = _dot(s, w_ref[...], _HI) + b_ref[...]


def _modulation(cvec, w_mod, b_mod):
    depth, D, N = w_mod.shape
    rows = cvec.shape[0]
    tn = _pick(N, 512)
    return pl.pallas_call(
        _mod_kernel,
        grid=(depth, N // tn),
        in_specs=[
            pl.BlockSpec((rows, D), lambda l, j: (0, 0)),
            pl.BlockSpec((None, D, tn), lambda l, j: (l, 0, j)),
            pl.BlockSpec((None, 1, tn), lambda l, j: (l, 0, j)),
        ],
        out_specs=pl.BlockSpec((None, rows, tn), lambda l, j: (l, 0, j)),
        out_shape=jax.ShapeDtypeStruct((depth, rows, N), _F32),
        compiler_params=_cparams("parallel", "parallel"),
        name="adaln_modulation",
    )(cvec, w_mod, b_mod.reshape(depth, 1, N))


def _linear_kernel(act, has_bias, *refs):
    x_ref, w_ref = refs[0], refs[1]
    o_ref = refs[-1]
    acc = _dot(x_ref[...], w_ref[...])
    if has_bias:
        acc = acc + refs[2][...]
    if act == "tanh":
        acc = jnp.tanh(acc)
    elif act == "sigmoid":
        acc = _sigmoid(acc)
    o_ref[...] = acc.astype(o_ref.dtype)


def _linear(x, w, bias=None, act=None, out_dtype=_F32, tm=512, tn=2048, name="linear"):
    M, K = x.shape
    N = w.shape[1]
    tm, tn = _pick(M, tm), _pick(N, tn)
    in_specs = [pl.BlockSpec((tm, K), lambda i, j: (i, 0)), pl.BlockSpec((K, tn), lambda i, j: (0, j))]
    args = [x, w]
    if bias is not None:
        in_specs.append(pl.BlockSpec((1, tn), lambda i, j: (0, j)))
        args.append(bias.reshape(1, N))
    return pl.pallas_call(
        functools.partial(_linear_kernel, act, bias is not None),
        grid=(M // tm, N // tn),
        in_specs=in_specs,
        out_specs=pl.BlockSpec((tm, tn), lambda i, j: (i, j)),
        out_shape=jax.ShapeDtypeStruct((M, N), out_dtype),
        compiler_params=_cparams("parallel", "arbitrary"),
        name=name,
    )(*args)


def _attn_in_kernel(n_qk_tiles, n_q_tiles, x_ref, g_ref, sh_ref, sc_ref, w_ref, qg_ref, kg_ref, cos_ref, sin_ref,
                    o_ref, h_ref):
    j = pl.program_id(1)

    @pl.when(j == 0)
    def _():
        h_ref[...] = _normmod(x_ref[...], g_ref[...], sh_ref[...], sc_ref[...]).astype(_BF16)

    acc = _dot(h_ref[...], w_ref[...])
    tn = acc.shape[1]

    @pl.when(j < n_qk_tiles)
    def _():
        gain = jnp.where(j < n_q_tiles, qg_ref[...], kg_ref[...])
        cos, sin = cos_ref[...], sin_ref[...]
        for hd in range(tn // HEAD_DIM):
            a = acc[:, hd * HEAD_DIM:(hd + 1) * HEAD_DIM]
            y = a * lax.rsqrt(jnp.mean(a * a, axis=-1, keepdims=True) + NORM_EPS) * gain
            y = y * cos + pltpu.roll(y, HEAD_DIM // 2, 1) * sin
            o_ref[:, hd * HEAD_DIM:(hd + 1) * HEAD_DIM] = y.astype(o_ref.dtype)

    @pl.when(j >= n_qk_tiles)
    def _():
        o_ref[...] = acc.astype(o_ref.dtype)


def _attn_in_proj(x, g, shift, scale, w, qg, kg, cos, sin, lay, tm=512):
    M, D = x.shape
    N = w.shape[1]
    tm = _pick(lay.n, tm)
    tm = _pick(lay.B * lay.nc, tm)
    attn_w = N_ATTN_HEADS * HEAD_DIM
    kv_w = N_KV_HEADS * HEAD_DIM
    tn = math.gcd(math.gcd(attn_w, kv_w), 512)
    seg = _seg_index(lay, tm)
    vec = lambda: pl.BlockSpec((None, 1, D), lambda i, j: (seg(i), 0, 0))
    return pl.pallas_call(
        functools.partial(_attn_in_kernel, (attn_w + kv_w) // tn, attn_w // tn),
        grid=(M // tm, N // tn),
        in_specs=[
            pl.BlockSpec((tm, D), lambda i, j: (i, 0)),
            pl.BlockSpec((1, D), lambda i, j: (0, 0)),
            vec(), vec(),
            pl.BlockSpec((D, tn), lambda i, j: (0, j)),
            pl.BlockSpec((1, HEAD_DIM), lambda i, j: (0, 0)),
            pl.BlockSpec((1, HEAD_DIM), lambda i, j: (0, 0)),
            pl.BlockSpec((tm, HEAD_DIM), lambda i, j: (i, 0)),
            pl.BlockSpec((tm, HEAD_DIM), lambda i, j: (i, 0)),
        ],
        out_specs=pl.BlockSpec((tm, tn), lambda i, j: (i, j)),
        out_shape=jax.ShapeDtypeStruct((M, N), _BF16),
        scratch_shapes=[pltpu.VMEM((tm, D), _BF16)],
        compiler_params=_cparams("parallel", "arbitrary"),
        name="attn_in_proj",
    )(x, g, shift, scale, w, qg, kg, cos, sin)


def _attn_kernel(n_lat_tiles, g_per_kv, kb, q_ref, kc_ref, vc_ref, kl_ref, vl_ref, o_ref):
    qi = pl.program_id(2)
    tq = q_ref.shape[0]
    q = jnp.concatenate([q_ref[:, g * HEAD_DIM:(g + 1) * HEAD_DIM] for g in range(g_per_kv)], axis=0)

    def finish(acc, l):
        o = acc / l
        for g in range(g_per_kv):
            o_ref[:, g * HEAD_DIM:(g + 1) * HEAD_DIM] = o[g * tq:(g + 1) * tq].astype(o_ref.dtype)

    s_c = _dot_nt(q, kc_ref[...])
    m = jnp.max(s_c, axis=-1, keepdims=True)
    p = jnp.exp(s_c - m)
    l = jnp.sum(p, axis=-1, keepdims=True)
    acc = _dot(p.astype(_BF16), vc_ref[...])

    @pl.when(qi < n_lat_tiles)
    def _():
        n_blocks = kl_ref.shape[0] // kb
        mj, lj, accj = m, l, acc
        s_next = _dot_nt(q, kl_ref[0:kb, :])
        for j in range(n_blocks):
            s = s_next
            if j + 1 < n_blocks:
                s_next = _dot_nt(q, kl_ref[(j + 1) * kb:(j + 2) * kb, :])
            m_new = jnp.maximum(mj, jnp.max(s, axis=-1, keepdims=True))
            alpha = jnp.exp(mj - m_new)
            pj = jnp.exp(s - m_new)
            lj = lj * alpha + jnp.sum(pj, axis=-1, keepdims=True)
            accj = accj * alpha + _dot(pj.astype(_BF16), vl_ref[j * kb:(j + 1) * kb, :])
            mj = m_new
        finish(accj, lj)

    @pl.when(qi >= n_lat_tiles)
    def _():
        finish(acc, l)


def _attention(u, lay, tq=256, kb=512):
    M = u.shape[0]
    g_per_kv = N_ATTN_HEADS // N_KV_HEADS
    qw = g_per_kv * HEAD_DIM
    attn_w = N_ATTN_HEADS * HEAD_DIM
    tq = _pick(lay.nc, tq)
    nlt, nct = lay.n // tq, lay.nc // tq
    k_blk = attn_w // HEAD_DIM
    v_blk = k_blk + N_KV_HEADS
    ctx_blk0 = lay.lat_rows // lay.nc

    def q_map(b, kv, qi):
        row = jnp.where(qi < nlt, b * nlt + qi, lay.lat_rows // tq + b * nct + (qi - nlt))
        return (row, kv)

    return pl.pallas_call(
        functools.partial(_attn_kernel, nlt, g_per_kv, _pick(lay.n, kb)),
        grid=(lay.B, N_KV_HEADS, nlt + nct),
        in_specs=[
            pl.BlockSpec((tq, qw), q_map),
            pl.BlockSpec((lay.nc, HEAD_DIM), lambda b, kv, qi: (ctx_blk0 + b, k_blk + kv)),
            pl.BlockSpec((lay.nc, HEAD_DIM), lambda b, kv, qi: (ctx_blk0 + b, v_blk + kv)),
            pl.BlockSpec((lay.n, HEAD_DIM), lambda b, kv, qi: (b, k_blk + kv)),
            pl.BlockSpec((lay.n, HEAD_DIM), lambda b, kv, qi: (b, v_blk + kv)),
        ],
        out_specs=pl.BlockSpec((tq, qw), q_map),
        out_shape=jax.ShapeDtypeStruct((M, attn_w), _BF16),
        compiler_params=_cparams("parallel", "parallel", "arbitrary"),
        name="gqa_attention",
    )(u, u, u, u, u)


def _dft_kernel(f_ref, cc_ref, sc_ref, cn_ref, sn_ref, *refs):
    o_ref, a_ref, b_ref = refs[-3], refs[-2], refs[-1]

    @pl.when(pl.program_id(1) == 0)
    def _():
        f = f_ref[...]
        a_ref[...] = _dot(f, cc_ref[...]).astype(_BF16)
        b_ref[...] = _dot(f, sc_ref[...]).astype(_BF16)

    o_ref[...] = (_dot(cn_ref[...], a_ref[...]) - _dot(sn_ref[...], b_ref[...])).astype(o_ref.dtype)


def _dft_tables(n, group, n_groups):
    def cs(m):
        k = jnp.arange(m, dtype=jnp.int32)
        ang = (2.0 * math.pi / m) * ((k[:, None] * k[None, :]) % m).astype(_F32)
        return jnp.cos(ang), jnp.sin(ang)
    cn, sn = cs(n)
    scale = 1.0 / math.sqrt(n * group)
    cg, sg = cs(group)
    eye = jnp.eye(n_groups, dtype=_F32)
    return ((cn * scale).astype(_BF16), (sn * scale).astype(_BF16),
            jnp.kron(eye, cg).astype(_BF16), jnp.kron(eye, sg).astype(_BF16))


def _fourier_mix(u, lay, fw, prev=None, ctx=False, tm=512):
    M, N = u.shape
    n = lay.nc if ctx else lay.n
    tm = _pick(n, tm)
    nt = n // tm
    cn, sn, cc, sc = _dft_tables(n, fw // N_FOURIER_GROUPS, N_FOURIER_GROUPS)
    f_blk = N // fw - 1
    seg0 = lay.lat_rows // n if ctx else 0
    row0 = lay.lat_rows // tm if ctx else 0
    in_specs = [
        pl.BlockSpec((n, fw), lambda b, i: (seg0 + b, f_blk)),
        pl.BlockSpec((fw, fw), lambda b, i: (0, 0)),
        pl.BlockSpec((fw, fw), lambda b, i: (0, 0)),
        pl.BlockSpec((tm, n), lambda b, i: (i, 0)),
        pl.BlockSpec((tm, n), lambda b, i: (i, 0)),
    ]
    args = [u, cc, sc, cn, sn]
    aliases = {}
    if prev is not None:
        in_specs.append(pl.BlockSpec(memory_space=pl.ANY))
        args.append(prev)
        aliases = {5: 0}
    return pl.pallas_call(
        _dft_kernel,
        grid=(lay.B, nt),
        in_specs=in_specs,
        out_specs=pl.BlockSpec((tm, fw), lambda b, i: (row0 + b * nt + i, 0)),
        out_shape=jax.ShapeDtypeStruct((M, fw), _BF16),
        scratch_shapes=[pltpu.VMEM((n, fw), _BF16), pltpu.VMEM((n, fw), _BF16)],
        input_output_aliases=aliases,
        compiler_params=_cparams("parallel", "arbitrary"),
        name="fourier_mix_ctx" if ctx else "fourier_mix_lat",
    )(*args)


def _out_proj_kernel(n_lhs, *refs):
    lhs = refs[:n_lhs]
    ws = refs[n_lhs:2 * n_lhs]
    x_ref, gate_ref, o_ref = refs[2 * n_lhs], refs[2 * n_lhs + 1], refs[2 * n_lhs + 2]
    acc = _dot(lhs[0][...], ws[0][...])
    for a, w in zip(lhs[1:], ws[1:]):
        acc = acc + _dot(a[...], w[...])
    o_ref[...] = x_ref[...] + gate_ref[...] * acc


def _out_proj(lhs, ws, x, gate, lay, tm=512):
    M, D = x.shape
    tm = _pick(lay.n, tm)
    tm = _pick(lay.B * lay.nc, tm)
    seg = _seg_index(lay, tm)
    in_specs = [pl.BlockSpec((tm, a.shape[1]), lambda i: (i, 0)) for a in lhs]
    in_specs += [pl.BlockSpec(w.shape, lambda i: (0, 0)) for w in ws]
    in_specs += [pl.BlockSpec((tm, D), lambda i: (i, 0)), pl.BlockSpec((None, 1, D), lambda i: (seg(i), 0, 0))]
    return pl.pallas_call(
        functools.partial(_out_proj_kernel, len(lhs)),
        grid=(M // tm,),
        in_specs=in_specs,
        out_specs=pl.BlockSpec((tm, D), lambda i: (i, 0)),
        out_shape=jax.ShapeDtypeStruct((M, D), _F32),
        compiler_params=_cparams("parallel"),
        name="out_proj_residual",
    )(*lhs, *ws, x, gate)


def _ffn_kernel(tiles_per_seg, tm, halo, x_ref, xp_ref, xn_ref, g_ref, sh_ref, sc_ref, gate_ref, wg_ref, wv_ref,
                cwg_ref, cwv_ref, cbg_ref, cbv_ref, wd_ref, *refs):
    o_ref, h_ref, acc_ref = refs[-3:]
    i, j = pl.program_id(0), pl.program_id(1)
    ext = tm + 2 * halo

    @pl.when(j == 0)
    def _():
        g, sh, sc = g_ref[...], sh_ref[...], sc_ref[...]
        pos = lax.rem(i, tiles_per_seg)
        hp = _normmod(xp_ref[...], g, sh, sc)
        hn = _normmod(xn_ref[...], g, sh, sc)
        h_ref[0:halo, :] = jnp.where(pos != 0, hp, 0.0).astype(_BF16)
        h_ref[halo:halo + tm, :] = _normmod(x_ref[...], g, sh, sc).astype(_BF16)
        h_ref[halo + tm:ext, :] = jnp.where(pos != tiles_per_seg - 1, hn, 0.0).astype(_BF16)
        acc_ref[...] = jnp.zeros_like(acc_ref)

    h = h_ref[...]

    def conv(w_ref, cw_ref, cb_ref):
        u = _dot(h, w_ref[...])
        cw = cw_ref[...]
        up = pltpu.roll(u, 1, 0)[halo:halo + tm]
        un = pltpu.roll(u, ext - 1, 0)[halo:halo + tm]
        return up * cw[0:1] + u[halo:halo + tm] * cw[1:2] + un * cw[2:3] + cb_ref[...]

    gt = conv(wg_ref, cwg_ref, cbg_ref)
    vl = conv(wv_ref, cwv_ref, cbv_ref)
    act = (gt * _sigmoid(gt) * vl).astype(_BF16)
    acc_ref[...] += _dot(act, wd_ref[...])

    @pl.when(j == pl.num_programs(1) - 1)
    def _():
        o_ref[...] = x_ref[...] + gate_ref[...] * acc_ref[...]


def _ffn(x, g, shift, scale, gate, w_up, conv_w, conv_b, w_down, lay, ctx=False, prev=None, tm=512, tf=512):
    M, D = x.shape
    F = w_down.shape[0]
    seg_len = lay.nc if ctx else lay.n
    tm = _pick(seg_len, tm)
    tiles_per_seg = seg_len // tm
    row0 = lay.lat_rows // tm if ctx else 0
    tf = _pick(F, tf)
    nf = F // tf
    halo = BF16_ROWS
    per = tm // halo
    mod_idx = (lambda i: lay.B) if ctx else (lambda i: i // tiles_per_seg)
    vec = lambda: pl.BlockSpec((None, 1, D), lambda i, j: (mod_idx(i), 0, 0))
    in_specs = [
        pl.BlockSpec((tm, D), lambda i, j: (row0 + i, 0)),
        pl.BlockSpec((halo, D), lambda i, j: (jnp.maximum((row0 + i) * per - 1, 0), 0)),
        pl.BlockSpec((halo, D), lambda i, j: (jnp.minimum((row0 + i + 1) * per, M // halo - 1), 0)),
        pl.BlockSpec((1, D), lambda i, j: (0, 0)),
        vec(), vec(), vec(),
        pl.BlockSpec((D, tf), lambda i, j: (0, j)),
        pl.BlockSpec((D, tf), lambda i, j: (0, nf + j)),
        pl.BlockSpec((CONV_W, tf), lambda i, j: (0, j)),
        pl.BlockSpec((CONV_W, tf), lambda i, j: (0, nf + j)),
        pl.BlockSpec((1, tf), lambda i, j: (0, j)),
        pl.BlockSpec((1, tf), lambda i, j: (0, nf + j)),
        pl.BlockSpec((tf, D), lambda i, j: (j, 0)),
    ]
    args = [x, x, x, g, shift, scale, gate, w_up, w_up, conv_w, conv_w, conv_b, conv_b, w_down]
    aliases = {}
    if prev is not None:
        aliases = {len(args): 0}
        in_specs.append(pl.BlockSpec(memory_space=pl.ANY))
        args.append(prev)
    return pl.pallas_call(
        functools.partial(_ffn_kernel, tiles_per_seg, tm, halo),
        grid=(lay.B * tiles_per_seg, nf),
        in_specs=in_specs,
        out_specs=pl.BlockSpec((tm, D), lambda i, j: (row0 + i, 0)),
        out_shape=jax.ShapeDtypeStruct((M, D), _F32),
        scratch_shapes=[pltpu.VMEM((tm + 2 * halo, D), _BF16), pltpu.VMEM((tm, D), _F32)],
        input_output_aliases=aliases,
        compiler_params=_cparams("parallel", "arbitrary"),
        name="conv_gated_ffn_ctx" if ctx else "conv_gated_ffn_lat",
    )(*args)


def _shift_mix_kernel(lay, tm, halo, x_ref, xp_ref, xn_ref, g_ref, sh_ref, sc_ref, mu_ref, *o_refs):
    i = pl.program_id(0)
    g, sh, sc = g_ref[...], sh_ref[...], sc_ref[...]
    h = _normmod(x_ref[...], g, sh, sc)
    hp = _normmod(xp_ref[...], g, sh, sc)
    hn = _normmod(xn_ref[...], g, sh, sc)
    ext = jnp.concatenate([hp, h, hn], axis=0)
    n_ext = tm + 2 * halo
    has_prev, has_next = _seg_masks(i, tm, h.shape, lay)
    prev = jnp.where(has_prev, pltpu.roll(ext, 1, 0)[halo:halo + tm], 0.0)
    nxt = jnp.where(has_next, pltpu.roll(ext, n_ext - 1, 0)[halo:halo + tm], 0.0)
    dx = 0.5 * (prev + nxt) - h
    mu = mu_ref[...]
    for m, o_ref in enumerate(o_refs):
        o_ref[...] = (h + dx * mu[m:m + 1]).astype(o_ref.dtype)


def _shift_mix(x, g, shift, scale, mu, lay, tm=256):
    M, D = x.shape
    tm = _pick(lay.n, tm)
    tm = _pick(lay.B * lay.nc, tm)
    halo = 8
    per = tm // halo
    seg = _seg_index(lay, tm)
    vec = lambda: pl.BlockSpec((None, 1, D), lambda i: (seg(i), 0, 0))
    n_mix = mu.shape[0]
    return pl.pallas_call(
        functools.partial(_shift_mix_kernel, lay, tm, halo),
        grid=(M // tm,),
        in_specs=[
            pl.BlockSpec((tm, D), lambda i: (i, 0)),
            pl.BlockSpec((halo, D), lambda i: (jnp.maximum(i * per - 1, 0), 0)),
            pl.BlockSpec((halo, D), lambda i: (jnp.minimum((i + 1) * per, M // halo - 1), 0)),
            pl.BlockSpec((1, D), lambda i: (0, 0)),
            vec(), vec(),
            pl.BlockSpec((n_mix, D), lambda i: (0, 0)),
        ],
        out_specs=[pl.BlockSpec((tm, D), lambda i: (i, 0)) for _ in range(n_mix)],
        out_shape=[jax.ShapeDtypeStruct((M, D), _BF16) for _ in range(n_mix)],
        compiler_params=_cparams("parallel"),
        name="rwkv_shift_mix",
    )(x, x, x, g, shift, scale, mu)


def _vres_kernel(x_ref, w_ref, b_ref, v_ref, vf_ref, o_ref):
    gate = _sigmoid(_dot(x_ref[...], w_ref[...]) + b_ref[...])
    v = v_ref[...]
    o_ref[...] = v + (vf_ref[...] - v) * gate


def _vres(xl, w2, v0, v, v_first, tm=512):
    M, K = xl.shape
    D = w2.shape[1]
    tm = _pick(M, tm)
    return pl.pallas_call(
        _vres_kernel,
        grid=(M // tm,),
        in_specs=[
            pl.BlockSpec((tm, K), lambda i: (i, 0)),
            pl.BlockSpec((K, D), lambda i: (0, 0)),
            pl.BlockSpec((1, D), lambda i: (0, 0)),
            pl.BlockSpec((tm, D), lambda i: (i, 0)),
            pl.BlockSpec((tm, D), lambda i: (i, 0)),
        ],
        out_specs=pl.BlockSpec((tm, D), lambda i: (i, 0)),
        out_shape=jax.ShapeDtypeStruct((M, D), _F32),
        compiler_params=_cparams("parallel"),
        name="rwkv_value_residual",
    )(xl, w2, v0.reshape(1, D), v, v_first)


def _wkv_kernel_bd(reverse, final, C, LW, r_ref, k_ref, v_ref, wp_ref, ap_ref, kk_ref, ka_ref, rk_ref, *refs):
    if final:
        y0_ref, b0_ref, g_ref, lng_ref, lnb_ref, o_ref, s_ref = refs
    else:
        y_ref, bon_ref, s_ref = refs
    j = pl.program_id(2)

    @pl.when(j == 0)
    def _():
        s_ref[...] = jnp.zeros_like(s_ref)

    hpg = WKV_GROUP // RWKV_HEAD
    groups = range(LW // WKV_GROUP)
    sls = [slice(g * WKV_GROUP, (g + 1) * WKV_GROUP) for g in groups]
    row = lax.broadcasted_iota(jnp.int32, (C, C), 0)
    col = lax.broadcasted_iota(jnp.int32, (C, C), 1)
    tri = jnp.where((col >= row) if reverse else (col <= row), 1.0, 0.0).astype(_BF16)
    rc = lax.broadcasted_iota(jnp.int32, (C, WKV_GROUP), 0)
    cc = lax.rem(lax.broadcasted_iota(jnp.int32, (C, WKV_GROUP), 1), C)
    strict = (cc > rc) if reverse else (cc < rc)
    incl = (cc >= rc) if reverse else (cc <= rc)
    eye_cat = jnp.where(cc == rc, 1.0, 0.0).astype(_F32)
    br = lax.broadcasted_iota(jnp.int32, (WKV_GROUP, WKV_GROUP), 0) // RWKV_HEAD
    bc = lax.broadcasted_iota(jnp.int32, (WKV_GROUP, WKV_GROUP), 1) // RWKV_HEAD
    same_head = br == bc
    head_ones = jnp.where(same_head, 1.0, 0.0).astype(_BF16)

    def bd(x):
        return jnp.where(same_head, jnp.concatenate([x] * hpg, axis=0), jnp.zeros((), x.dtype))

    cast = lambda x: x.astype(_BF16)

    v = [v_ref[:, sl] for sl in sls]
    lw = []
    for sl in sls:
        wp = wp_ref[:, sl]
        sp = jnp.maximum(-wp, 0.0) + jnp.log(1.0 + jnp.exp(-jnp.abs(wp)))
        lw.append(-jnp.exp(-sp - 0.5))
    cl = [_ones_dot(tri, x, 3, left=True) for x in lw]
    total = [c[0:1] if reverse else c[C - 1:C] for c in cl]

    a = [_sigmoid(ap_ref[:, sl]) for sl in sls]
    kk = [k_ref[:, sl] * kk_ref[:, sl] for sl in sls]
    nrm = [jnp.sqrt(_ones_dot(head_ones, x * x, 2)) for x in kk]
    kk = [x / jnp.maximum(n, 1e-12) for x, n in zip(kk, nrm)]
    kd = [k_ref[:, sl] * (1.0 + (a[g] - 1.0) * ka_ref[:, sl]) for g, sl in enumerate(sls)]
    bonus = [_ones_dot(head_ones, r_ref[:, sl] * kd[g] * rk_ref[:, sl], 2) * v[g] for g, sl in enumerate(sls)]
    b = [kk[g] * a[g] for g in groups]

    At = [cast(-kk[g] * jnp.exp(cl[g] - lw[g])) for g in groups]
    Rt = [cast(r_ref[:, sl] * jnp.exp(cl[g])) for g, sl in enumerate(sls)]
    BKbd, BKp = [], []
    for g in groups:
        e_neg = jnp.exp(-cl[g])
        e_rem = jnp.exp(total[g] - cl[g])
        BKbd.append(jnp.concatenate([bd(cast(b[g] * e_neg)), bd(cast(kd[g] * e_neg))], axis=0))
        BKp.append(cast(jnp.concatenate([b[g] * e_rem, kd[g] * e_rem], axis=0)))
    vb = [cast(x) for x in v]
    Vbd = [bd(x) for x in vb]
    S0 = [s_ref[g] for g in groups]
    S0b = [cast(s) for s in S0]

    M = [_dot_nt(jnp.concatenate([At[g], Rt[g]], axis=0), BKbd[g]) for g in groups]
    AS = [_dot_nt(At[g], S0b[g]) for g in groups]
    RS = [_dot_nt(Rt[g], S0b[g]) for g in groups]
    Lab = [jnp.where(strict, m[:C, :WKV_GROUP], 0.0) for m in M]
    Lak = [cast(jnp.where(strict, m[:C, WKV_GROUP:], 0.0)) for m in M]
    Mr = [cast(jnp.concatenate([jnp.where(incl, m[C:, :WKV_GROUP], 0.0), jnp.where(incl, m[C:, WKV_GROUP:], 0.0)],
                               axis=1)) for m in M]

    Pw = [cast(x) for x in Lab]
    T = [eye_cat + x for x in Lab]
    Pbd = [bd(x) for x in Pw]
    for _ in range(int(math.log2(C)) - 1):
        Pw = [cast(_dot(x, xb)) for x, xb in zip(Pw, Pbd)]
        Pbd = [bd(x) for x in Pw]
        T = [t + _dot(cast(t), xb) for t, xb in zip(T, Pbd)]

    X = [AS[g] + _dot(Lak[g], Vbd[g]) for g in groups]
    Z = [cast(_dot(cast(T[g]), bd(cast(X[g])))) for g in groups]
    y = [RS[g] + _dot(Mr[g], jnp.concatenate([bd(Z[g]), Vbd[g]], axis=0)) for g in groups]
    dS = [_dot_tn(jnp.concatenate([Z[g], vb[g]], axis=0), BKp[g]) for g in groups]
    s_ref[...] = jnp.stack([S0[g] * jnp.exp(total[g]) + jnp.where(same_head, dS[g], 0.0) for g in groups], axis=0)

    if final:
        inv = 1.0 / RWKV_HEAD
        wkv = [y[g] + y0_ref[:, sl] for g, sl in enumerate(sls)]
        cen = [x - _ones_dot(head_ones, x, 3) * inv for x in wkv]
        var = [_ones_dot(head_ones, x * x, 2) * inv for x in cen]
        outs = []
        for g, sl in enumerate(sls):
            normed = cen[g] * lax.rsqrt(var[g] + GN_EPS) * lng_ref[:, sl] + lnb_ref[:, sl]
            outs.append(((normed + bonus[g] + b0_ref[:, sl]) * g_ref[:, sl]).astype(o_ref.dtype))
        o_ref[...] = jnp.concatenate(outs, axis=1)
    else:
        y_ref[...] = jnp.concatenate(y, axis=1)
        bon_ref[...] = jnp.concatenate(bonus, axis=1)


def _wkv_kernel(reverse, final, C, LW, prec, r_ref, k_ref, v_ref, wp_ref, ap_ref, kk_ref, ka_ref, rk_ref, *refs):
    if final:
        y0_ref, b0_ref, g_ref, lng_ref, lnb_ref, o_ref, s_ref = refs
    else:
        y_ref, bon_ref, s_ref = refs
    j = pl.program_id(2)

    @pl.when(j == 0)
    def _():
        s_ref[...] = jnp.zeros_like(s_ref)

    row = lax.broadcasted_iota(jnp.int32, (C, C), 0)
    col = lax.broadcasted_iota(jnp.int32, (C, C), 1)
    tri = jnp.where((col >= row) if reverse else (col <= row), 1.0, 0.0).astype(_BF16)
    row2 = lax.broadcasted_iota(jnp.int32, (C, 2 * C), 0)
    col2 = lax.broadcasted_iota(jnp.int32, (C, 2 * C), 1)
    col2 = jnp.where(col2 >= C, col2 - C, col2)
    strict2 = (col2 > row2) if reverse else (col2 < row2)
    incl2 = (col2 >= row2) if reverse else (col2 <= row2)
    pr = lax.broadcasted_iota(jnp.int32, (LANES, LANES), 0) // RWKV_HEAD
    pc = lax.broadcasted_iota(jnp.int32, (LANES, LANES), 1) // RWKV_HEAD
    same_head = pr == pc
    head_ones = jnp.where(same_head, 1.0, 0.0).astype(_BF16)
    first_head = lax.broadcasted_iota(jnp.int32, (C, LANES), 1) < RWKV_HEAD
    eye = jnp.where(row == col, 1.0, 0.0).astype(_F32)
    n_pairs = LW // LANES
    per_pair = LANES // RWKV_HEAD
    pairs = range(n_pairs)
    heads = [(p, hh) for p in pairs for hh in range(per_pair)]
    sls = [slice(p * LANES, (p + 1) * LANES) for p in pairs]

    def cast(x):
        return x.astype(_BF16) if prec is None else x

    def mm(a, b):
        return _dot(cast(a), cast(b), prec)

    def mm_nt(a, b):
        return _dot_nt(cast(a), cast(b), prec)

    def mm_tn(a, b):
        return _dot_tn(cast(a), cast(b), prec)

    v = [v_ref[:, sl] for sl in sls]
    lw = []
    for sl in sls:
        wp = wp_ref[:, sl]
        sp = jnp.maximum(-wp, 0.0) + jnp.log(1.0 + jnp.exp(-jnp.abs(wp)))
        lw.append(-jnp.exp(-sp - 0.5))
    cl = [_ones_dot(tri, x, 3, left=True) for x in lw]
    total = [c[0:1] if reverse else c[C - 1:C] for c in cl]

    a = [_sigmoid(ap_ref[:, sl]) for sl in sls]
    kk = [k_ref[:, sl] * kk_ref[:, sl] for sl in sls]
    nrm = [jnp.sqrt(_ones_dot(head_ones, x * x, 2)) for x in kk]
    kk = [x / jnp.maximum(n, 1e-12) for x, n in zip(kk, nrm)]
    kd = [k_ref[:, sl] * (1.0 + (a[p] - 1.0) * ka_ref[:, sl]) for p, sl in enumerate(sls)]
    bonus = [_ones_dot(head_ones, r_ref[:, sl] * kd[p] * rk_ref[:, sl], 2) * v[p] for p, sl in enumerate(sls)]
    b = [kk[p] * a[p] for p in pairs]

    At = [cast(-kk[p] * jnp.exp(cl[p] - lw[p])) for p in pairs]
    Rt = [cast(r_ref[:, sl] * jnp.exp(cl[p])) for p, sl in enumerate(sls)]
    BK, BKp = [], []
    for p in pairs:
        e_neg = jnp.exp(-cl[p])
        e_rem = jnp.exp(total[p] - cl[p])
        BK.append(cast(jnp.concatenate([b[p] * e_neg, kd[p] * e_neg], axis=0)))
        BKp.append(cast(jnp.concatenate([b[p] * e_rem, kd[p] * e_rem], axis=0)))
    vb = [cast(x) for x in v]
    ZV0 = [jnp.concatenate([jnp.zeros_like(x), x], axis=0) for x in vb]
    S0 = [s_ref[p] for p in pairs]
    S0b = [cast(s) for s in S0]

    zero = jnp.zeros_like(At[0])
    M = []
    for p, hh in heads:
        mh = first_head if hh == 0 else jnp.logical_not(first_head)
        AR = jnp.concatenate([jnp.where(mh, At[p], zero), jnp.where(mh, Rt[p], zero)], axis=0)
        M.append(_dot_nt(AR, BK[p], prec))
    AS = [_dot_nt(At[p], S0b[p], prec) for p in pairs]
    RS = [_dot_nt(Rt[p], S0b[p], prec) for p in pairs]
    M_top = [jnp.where(strict2, m[:C], 0.0) for m in M]
    M_bot = [cast(jnp.where(incl2, m[C:], 0.0)) for m in M]

    Pw = [m[:, :C] for m in M_top]
    T = [eye + x for x in Pw]
    for _ in range(int(math.log2(C)) - 1):
        Pwb = [cast(x) for x in Pw]
        Pw = [_dot(x, x, prec) for x in Pwb]
        T = [t + mm(t, x) for t, x in zip(T, Pw)]

    X = [AS[p] + mm(M_top[i], ZV0[p]) for i, (p, hh) in enumerate(heads)]
    Zh = [mm(t, x) for t, x in zip(T, X)]
    Yh = [_dot(M_bot[i], jnp.concatenate([cast(Zh[i]), vb[p]], axis=0), prec) for i, (p, hh) in enumerate(heads)]
    Z = [jnp.where(first_head, Zh[per_pair * p], Zh[per_pair * p + 1]) for p in pairs]
    y = [RS[p] + jnp.where(first_head, Yh[per_pair * p], Yh[per_pair * p + 1]) for p in pairs]
    dS = [_dot_tn(jnp.concatenate([cast(Z[p]), vb[p]], axis=0), BKp[p], prec) for p in pairs]
    s_ref[...] = jnp.stack([S0[p] * jnp.exp(total[p]) + jnp.where(same_head, dS[p], 0.0) for p in pairs], axis=0)

    if final:
        inv = 1.0 / RWKV_HEAD
        wkv = [y[p] + y0_ref[:, sl] for p, sl in enumerate(sls)]
        cen = [x - _ones_dot(head_ones, x, 3) * inv for x in wkv]
        var = [_ones_dot(head_ones, x * x, 2) * inv for x in cen]
        outs = []
        for p, sl in enumerate(sls):
            normed = cen[p] * lax.rsqrt(var[p] + GN_EPS) * lng_ref[:, sl] + lnb_ref[:, sl]
            outs.append(((normed + bonus[p] + b0_ref[:, sl]) * g_ref[:, sl]).astype(o_ref.dtype))
        o_ref[...] = jnp.concatenate(outs, axis=1)
    else:
        y_ref[...] = jnp.concatenate(y, axis=1)
        bon_ref[...] = jnp.concatenate(bonus, axis=1)


def _wkv_kernel_depth_first(reverse, final, C, LW, prec, r_ref, k_ref, v_ref, wp_ref, ap_ref, kk_ref, ka_ref, rk_ref,
                            *refs):
    if final:
        y0_ref, b0_ref, g_ref, lng_ref, lnb_ref, o_ref, s_ref = refs
    else:
        y_ref, bon_ref, s_ref = refs
    j = pl.program_id(2)

    @pl.when(j == 0)
    def _():
        s_ref[...] = jnp.zeros_like(s_ref)

    row = lax.broadcasted_iota(jnp.int32, (C, C), 0)
    col = lax.broadcasted_iota(jnp.int32, (C, C), 1)
    tri = jnp.where((col >= row) if reverse else (col <= row), 1.0, 0.0).astype(_BF16)
    row2 = lax.broadcasted_iota(jnp.int32, (C, 2 * C), 0)
    col2 = lax.broadcasted_iota(jnp.int32, (C, 2 * C), 1)
    col2 = jnp.where(col2 >= C, col2 - C, col2)
    strict2 = (col2 > row2) if reverse else (col2 < row2)
    incl2 = (col2 >= row2) if reverse else (col2 <= row2)
    pr = lax.broadcasted_iota(jnp.int32, (LANES, LANES), 0) // RWKV_HEAD
    pc = lax.broadcasted_iota(jnp.int32, (LANES, LANES), 1) // RWKV_HEAD
    same_head = pr == pc
    head_ones = jnp.where(same_head, 1.0, 0.0).astype(_BF16)
    first_head = lax.broadcasted_iota(jnp.int32, (C, LANES), 1) < RWKV_HEAD
    eye = jnp.where(row == col, 1.0, 0.0).astype(_F32)

    def mm(a, b):
        return _dot(a.astype(_BF16), b.astype(_BF16)) if prec is None else _dot(a, b, prec)

    def mm_nt(a, b):
        return _dot_nt(a.astype(_BF16), b.astype(_BF16)) if prec is None else _dot_nt(a, b, prec)

    def mm_tn(a, b):
        return _dot_tn(a.astype(_BF16), b.astype(_BF16)) if prec is None else _dot_tn(a, b, prec)

    new_states, outs, bons = [], [], []
    for p in range(LW // LANES):
        sl = slice(p * LANES, (p + 1) * LANES)
        r, k, v = r_ref[:, sl], k_ref[:, sl], v_ref[:, sl]
        wp, ap = wp_ref[:, sl], ap_ref[:, sl]

        sp = jnp.maximum(-wp, 0.0) + jnp.log(1.0 + jnp.exp(-jnp.abs(wp)))
        lw = -jnp.exp(-sp - 0.5)
        cl = _ones_dot(tri, lw, 3, left=True)
        total = cl[0:1] if reverse else cl[C - 1:C]

        a = _sigmoid(ap)
        kk = k * kk_ref[:, sl]
        nrm = jnp.sqrt(_ones_dot(head_ones, kk * kk, 2))
        kk = kk / jnp.maximum(nrm, 1e-12)
        kd = k * (1.0 + (a - 1.0) * ka_ref[:, sl])
        bonus = _ones_dot(head_ones, r * kd * rk_ref[:, sl], 2) * v
        b = kk * a

        e_neg = jnp.exp(-cl)
        e_rem = jnp.exp(total - cl)
        At = -kk * jnp.exp(cl - lw)
        Rt = r * jnp.exp(cl)
        S0 = s_ref[p]
        AS = mm_nt(At, S0)
        RS = mm_nt(Rt, S0)
        BK = jnp.concatenate([b * e_neg, kd * e_neg], axis=0)
        ZV0 = jnp.concatenate([jnp.zeros_like(v), v], axis=0)
        Zs, Ys = [], []
        for hh in range(LANES // RWKV_HEAD):
            mh = first_head if hh == 0 else jnp.logical_not(first_head)
            AR = jnp.concatenate([jnp.where(mh, At, 0.0), jnp.where(mh, Rt, 0.0)], axis=0)
            M = mm_nt(AR, BK)
            M_top = jnp.where(strict2, M[:C], 0.0)
            M_bot = jnp.where(incl2, M[C:], 0.0)
            Lab = M_top[:, :C]
            T = eye + Lab
            Pw = Lab
            for _ in range(int(math.log2(C)) - 1):
                Pw = mm(Pw, Pw)
                T = T + mm(T, Pw)
            X = AS + mm(M_top, ZV0)
            Zh = mm(T, X)
            Yh = mm(M_bot, jnp.concatenate([Zh, v], axis=0))
            Zs.append(Zh)
            Ys.append(Yh)
        Z = jnp.where(first_head, Zs[0], Zs[1])
        y = RS + jnp.where(first_head, Ys[0], Ys[1])
        dS = mm_tn(jnp.concatenate([Z, v], axis=0), jnp.concatenate([b * e_rem, kd * e_rem], axis=0))
        new_states.append(S0 * jnp.exp(total) + jnp.where(same_head, dS, 0.0))

        if final:
            wkv = y + y0_ref[:, sl]
            inv = 1.0 / RWKV_HEAD
            cen = wkv - _ones_dot(head_ones, wkv, 3) * inv
            var = _ones_dot(head_ones, cen * cen, 2) * inv
            normed = cen * lax.rsqrt(var + GN_EPS) * lng_ref[:, sl] + lnb_ref[:, sl]
            outs.append(((normed + bonus + b0_ref[:, sl]) * g_ref[:, sl]).astype(o_ref.dtype))
        else:
            outs.append(y)
            bons.append(bonus)

    s_ref[...] = jnp.stack(new_states, axis=0)
    if final:
        o_ref[...] = jnp.concatenate(outs, axis=1)
    else:
        y_ref[...] = jnp.concatenate(outs, axis=1)
        bon_ref[...] = jnp.concatenate(bons, axis=1)


def _wkv(r, k, v, wpre, apre, k_k, k_a, r_k, lay, reverse, final_args=None, lw=2048):
    M, D = r.shape
    C = WKV_CHUNK
    assert C == RWKV_HEAD and lay.nc % C == 0 and lay.n % C == 0 and D % WKV_GROUP == 0
    lw = _pick(D, lw)
    assert lw % WKV_GROUP == 0
    ncc, ncl = lay.nc // C, lay.n // C
    ctx_blk0 = lay.lat_rows // C

    def row_blk(b, j):
        if reverse:
            return jnp.where(j < ncc, ctx_blk0 + b * ncc + (ncc - 1 - j), b * ncl + (ncl - 1 - (j - ncc)))
        return jnp.where(j < ncc, ctx_blk0 + b * ncc + j, b * ncl + (j - ncc))

    tile = lambda: pl.BlockSpec((C, lw), lambda b, h, j: (row_blk(b, j), h))
    vec = lambda: pl.BlockSpec((1, lw), lambda b, h, j: (0, h))
    in_specs = [tile() for _ in range(5)] + [vec() for _ in range(3)]
    args = [r, k, v, wpre, apre, k_k.reshape(1, D), k_a.reshape(1, D), r_k.reshape(1, D)]
    final = final_args is not None
    if final:
        y0, b0, g, lng, lnb = final_args
        in_specs += [tile(), tile(), tile(), vec(), vec()]
        args += [y0, b0, g, lng.reshape(1, D), lnb.reshape(1, D)]
        out_specs = tile()
        out_shape = jax.ShapeDtypeStruct((M, D), _BF16)
    else:
        out_specs = [tile(), tile()]
        out_shape = [jax.ShapeDtypeStruct((M, D), _F32), jax.ShapeDtypeStruct((M, D), _F32)]
    return pl.pallas_call(
        functools.partial(_wkv_kernel_bd, reverse, final, C, lw),
        grid=(lay.B, D // lw, ncc + ncl),
        in_specs=in_specs,
        out_specs=out_specs,
        out_shape=out_shape,
        scratch_shapes=[pltpu.VMEM((lw // WKV_GROUP, WKV_GROUP, WKV_GROUP), _F32)],
        compiler_params=_cparams("parallel", "parallel", "arbitrary"),
        name="wkv7_scan_rev" if reverse else "wkv7_scan_fwd",
    )(*args)


def _final_norm_kernel(x_ref, g_ref, o_ref):
    x = x_ref[...]
    o_ref[...] = x * lax.rsqrt(jnp.mean(x * x, axis=-1, keepdims=True) + NORM_EPS) * g_ref[...]


def _final_norm(x, g, n_rows, tm=512):
    D = x.shape[1]
    tm = _pick(n_rows, tm)
    return pl.pallas_call(
        _final_norm_kernel,
        grid=(n_rows // tm,),
        in_specs=[pl.BlockSpec((tm, D), lambda i: (i, 0)), pl.BlockSpec((1, D), lambda i: (0, 0))],
        out_specs=pl.BlockSpec((tm, D), lambda i: (i, 0)),
        out_shape=jax.ShapeDtypeStruct((n_rows, D), _F32),
        compiler_params=_cparams("parallel"),
        name="final_rms_norm",
    )(x, g.reshape(1, D))


def _rope_tables(lay):
    n = lay.n
    t = jnp.arange(n, dtype=jnp.int32)
    row = (t // GRID_W).astype(_F32)
    col = (t % GRID_W).astype(_F32)
    axis_dim = HEAD_DIM // 2
    inv_freq = ROPE_THETA ** (-jnp.arange(0, axis_dim, 2, dtype=_F32) / axis_dim)
    ang = jnp.concatenate([row[:, None] * inv_freq, col[:, None] * inv_freq], axis=-1)
    cos, sin = jnp.cos(ang), jnp.sin(ang)
    cos_l = jnp.concatenate([cos, cos], axis=-1)
    sin_l = jnp.concatenate([-sin, sin], axis=-1)
    n_ctx_rows = lay.B * lay.nc
    cos_f = jnp.concatenate([jnp.tile(cos_l, (lay.B, 1)), jnp.ones((n_ctx_rows, HEAD_DIM), _F32)], axis=0)
    sin_f = jnp.concatenate([jnp.tile(sin_l, (lay.B, 1)), jnp.zeros((n_ctx_rows, HEAD_DIM), _F32)], axis=0)
    return cos_f, sin_f


def _deinterleave_heads(w, n_heads):
    lead = w.shape[:-1]
    w = w.reshape(*lead, n_heads, HEAD_DIM // 2, 2)
    w = jnp.swapaxes(w, -1, -2)
    return w.reshape(*lead, n_heads * HEAD_DIM)


def _pad_cols(w, mult=LANES):
    pad = (-w.shape[-1]) % mult
    return jnp.pad(w, [(0, 0)] * (w.ndim - 1) + [(0, pad)]) if pad else w


def _pad_rows(w, mult=LANES):
    pad = (-w.shape[-2]) % mult
    return jnp.pad(w, [(0, 0)] * (w.ndim - 2) + [(0, pad), (0, 0)]) if pad else w


def kernel(x, c, ctx, c_ctx, w_mod, b_mod, norm1_g, norm2_g, attn_w_in, attn_w_out, q_norm_g, k_norm_g, rwkv_mu, rwkv_w_r, rwkv_w_k, rwkv_w_v, rwkv_w_o, rwkv_decay_w0, rwkv_decay_w1, rwkv_decay_w2, rwkv_iclr_a0, rwkv_iclr_a1, rwkv_iclr_a2, rwkv_gate_g1, rwkv_gate_g2, rwkv_k_k, rwkv_k_a, rwkv_r_k, rwkv_lnx_g, rwkv_lnx_b, rwkv_vres_v0, rwkv_vres_v1, rwkv_vres_v2, ffn_w_up, ffn_conv_w, ffn_conv_b, ffn_w_down, final_norm_g):
    B, n, D = x.shape
    nc = ctx.shape[1]
    depth = w_mod.shape[0]
    lay = Layout(B, n, nc, B * n, B * (n + nc))
    attn_w = N_ATTN_HEADS * HEAD_DIM
    kv_w = N_KV_HEADS * HEAD_DIM
    fw = D - attn_w

    xs = jnp.concatenate([x.reshape(B * n, D), ctx.reshape(B * nc, D)], axis=0)

    cvec = jnp.concatenate([c, c_ctx[None, :], jnp.zeros((7 - B % 8, D), _F32)], axis=0)
    mods = _modulation(cvec, w_mod, b_mod)[:, :B + 1]
    mods = mods.reshape(depth, B + 1, 6, 1, D)
    cos_f, sin_f = _rope_tables(lay)
    qg_scale = HEAD_DIM ** -0.5

    v_first = None
    for i in range(depth):
        last = i == depth - 1
        jl = i // 2
        m = [mods[i, :, t] for t in range(6)]
        g1 = norm1_g[i].reshape(1, D)
        if i % 2 == 0:
            w_in = attn_w_in[jl]
            w_in = jnp.concatenate([
                _deinterleave_heads(w_in[:, :attn_w], N_ATTN_HEADS),
                _deinterleave_heads(w_in[:, attn_w:attn_w + kv_w], N_KV_HEADS),
                w_in[:, attn_w + kv_w:]], axis=1).astype(_BF16)
            qg = (_deinterleave_heads(q_norm_g[jl], 1) * qg_scale).reshape(1, HEAD_DIM)
            kg = _deinterleave_heads(k_norm_g[jl], 1).reshape(1, HEAD_DIM)
            u = _attn_in_proj(xs, g1, m[0], m[1], w_in, qg, kg, cos_f, sin_f, lay)
            o = _attention(u, lay)
            fm = _fourier_mix(u, lay, fw)
            fm = _fourier_mix(u, lay, fw, prev=fm, ctx=True)
            w_out = attn_w_out[jl].astype(_BF16)
            xs = _out_proj([o, fm], [w_out[:attn_w], w_out[attn_w:]], xs, m[2], lay)
        else:
            bf = lambda w: w.astype(_BF16)
            x_r, x_w, x_k, x_v, x_a, x_g = _shift_mix(xs, g1, m[0], m[1], rwkv_mu[jl], lay)
            r = _linear(x_r, bf(rwkv_w_r[jl]), name="rwkv_r")
            k = _linear(x_k, bf(rwkv_w_k[jl]), name="rwkv_k")
            v = _linear(x_v, bf(rwkv_w_v[jl]), name="rwkv_v")
            if jl == 0:
                v_first = v
            else:
                vl = _linear(x_v, bf(_pad_cols(rwkv_vres_v1[jl - 1])), out_dtype=_BF16, name="rwkv_vres_lora")
                v = _vres(vl, bf(_pad_rows(rwkv_vres_v2[jl - 1])), rwkv_vres_v0[jl - 1], v, v_first)
            gl = _linear(x_g, bf(_pad_cols(rwkv_gate_g1[jl])), act="sigmoid", out_dtype=_BF16, name="rwkv_gate_lora")
            g = _linear(gl, bf(_pad_rows(rwkv_gate_g2[jl])), name="rwkv_gate")
            pre = []
            for d in range(2):
                dl = _linear(x_w, bf(_pad_cols(rwkv_decay_w1[jl, d])), act="tanh", out_dtype=_BF16,
                             name="rwkv_decay_lora")
                wpre = _linear(dl, bf(_pad_rows(rwkv_decay_w2[jl, d])), bias=rwkv_decay_w0[jl, d], name="rwkv_decay")
                al = _linear(x_a, bf(_pad_cols(rwkv_iclr_a1[jl, d])), out_dtype=_BF16, name="rwkv_iclr_lora")
                apre = _linear(al, bf(_pad_rows(rwkv_iclr_a2[jl, d])), bias=rwkv_iclr_a0[jl, d], name="rwkv_iclr")
                pre.append((wpre, apre))
            kk_, ka_, rk_ = rwkv_k_k[jl], rwkv_k_a[jl], rwkv_r_k[jl].reshape(D)
            y0, bon0 = _wkv(r, k, v, pre[0][0], pre[0][1], kk_, ka_, rk_, lay, reverse=False)
            z = _wkv(r, k, v, pre[1][0], pre[1][1], kk_, ka_, rk_, lay, reverse=True,
                     final_args=(y0, bon0, g, rwkv_lnx_g[jl], rwkv_lnx_b[jl]))
            xs = _out_proj([z], [bf(rwkv_w_o[jl])], xs, m[2], lay)
        ffn_args = (xs, norm2_g[i].reshape(1, D), m[3], m[4], m[5], ffn_w_up[i].astype(_BF16), ffn_conv_w[i],
                    ffn_conv_b[i].reshape(1, -1), ffn_w_down[i].astype(_BF16), lay)
        xs_new = _ffn(*ffn_args)
        if not last:
            xs_new = _ffn(*ffn_args, ctx=True, prev=xs_new)
        xs = xs_new
    out = _final_norm(xs, final_norm_g, lay.lat_rows)
    return out.reshape(B, n, D)
```

```python
import collections
import functools
import math

import jax
import jax.numpy as jnp
from jax import lax
from jax.experimental import pallas as pl
from jax.experimental.pallas import tpu as pltpu

HEAD_DIM = 128
N_ATTN_HEADS = 12
N_KV_HEADS = 4
GRID_W = 64
ROPE_THETA = 10000.0
N_FOURIER_GROUPS = 4
RWKV_HEAD = 64
CONV_W = 3
NORM_EPS = 1e-6
GN_EPS = 64e-5

LANES = 128
MXU_TILE = 256
BF16_ROWS = 16
VMEM_LIMIT = 56 * 1024 * 1024
WKV_CHUNK = 64
WKV_GROUP = MXU_TILE

_F32 = jnp.float32
_BF16 = jnp.bfloat16
_HI = lax.Precision.HIGHEST

Layout = collections.namedtuple("Layout", "B n nc lat_rows rows")


def _cparams(*sem):
    return pltpu.CompilerParams(dimension_semantics=sem, vmem_limit_bytes=VMEM_LIMIT)


def _pick(total, pref):
    t = min(pref, total)
    while total % t:
        t -= 1
    return t


def _row_tile(lay, pref):
    return _pick(lay.B * lay.nc, _pick(lay.n, pref))


def _seg_index(lay, tm):
    def f(i):
        r0 = i * tm
        return jnp.where(r0 < lay.lat_rows, r0 // lay.n, lay.B)
    return f


def _seg_masks(i, tm, shape, lay):
    t = lax.broadcasted_iota(jnp.int32, shape, 0)
    r0 = i * tm
    is_lat = r0 < lay.lat_rows
    if tm >= lay.nc:
        pc = lax.rem(t, lay.nc)
    else:
        pc = t + lax.rem(r0 - lay.lat_rows, lay.nc)
    pos = jnp.where(is_lat, t + lax.rem(r0, lay.n), pc)
    last = jnp.where(is_lat, lay.n - 1, lay.nc - 1)
    return pos != 0, pos != last


def _normmod(x, g, shift, scale):
    ms = jnp.mean(x * x, axis=-1, keepdims=True)
    return (x * lax.rsqrt(ms + NORM_EPS)) * g * (1.0 + scale) + shift


def _sigmoid(x):
    return 1.0 / (1.0 + jnp.exp(-x))


def _dot(a, b, prec=None):
    return jnp.dot(a, b, preferred_element_type=_F32, precision=prec)


def _dot_nt(a, b):
    return lax.dot_general(a, b, (((1,), (1,)), ((), ())), preferred_element_type=_F32)


def _dot_tn(a, b):
    return lax.dot_general(a, b, (((0,), (0,)), ((), ())), preferred_element_type=_F32)


def _bf16_pieces(x, parts):
    pieces = []
    rem = x
    for i in range(parts):
        piece = rem.astype(_BF16)
        pieces.append(piece)
        if i + 1 < parts:
            rem = rem - piece.astype(_F32)
    return pieces


def _ones_dot_left(ones, x, parts):
    acc = None
    for piece in _bf16_pieces(x, parts):
        d = _dot(ones, piece)
        acc = d if acc is None else acc + d
    return acc


def _ones_dot_rows(ones, xs, parts):
    rows = xs[0].shape[0]
    stacked = jnp.concatenate([p for x in xs for p in _bf16_pieces(x, parts)], axis=0)
    res = _dot(stacked, ones)
    outs = []
    for n in range(len(xs)):
        base = n * parts * rows
        acc = res[base:base + rows]
        for i in range(1, parts):
            acc = acc + res[base + i * rows:base + (i + 1) * rows]
        outs.append(acc)
    return outs


def _mod_kernel(c_ref, w_ref, b_ref, o_ref):
    c = c_ref[...]
    s = c * _sigmoid(c)
    o_ref[...] = _dot(s, w_ref[...], _HI) + b_ref[...]


def _modulation(cvec, w_mod, b_mod):
    depth, D, N = w_mod.shape
    rows = cvec.shape[0]
    tn = _pick(N, 512)
    return pl.pallas_call(
        _mod_kernel,
        grid=(depth, N // tn),
        in_specs=[
            pl.BlockSpec((rows, D), lambda l, j: (0, 0)),
            pl.BlockSpec((None, D, tn), lambda l, j: (l, 0, j)),
            pl.BlockSpec((None, 1, tn), lambda l, j: (l, 0, j)),
        ],
        out_specs=pl.BlockSpec((None, rows, tn), lambda l, j: (l, 0, j)),
        out_shape=jax.ShapeDtypeStruct((depth, rows, N), _F32),
        compiler_params=_cparams("parallel", "parallel"),
        name="adaln_modulation",
    )(cvec, w_mod, b_mod.reshape(depth, 1, N))


def _linear_kernel(x_ref, w_ref, o_ref):
    o_ref[...] = _dot(x_ref[...], w_ref[...]).astype(o_ref.dtype)


def _linear(x, w, out_dtype=_F32, tm=512, tn=2048, name="linear"):
    M, K = x.shape
    N = w.shape[1]
    tm, tn = _pick(M, tm), _pick(N, tn)
    return pl.pallas_call(
        _linear_kernel,
        grid=(M // tm, N // tn),
        in_specs=[pl.BlockSpec((tm, K), lambda i, j: (i, 0)), pl.BlockSpec((K, tn), lambda i, j: (0, j))],
        out_specs=pl.BlockSpec((tm, tn), lambda i, j: (i, j)),
        out_shape=jax.ShapeDtypeStruct((M, N), out_dtype),
        compiler_params=_cparams("parallel", "arbitrary"),
        name=name,
    )(x, w)


def _lora_kernel(act, ranks, has_bias, x_ref, w1_ref, *refs):
    n_out = len(ranks)
    w2_refs = refs[:n_out]
    b_refs = refs[n_out:2 * n_out] if has_bias else ()
    o_refs = refs[-n_out:]
    t = _dot(x_ref[...], w1_ref[...])
    if act == "tanh":
        t = jnp.tanh(t)
    elif act == "sigmoid":
        t = _sigmoid(t)
    t = t.astype(_BF16)
    off = 0
    for d in range(n_out):
        y = _dot(t[:, off:off + ranks[d]], w2_refs[d][...])
        if has_bias:
            y = y + b_refs[d][...]
        o_refs[d][...] = y
        off += ranks[d]


def _lora(x, w1s, w2s, biases=None, act=None, tm=512, name="lora"):
    M, D = x.shape
    N = w2s[0].shape[1]
    tm = _pick(M, tm)
    w1 = jnp.concatenate([_pad_cols(w) for w in w1s], axis=1).astype(_BF16)
    w2p = [_pad_rows(w).astype(_BF16) for w in w2s]
    ranks = tuple(w.shape[0] for w in w2p)
    in_specs = [pl.BlockSpec((tm, D), lambda i: (i, 0)), pl.BlockSpec(w1.shape, lambda i: (0, 0))]
    in_specs += [pl.BlockSpec(w.shape, lambda i: (0, 0)) for w in w2p]
    args = [x, w1, *w2p]
    if biases is not None:
        in_specs += [pl.BlockSpec((1, N), lambda i: (0, 0)) for _ in biases]
        args += [b.reshape(1, N) for b in biases]
    return pl.pallas_call(
        functools.partial(_lora_kernel, act, ranks, biases is not None),
        grid=(M // tm,),
        in_specs=in_specs,
        out_specs=[pl.BlockSpec((tm, N), lambda i: (i, 0)) for _ in w2p],
        out_shape=[jax.ShapeDtypeStruct((M, N), _F32) for _ in w2p],
        compiler_params=_cparams("parallel"),
        name=name,
    )(*args)


def _attn_in_kernel(sw, n_q, n_qk, x_ref, g_ref, sh_ref, sc_ref, w_ref, qg_ref, kg_ref, cos_ref, sin_ref, o_ref):
    h = _normmod(x_ref[...], g_ref[...], sh_ref[...], sc_ref[...]).astype(_BF16)
    cos, sin = cos_ref[...], sin_ref[...]
    n_blocks = w_ref.shape[1] // sw

    def finish(s, acc):
        cols = slice(s * sw, (s + 1) * sw)
        if s >= n_qk:
            o_ref[:, cols] = acc.astype(o_ref.dtype)
            return
        gain = qg_ref[...] if s < n_q else kg_ref[...]
        for hd in range(sw // HEAD_DIM):
            a = acc[:, hd * HEAD_DIM:(hd + 1) * HEAD_DIM]
            y = a * lax.rsqrt(jnp.mean(a * a, axis=-1, keepdims=True) + NORM_EPS) * gain
            y = y * cos + pltpu.roll(y, HEAD_DIM // 2, 1) * sin
            o_ref[:, s * sw + hd * HEAD_DIM:s * sw + (hd + 1) * HEAD_DIM] = y.astype(o_ref.dtype)

    acc = _dot(h, w_ref[:, 0:sw])
    for s in range(n_blocks):
        nxt = _dot(h, w_ref[:, (s + 1) * sw:(s + 2) * sw]) if s + 1 < n_blocks else None
        finish(s, acc)
        acc = nxt


def _attn_in_proj(x, g, shift, scale, w, qg, kg, cos, sin, lay, tm=512):
    M, D = x.shape
    N = w.shape[1]
    tm = _row_tile(lay, tm)
    attn_w = N_ATTN_HEADS * HEAD_DIM
    kv_w = N_KV_HEADS * HEAD_DIM
    sw = math.gcd(math.gcd(attn_w, kv_w), MXU_TILE)
    seg = _seg_index(lay, tm)
    vec = lambda: pl.BlockSpec((None, 1, D), lambda i: (seg(i), 0, 0))
    return pl.pallas_call(
        functools.partial(_attn_in_kernel, sw, attn_w // sw, (attn_w + kv_w) // sw),
        grid=(M // tm,),
        in_specs=[
            pl.BlockSpec((tm, D), lambda i: (i, 0)),
            pl.BlockSpec((1, D), lambda i: (0, 0)),
            vec(), vec(),
            pl.BlockSpec((D, N), lambda i: (0, 0)),
            pl.BlockSpec((1, HEAD_DIM), lambda i: (0, 0)),
            pl.BlockSpec((1, HEAD_DIM), lambda i: (0, 0)),
            pl.BlockSpec((tm, HEAD_DIM), lambda i: (i, 0)),
            pl.BlockSpec((tm, HEAD_DIM), lambda i: (i, 0)),
        ],
        out_specs=pl.BlockSpec((tm, N), lambda i: (i, 0)),
        out_shape=jax.ShapeDtypeStruct((M, N), _BF16),
        compiler_params=_cparams("parallel"),
        name="attn_in_proj",
    )(x, g, shift, scale, w, qg, kg, cos, sin)


def _attn_kernel(n_lat_tiles, g_per_kv, kb, q_ref, kc_ref, vc_ref, kl_ref, vl_ref, o_ref):
    qi = pl.program_id(2)
    tq = q_ref.shape[0]
    q = jnp.concatenate([q_ref[:, g * HEAD_DIM:(g + 1) * HEAD_DIM] for g in range(g_per_kv)], axis=0)

    def finish(acc, l):
        o = acc / l
        for g in range(g_per_kv):
            o_ref[:, g * HEAD_DIM:(g + 1) * HEAD_DIM] = o[g * tq:(g + 1) * tq].astype(o_ref.dtype)

    s_c = _dot_nt(q, kc_ref[...])
    m = jnp.max(s_c, axis=-1, keepdims=True)
    p = jnp.exp(s_c - m)
    l = jnp.sum(p, axis=-1, keepdims=True)
    acc = _dot(p.astype(_BF16), vc_ref[...])

    @pl.when(qi < n_lat_tiles)
    def _():
        n_blocks = kl_ref.shape[0] // kb
        mj, lj, accj = m, l, acc
        s_next = _dot_nt(q, kl_ref[0:kb, :])
        for j in range(n_blocks):
            s = s_next
            if j + 1 < n_blocks:
                s_next = _dot_nt(q, kl_ref[(j + 1) * kb:(j + 2) * kb, :])
            m_new = jnp.maximum(mj, jnp.max(s, axis=-1, keepdims=True))
            alpha = jnp.exp(mj - m_new)
            pj = jnp.exp(s - m_new)
            lj = lj * alpha + jnp.sum(pj, axis=-1, keepdims=True)
            accj = accj * alpha + _dot(pj.astype(_BF16), vl_ref[j * kb:(j + 1) * kb, :])
            mj = m_new
        finish(accj, lj)

    @pl.when(qi >= n_lat_tiles)
    def _():
        finish(acc, l)


def _attention(u, lay, tq=256, kb=512):
    M = u.shape[0]
    g_per_kv = N_ATTN_HEADS // N_KV_HEADS
    qw = g_per_kv * HEAD_DIM
    attn_w = N_ATTN_HEADS * HEAD_DIM
    tq = _pick(lay.nc, tq)
    nlt, nct = lay.n // tq, lay.nc // tq
    k_blk = attn_w // HEAD_DIM
    v_blk = k_blk + N_KV_HEADS
    ctx_blk0 = lay.lat_rows // lay.nc

    def q_map(b, kv, qi):
        row = jnp.where(qi < nlt, b * nlt + qi, lay.lat_rows // tq + b * nct + (qi - nlt))
        return (row, kv)

    return pl.pallas_call(
        functools.partial(_attn_kernel, nlt, g_per_kv, _pick(lay.n, kb)),
        grid=(lay.B, N_KV_HEADS, nlt + nct),
        in_specs=[
            pl.BlockSpec((tq, qw), q_map),
            pl.BlockSpec((lay.nc, HEAD_DIM), lambda b, kv, qi: (ctx_blk0 + b, k_blk + kv)),
            pl.BlockSpec((lay.nc, HEAD_DIM), lambda b, kv, qi: (ctx_blk0 + b, v_blk + kv)),
            pl.BlockSpec((lay.n, HEAD_DIM), lambda b, kv, qi: (b, k_blk + kv)),
            pl.BlockSpec((lay.n, HEAD_DIM), lambda b, kv, qi: (b, v_blk + kv)),
        ],
        out_specs=pl.BlockSpec((tq, qw), q_map),
        out_shape=jax.ShapeDtypeStruct((M, attn_w), _BF16),
        compiler_params=_cparams("parallel", "parallel", "arbitrary"),
        name="gqa_attention",
    )(u, u, u, u, u)


def _dft_kernel(f_ref, cc_ref, sc_ref, cn_ref, sn_ref, o_ref, a_ref, b_ref):
    @pl.when(pl.program_id(1) == 0)
    def _():
        f = f_ref[...]
        a_ref[...] = _dot(f, cc_ref[...]).astype(_BF16)
        b_ref[...] = _dot(f, sc_ref[...]).astype(_BF16)

    o_ref[...] = (_dot(cn_ref[...], a_ref[...]) - _dot(sn_ref[...], b_ref[...])).astype(o_ref.dtype)


def _cos_sin_matrix(m, split=64):
    j = jnp.arange(m, dtype=jnp.int32)[:, None]
    w = 2.0 * math.pi / m

    def cs(k):
        ang = w * ((j * k[None, :]) % m).astype(_F32)
        return jnp.cos(ang), jnp.sin(ang)

    if m % split or m <= 8 * split:
        return cs(jnp.arange(m, dtype=jnp.int32))
    ca, sa = cs(split * jnp.arange(m // split, dtype=jnp.int32))
    cb, sb = cs(jnp.arange(split, dtype=jnp.int32))
    cos = ca[:, :, None] * cb[:, None, :] - sa[:, :, None] * sb[:, None, :]
    sin = sa[:, :, None] * cb[:, None, :] + ca[:, :, None] * sb[:, None, :]
    return cos.reshape(m, m), sin.reshape(m, m)


def _dft_tables(n, group, n_groups):
    cn, sn = _cos_sin_matrix(n)
    scale = 1.0 / math.sqrt(n * group)
    cg, sg = _cos_sin_matrix(group)
    eye = jnp.eye(n_groups, dtype=_F32)
    return ((cn * scale).astype(_BF16), (sn * scale).astype(_BF16),
            jnp.kron(eye, cg).astype(_BF16), jnp.kron(eye, sg).astype(_BF16))


def _fourier_mix(u, lay, fw, ctx=False, tm=512):
    M, N = u.shape
    n = lay.nc if ctx else lay.n
    tm = _pick(n, tm)
    nt = n // tm
    cn, sn, cc, sc = _dft_tables(n, fw // N_FOURIER_GROUPS, N_FOURIER_GROUPS)
    f_blk = N // fw - 1
    seg0 = lay.lat_rows // n if ctx else 0
    return pl.pallas_call(
        _dft_kernel,
        grid=(lay.B, nt),
        in_specs=[
            pl.BlockSpec((n, fw), lambda b, i: (seg0 + b, f_blk)),
            pl.BlockSpec((fw, fw), lambda b, i: (0, 0)),
            pl.BlockSpec((fw, fw), lambda b, i: (0, 0)),
            pl.BlockSpec((tm, n), lambda b, i: (i, 0)),
            pl.BlockSpec((tm, n), lambda b, i: (i, 0)),
        ],
        out_specs=pl.BlockSpec((tm, fw), lambda b, i: (b * nt + i, 0)),
        out_shape=jax.ShapeDtypeStruct((lay.B * n, fw), _BF16),
        scratch_shapes=[pltpu.VMEM((n, fw), _BF16), pltpu.VMEM((n, fw), _BF16)],
        compiler_params=_cparams("parallel", "arbitrary"),
        name="fourier_mix_ctx" if ctx else "fourier_mix_lat",
    )(u, cc, sc, cn, sn)


def _out_proj_kernel(n_lhs, *refs):
    lhs = refs[:n_lhs]
    ws = refs[n_lhs:2 * n_lhs]
    x_ref, gate_ref, o_ref = refs[2 * n_lhs], refs[2 * n_lhs + 1], refs[2 * n_lhs + 2]
    acc = _dot(lhs[0][...], ws[0][...])
    for a, w in zip(lhs[1:], ws[1:]):
        acc = acc + _dot(a[...], w[...])
    o_ref[...] = x_ref[...] + gate_ref[...] * acc


def _out_proj(lhs, ws, x, gate, lay, tm=512):
    M, D = x.shape
    tm = _row_tile(lay, tm)
    seg = _seg_index(lay, tm)
    in_specs = [pl.BlockSpec((tm, a.shape[1]), lambda i: (i, 0)) for a in lhs]
    in_specs += [pl.BlockSpec(w.shape, lambda i: (0, 0)) for w in ws]
    in_specs += [pl.BlockSpec((tm, D), lambda i: (i, 0)), pl.BlockSpec((None, 1, D), lambda i: (seg(i), 0, 0))]
    return pl.pallas_call(
        functools.partial(_out_proj_kernel, len(lhs)),
        grid=(M // tm,),
        in_specs=in_specs,
        out_specs=pl.BlockSpec((tm, D), lambda i: (i, 0)),
        out_shape=jax.ShapeDtypeStruct((M, D), _F32),
        compiler_params=_cparams("parallel"),
        name="out_proj_residual",
    )(*lhs, *ws, x, gate)


def _ffn_kernel(n_lat_tiles, n_ctx_tiles, tiles_per_seg, nc, tm, halo, n_sub, x_ref, xp_ref, xn_ref, g_ref, sh_ref,
                sc_ref, gate_ref, wg_ref, wv_ref, cwg_ref, cwv_ref, cbg_ref, cbv_ref, wd_ref, o_ref, h_ref, acc_ref):
    i, j = pl.program_id(0), pl.program_id(1)
    ext = tm + 2 * halo
    is_lat = i < n_lat_tiles

    @pl.when(j == 0)
    def _():
        g, sh, sc = g_ref[...], sh_ref[...], sc_ref[...]
        pos = lax.rem(i, tiles_per_seg)
        keep_prev = jnp.logical_and(is_lat, pos != 0)
        keep_next = jnp.logical_and(is_lat, pos != tiles_per_seg - 1)
        hp = _normmod(xp_ref[...], g, sh, sc)
        hn = _normmod(xn_ref[...], g, sh, sc)
        h_ref[0:halo, :] = jnp.where(keep_prev, hp, 0.0).astype(_BF16)
        h_ref[halo:halo + tm, :] = _normmod(x_ref[...], g, sh, sc).astype(_BF16)
        h_ref[halo + tm:ext, :] = jnp.where(keep_next, hn, 0.0).astype(_BF16)
        acc_ref[...] = jnp.zeros_like(acc_ref)

    def step(inner_boundaries):
        h = h_ref[...]
        tf = wg_ref.shape[1]
        ts = tf // n_sub
        subs = [slice(s * ts, (s + 1) * ts) for s in range(n_sub)]

        def conv(u, cw, cb):
            up = pltpu.roll(u, 1, 0)[halo:halo + tm]
            un = pltpu.roll(u, ext - 1, 0)[halo:halo + tm]
            if inner_boundaries:
                t = lax.rem(lax.broadcasted_iota(jnp.int32, up.shape, 0), nc)
                up = jnp.where(t != 0, up, 0.0)
                un = jnp.where(t != nc - 1, un, 0.0)
            return up * cw[0:1] + u[halo:halo + tm] * cw[1:2] + un * cw[2:3] + cb

        ups = [None] * n_sub
        ups[0] = (_dot(h, wg_ref[:, subs[0]]), _dot(h, wv_ref[:, subs[0]]))
        total = None
        for s, sl in enumerate(subs):
            if s + 1 < n_sub:
                ups[s + 1] = (_dot(h, wg_ref[:, subs[s + 1]]), _dot(h, wv_ref[:, subs[s + 1]]))
            ug, uv = ups[s]
            gt = conv(ug, cwg_ref[:, sl], cbg_ref[:, sl])
            vl = conv(uv, cwv_ref[:, sl], cbv_ref[:, sl])
            act = (gt * _sigmoid(gt) * vl).astype(_BF16)
            d = _dot(act, wd_ref[sl, :])
            total = d if total is None else total + d
        acc_ref[...] += total

    if n_ctx_tiles == 0:
        step(False)
    else:
        pl.when(is_lat)(functools.partial(step, False))
        pl.when(jnp.logical_not(is_lat))(functools.partial(step, tm > nc))

    @pl.when(j == pl.num_programs(1) - 1)
    def _():
        o_ref[...] = x_ref[...] + gate_ref[...] * acc_ref[...]


def _ffn(x, g, shift, scale, gate, w_up, conv_w, conv_b, w_down, lay, n_rows, tm=512, tf=512, sub=MXU_TILE):
    M, D = x.shape
    F = w_down.shape[0]
    tm = _row_tile(lay, tm)
    assert tm % lay.nc == 0, "context tiles must hold whole segments"
    tiles_per_seg = lay.n // tm
    n_lat_tiles = lay.lat_rows // tm
    n_tiles = n_rows // tm
    tf = _pick(F, tf)
    nf = F // tf
    halo = BF16_ROWS
    per = tm // halo
    seg = _seg_index(lay, tm)
    vec = lambda: pl.BlockSpec((None, 1, D), lambda i, j: (seg(i), 0, 0))
    return pl.pallas_call(
        functools.partial(_ffn_kernel, n_lat_tiles, n_tiles - n_lat_tiles, tiles_per_seg, lay.nc, tm, halo,
                          tf // _pick(tf, sub)),
        grid=(n_tiles, nf),
        in_specs=[
            pl.BlockSpec((tm, D), lambda i, j: (i, 0)),
            pl.BlockSpec((halo, D), lambda i, j: (jnp.maximum(i * per - 1, 0), 0)),
            pl.BlockSpec((halo, D), lambda i, j: (jnp.minimum((i + 1) * per, M // halo - 1), 0)),
            pl.BlockSpec((1, D), lambda i, j: (0, 0)),
            vec(), vec(), vec(),
            pl.BlockSpec((D, tf), lambda i, j: (0, j)),
            pl.BlockSpec((D, tf), lambda i, j: (0, nf + j)),
            pl.BlockSpec((CONV_W, tf), lambda i, j: (0, j)),
            pl.BlockSpec((CONV_W, tf), lambda i, j: (0, nf + j)),
            pl.BlockSpec((1, tf), lambda i, j: (0, j)),
            pl.BlockSpec((1, tf), lambda i, j: (0, nf + j)),
            pl.BlockSpec((tf, D), lambda i, j: (j, 0)),
        ],
        out_specs=pl.BlockSpec((tm, D), lambda i, j: (i, 0)),
        out_shape=jax.ShapeDtypeStruct((n_rows, D), _F32),
        scratch_shapes=[pltpu.VMEM((tm + 2 * halo, D), _BF16), pltpu.VMEM((tm, D), _F32)],
        compiler_params=_cparams("parallel", "arbitrary"),
        name="conv_gated_ffn",
    )(x, x, x, g, shift, scale, gate, w_up, w_up, conv_w, conv_w, conv_b, conv_b, w_down)


def _shift_mix_kernel(lay, tm, halo, x_ref, xp_ref, xn_ref, g_ref, sh_ref, sc_ref, mu_ref, *o_refs):
    i = pl.program_id(0)
    g, sh, sc = g_ref[...], sh_ref[...], sc_ref[...]
    h = _normmod(x_ref[...], g, sh, sc)
    hp = _normmod(xp_ref[...], g, sh, sc)
    hn = _normmod(xn_ref[...], g, sh, sc)
    ext = jnp.concatenate([hp, h, hn], axis=0)
    n_ext = tm + 2 * halo
    has_prev, has_next = _seg_masks(i, tm, h.shape, lay)
    prev = jnp.where(has_prev, pltpu.roll(ext, 1, 0)[halo:halo + tm], 0.0)
    nxt = jnp.where(has_next, pltpu.roll(ext, n_ext - 1, 0)[halo:halo + tm], 0.0)
    dx = 0.5 * (prev + nxt) - h
    mu = mu_ref[...]
    for m, o_ref in enumerate(o_refs):
        o_ref[...] = (h + dx * mu[m:m + 1]).astype(o_ref.dtype)


def _shift_mix(x, g, shift, scale, mu, lay, tm=256):
    M, D = x.shape
    tm = _row_tile(lay, tm)
    halo = 8
    per = tm // halo
    seg = _seg_index(lay, tm)
    vec = lambda: pl.BlockSpec((None, 1, D), lambda i: (seg(i), 0, 0))
    n_mix = mu.shape[0]
    return pl.pallas_call(
        functools.partial(_shift_mix_kernel, lay, tm, halo),
        grid=(M // tm,),
        in_specs=[
            pl.BlockSpec((tm, D), lambda i: (i, 0)),
            pl.BlockSpec((halo, D), lambda i: (jnp.maximum(i * per - 1, 0), 0)),
            pl.BlockSpec((halo, D), lambda i: (jnp.minimum((i + 1) * per, M // halo - 1), 0)),
            pl.BlockSpec((1, D), lambda i: (0, 0)),
            vec(), vec(),
            pl.BlockSpec((n_mix, D), lambda i: (0, 0)),
        ],
        out_specs=[pl.BlockSpec((tm, D), lambda i: (i, 0)) for _ in range(n_mix)],
        out_shape=[jax.ShapeDtypeStruct((M, D), _BF16) for _ in range(n_mix)],
        compiler_params=_cparams("parallel"),
        name="rwkv_shift_mix",
    )(x, x, x, g, shift, scale, mu)


def _vres_kernel(x_ref, w1_ref, w2_ref, b_ref, v_ref, vf_ref, o_ref):
    t = _dot(x_ref[...], w1_ref[...]).astype(_BF16)
    gate = _sigmoid(_dot(t, w2_ref[...]) + b_ref[...])
    v = v_ref[...]
    o_ref[...] = v + (vf_ref[...] - v) * gate


def _vres(x, v1, v2, v0, v, v_first, tm=512):
    M, D = x.shape
    w1 = _pad_cols(v1).astype(_BF16)
    w2 = _pad_rows(v2).astype(_BF16)
    R = w1.shape[1]
    tm = _pick(M, tm)
    return pl.pallas_call(
        _vres_kernel,
        grid=(M // tm,),
        in_specs=[
            pl.BlockSpec((tm, D), lambda i: (i, 0)),
            pl.BlockSpec((D, R), lambda i: (0, 0)),
            pl.BlockSpec((R, D), lambda i: (0, 0)),
            pl.BlockSpec((1, D), lambda i: (0, 0)),
            pl.BlockSpec((tm, D), lambda i: (i, 0)),
            pl.BlockSpec((tm, D), lambda i: (i, 0)),
        ],
        out_specs=pl.BlockSpec((tm, D), lambda i: (i, 0)),
        out_shape=jax.ShapeDtypeStruct((M, D), _F32),
        compiler_params=_cparams("parallel"),
        name="rwkv_value_residual",
    )(x, w1, w2, v0.reshape(1, D), v, v_first)


def _wkv_kernel(reverse, final, C, LW, r_ref, k_ref, v_ref, wp_ref, ap_ref, kk_ref, ka_ref, rk_ref, *refs):
    if final:
        y0_ref, b0_ref, g_ref, lng_ref, lnb_ref, o_ref, s_ref = refs
    else:
        y_ref, bon_ref, s_ref = refs
    j = pl.program_id(2)

    @pl.when(j == 0)
    def _():
        s_ref[...] = jnp.zeros_like(s_ref)

    hpg = WKV_GROUP // RWKV_HEAD
    groups = range(LW // WKV_GROUP)
    ng = len(groups)
    sls = [slice(g * WKV_GROUP, (g + 1) * WKV_GROUP) for g in groups]
    row = lax.broadcasted_iota(jnp.int32, (C, C), 0)
    col = lax.broadcasted_iota(jnp.int32, (C, C), 1)
    tri = jnp.where((col >= row) if reverse else (col <= row), 1.0, 0.0).astype(_BF16)
    rc = lax.broadcasted_iota(jnp.int32, (C, WKV_GROUP), 0)
    cc = lax.rem(lax.broadcasted_iota(jnp.int32, (C, WKV_GROUP), 1), C)
    strict = (cc > rc) if reverse else (cc < rc)
    incl = (cc >= rc) if reverse else (cc <= rc)
    eye_cat = jnp.where(cc == rc, 1.0, 0.0).astype(_F32)
    br = lax.broadcasted_iota(jnp.int32, (WKV_GROUP, WKV_GROUP), 0) // RWKV_HEAD
    bc = lax.broadcasted_iota(jnp.int32, (WKV_GROUP, WKV_GROUP), 1) // RWKV_HEAD
    same_head = br == bc
    head_ones = jnp.where(same_head, 1.0, 0.0).astype(_BF16)

    def bd(x):
        return jnp.where(same_head, jnp.concatenate([x] * hpg, axis=0), jnp.zeros((), x.dtype))

    cast = lambda x: x.astype(_BF16)

    v = [v_ref[:, sl] for sl in sls]
    lw = []
    for sl in sls:
        wp = wp_ref[:, sl]
        sp = jnp.maximum(-wp, 0.0) + jnp.log(1.0 + jnp.exp(-jnp.abs(wp)))
        lw.append(-jnp.exp(-sp - 0.5))
    cl = [_ones_dot_left(tri, x, 3) for x in lw]
    total = [c[0:1] if reverse else c[C - 1:C] for c in cl]

    a = [_sigmoid(ap_ref[:, sl]) for sl in sls]
    kk = [k_ref[:, sl] * kk_ref[:, sl] for sl in sls]
    kd = [k_ref[:, sl] * (1.0 + (a[g] - 1.0) * ka_ref[:, sl]) for g, sl in enumerate(sls)]
    rkd = [r_ref[:, sl] * kd[g] * rk_ref[:, sl] for g, sl in enumerate(sls)]
    sums = _ones_dot_rows(head_ones, [x * x for x in kk] + rkd, 2)
    kk = [x / jnp.maximum(jnp.sqrt(n), 1e-12) for x, n in zip(kk, sums[:ng])]
    bonus = [s * x for s, x in zip(sums[ng:], v)]
    b = [kk[g] * a[g] for g in groups]

    AR = [jnp.concatenate([cast(-kk[g] * jnp.exp(cl[g] - lw[g])), cast(r_ref[:, sl] * jnp.exp(cl[g]))], axis=0)
          for g, sl in enumerate(sls)]
    BKbd, BKp = [], []
    for g in groups:
        e_neg = jnp.exp(-cl[g])
        e_rem = jnp.exp(total[g] - cl[g])
        BKbd.append(jnp.concatenate([bd(cast(b[g] * e_neg)), bd(cast(kd[g] * e_neg))], axis=0))
        BKp.append(cast(jnp.concatenate([b[g] * e_rem, kd[g] * e_rem], axis=0)))
    vb = [cast(x) for x in v]
    Vbd = [bd(x) for x in vb]
    S0 = [s_ref[g] for g in groups]

    M = [_dot_nt(AR[g], BKbd[g]) for g in groups]
    ARS = [_dot_nt(AR[g], cast(S0[g])) for g in groups]
    Lab = [jnp.where(strict, m[:C, :WKV_GROUP], 0.0) for m in M]
    LMk = [cast(jnp.concatenate([jnp.where(strict, m[:C, WKV_GROUP:], 0.0),
                                 jnp.where(incl, m[C:, WKV_GROUP:], 0.0)], axis=0)) for m in M]
    Mrb = [cast(jnp.where(incl, m[C:, :WKV_GROUP], 0.0)) for m in M]
    LMkV = [_dot(LMk[g], Vbd[g]) for g in groups]

    Pw = [cast(x) for x in Lab]
    T = [eye_cat + x for x in Lab]
    Pbd = [bd(x) for x in Pw]
    for _ in range(int(math.log2(C)) - 1):
        Pw = [cast(_dot(x, xb)) for x, xb in zip(Pw, Pbd)]
        Pbd = [bd(x) for x in Pw]
        T = [t + _dot(cast(t), xb) for t, xb in zip(T, Pbd)]

    X = [ARS[g][:C] + LMkV[g][:C] for g in groups]
    Z = [cast(_dot(cast(T[g]), bd(cast(X[g])))) for g in groups]
    y = [ARS[g][C:] + LMkV[g][C:] + _dot(Mrb[g], bd(Z[g])) for g in groups]
    dS = [_dot_tn(jnp.concatenate([Z[g], vb[g]], axis=0), BKp[g]) for g in groups]
    s_ref[...] = jnp.stack([S0[g] * jnp.exp(total[g]) + jnp.where(same_head, dS[g], 0.0) for g in groups], axis=0)

    if final:
        inv = 1.0 / RWKV_HEAD
        wkv = [y[g] + y0_ref[:, sl] for g, sl in enumerate(sls)]
        cen = [x - s * inv for x, s in zip(wkv, _ones_dot_rows(head_ones, wkv, 3))]
        var = [s * inv for s in _ones_dot_rows(head_ones, [x * x for x in cen], 2)]
        outs = []
        for g, sl in enumerate(sls):
            normed = cen[g] * lax.rsqrt(var[g] + GN_EPS) * lng_ref[:, sl] + lnb_ref[:, sl]
            outs.append(((normed + bonus[g] + b0_ref[:, sl]) * g_ref[:, sl]).astype(o_ref.dtype))
        o_ref[...] = jnp.concatenate(outs, axis=1)
    else:
        y_ref[...] = jnp.concatenate(y, axis=1)
        bon_ref[...] = jnp.concatenate(bonus, axis=1)


def _wkv(r, k, v, wpre, apre, k_k, k_a, r_k, lay, reverse, final_args=None, lw=2048):
    M, D = r.shape
    C = WKV_CHUNK
    assert C == RWKV_HEAD and lay.nc % C == 0 and lay.n % C == 0 and D % WKV_GROUP == 0
    lw = _pick(D, lw)
    assert lw % WKV_GROUP == 0
    ncc, ncl = lay.nc // C, lay.n // C
    ctx_blk0 = lay.lat_rows // C

    def row_blk(b, j):
        if reverse:
            return jnp.where(j < ncc, ctx_blk0 + b * ncc + (ncc - 1 - j), b * ncl + (ncl - 1 - (j - ncc)))
        return jnp.where(j < ncc, ctx_blk0 + b * ncc + j, b * ncl + (j - ncc))

    tile = lambda: pl.BlockSpec((C, lw), lambda b, h, j: (row_blk(b, j), h))
    vec = lambda: pl.BlockSpec((1, lw), lambda b, h, j: (0, h))
    in_specs = [tile() for _ in range(5)] + [vec() for _ in range(3)]
    args = [r, k, v, wpre, apre, k_k.reshape(1, D), k_a.reshape(1, D), r_k.reshape(1, D)]
    final = final_args is not None
    if final:
        y0, b0, g, lng, lnb = final_args
        in_specs += [tile(), tile(), tile(), vec(), vec()]
        args += [y0, b0, g, lng.reshape(1, D), lnb.reshape(1, D)]
        out_specs = tile()
        out_shape = jax.ShapeDtypeStruct((M, D), _BF16)
    else:
        out_specs = [tile(), tile()]
        out_shape = [jax.ShapeDtypeStruct((M, D), _F32), jax.ShapeDtypeStruct((M, D), _F32)]
    return pl.pallas_call(
        functools.partial(_wkv_kernel, reverse, final, C, lw),
        grid=(lay.B, D // lw, ncc + ncl),
        in_specs=in_specs,
        out_specs=out_specs,
        out_shape=out_shape,
        scratch_shapes=[pltpu.VMEM((lw // WKV_GROUP, WKV_GROUP, WKV_GROUP), _F32)],
        compiler_params=_cparams("parallel", "parallel", "arbitrary"),
        name="wkv7_scan_rev" if reverse else "wkv7_scan_fwd",
    )(*args)


def _final_norm_kernel(x_ref, g_ref, o_ref):
    x = x_ref[...]
    o_ref[...] = x * lax.rsqrt(jnp.mean(x * x, axis=-1, keepdims=True) + NORM_EPS) * g_ref[...]


def _final_norm(x, g, n_rows, tm=512):
    D = x.shape[1]
    tm = _pick(n_rows, tm)
    return pl.pallas_call(
        _final_norm_kernel,
        grid=(n_rows // tm,),
        in_specs=[pl.BlockSpec((tm, D), lambda i: (i, 0)), pl.BlockSpec((1, D), lambda i: (0, 0))],
        out_specs=pl.BlockSpec((tm, D), lambda i: (i, 0)),
        out_shape=jax.ShapeDtypeStruct((n_rows, D), _F32),
        compiler_params=_cparams("parallel"),
        name="final_rms_norm",
    )(x, g.reshape(1, D))


def _rope_tables(lay):
    n = lay.n
    t = jnp.arange(n, dtype=jnp.int32)
    row = (t // GRID_W).astype(_F32)
    col = (t % GRID_W).astype(_F32)
    axis_dim = HEAD_DIM // 2
    inv_freq = ROPE_THETA ** (-jnp.arange(0, axis_dim, 2, dtype=_F32) / axis_dim)
    ang = jnp.concatenate([row[:, None] * inv_freq, col[:, None] * inv_freq], axis=-1)
    cos, sin = jnp.cos(ang), jnp.sin(ang)
    cos_l = jnp.concatenate([cos, cos], axis=-1)
    sin_l = jnp.concatenate([-sin, sin], axis=-1)
    n_ctx_rows = lay.B * lay.nc
    cos_f = jnp.concatenate([jnp.tile(cos_l, (lay.B, 1)), jnp.ones((n_ctx_rows, HEAD_DIM), _F32)], axis=0)
    sin_f = jnp.concatenate([jnp.tile(sin_l, (lay.B, 1)), jnp.zeros((n_ctx_rows, HEAD_DIM), _F32)], axis=0)
    return cos_f, sin_f


def _deinterleave_heads(w, n_heads):
    lead = w.shape[:-1]
    w = w.reshape(*lead, n_heads, HEAD_DIM // 2, 2)
    w = jnp.swapaxes(w, -1, -2)
    return w.reshape(*lead, n_heads * HEAD_DIM)


def _pad_cols(w, mult=LANES):
    pad = (-w.shape[-1]) % mult
    return jnp.pad(w, [(0, 0)] * (w.ndim - 1) + [(0, pad)]) if pad else w


def _pad_rows(w, mult=LANES):
    pad = (-w.shape[-2]) % mult
    return jnp.pad(w, [(0, 0)] * (w.ndim - 2) + [(0, pad), (0, 0)]) if pad else w


def kernel(x, c, ctx, c_ctx, w_mod, b_mod, norm1_g, norm2_g, attn_w_in, attn_w_out, q_norm_g, k_norm_g, rwkv_mu, rwkv_w_r, rwkv_w_k, rwkv_w_v, rwkv_w_o, rwkv_decay_w0, rwkv_decay_w1, rwkv_decay_w2, rwkv_iclr_a0, rwkv_iclr_a1, rwkv_iclr_a2, rwkv_gate_g1, rwkv_gate_g2, rwkv_k_k, rwkv_k_a, rwkv_r_k, rwkv_lnx_g, rwkv_lnx_b, rwkv_vres_v0, rwkv_vres_v1, rwkv_vres_v2, ffn_w_up, ffn_conv_w, ffn_conv_b, ffn_w_down, final_norm_g):
    B, n, D = x.shape
    nc = ctx.shape[1]
    depth = w_mod.shape[0]
    lay = Layout(B, n, nc, B * n, B * (n + nc))
    attn_w = N_ATTN_HEADS * HEAD_DIM
    kv_w = N_KV_HEADS * HEAD_DIM
    fw = D - attn_w
    bf = lambda w: w.astype(_BF16)

    xs = jnp.concatenate([x.reshape(B * n, D), ctx.reshape(B * nc, D)], axis=0)

    cvec = jnp.concatenate([c, c_ctx[None, :], jnp.zeros((7 - B % 8, D), _F32)], axis=0)
    mods = _modulation(cvec, w_mod, b_mod)[:, :B + 1]
    mods = mods.reshape(depth, B + 1, 6, 1, D)
    cos_f, sin_f = _rope_tables(lay)
    qg_scale = HEAD_DIM ** -0.5

    v_first = None
    for i in range(depth):
        last = i == depth - 1
        jl = i // 2
        m = [mods[i, :, t] for t in range(6)]
        g1 = norm1_g[i].reshape(1, D)
        if i % 2 == 0:
            w_in = attn_w_in[jl]
            w_in = bf(jnp.concatenate([
                _deinterleave_heads(w_in[:, :attn_w], N_ATTN_HEADS),
                _deinterleave_heads(w_in[:, attn_w:attn_w + kv_w], N_KV_HEADS),
                w_in[:, attn_w + kv_w:]], axis=1))
            qg = (_deinterleave_heads(q_norm_g[jl], 1) * qg_scale).reshape(1, HEAD_DIM)
            kg = _deinterleave_heads(k_norm_g[jl], 1).reshape(1, HEAD_DIM)
            u = _attn_in_proj(xs, g1, m[0], m[1], w_in, qg, kg, cos_f, sin_f, lay)
            o = _attention(u, lay)
            fm = jnp.concatenate([_fourier_mix(u, lay, fw), _fourier_mix(u, lay, fw, ctx=True)], axis=0)
            w_out = bf(attn_w_out[jl])
            xs = _out_proj([o, fm], [w_out[:attn_w], w_out[attn_w:]], xs, m[2], lay)
        else:
            x_r, x_w, x_k, x_v, x_a, x_g = _shift_mix(xs, g1, m[0], m[1], rwkv_mu[jl], lay)
            r = _linear(x_r, bf(rwkv_w_r[jl]), name="rwkv_r")
            k = _linear(x_k, bf(rwkv_w_k[jl]), name="rwkv_k")
            v = _linear(x_v, bf(rwkv_w_v[jl]), name="rwkv_v")
            if jl == 0:
                v_first = v
            else:
                v = _vres(x_v, rwkv_vres_v1[jl - 1], rwkv_vres_v2[jl - 1], rwkv_vres_v0[jl - 1], v, v_first)
            (g,) = _lora(x_g, [rwkv_gate_g1[jl]], [rwkv_gate_g2[jl]], act="sigmoid", name="rwkv_gate")
            wpre = _lora(x_w, list(rwkv_decay_w1[jl]), list(rwkv_decay_w2[jl]), list(rwkv_decay_w0[jl]),
                         act="tanh", name="rwkv_decay")
            apre = _lora(x_a, list(rwkv_iclr_a1[jl]), list(rwkv_iclr_a2[jl]), list(rwkv_iclr_a0[jl]),
                         name="rwkv_iclr")
            kk_, ka_, rk_ = rwkv_k_k[jl], rwkv_k_a[jl], rwkv_r_k[jl].reshape(D)
            y0, bon0 = _wkv(r, k, v, wpre[0], apre[0], kk_, ka_, rk_, lay, reverse=False)
            z = _wkv(r, k, v, wpre[1], apre[1], kk_, ka_, rk_, lay, reverse=True,
                     final_args=(y0, bon0, g, rwkv_lnx_g[jl], rwkv_lnx_b[jl]))
            xs = _out_proj([z], [bf(rwkv_w_o[jl])], xs, m[2], lay)
        n_rows = lay.lat_rows if last else lay.rows
        xs = _ffn(xs, norm2_g[i].reshape(1, D), m[3], m[4], m[5], bf(ffn_w_up[i]), ffn_conv_w[i],
                  ffn_conv_b[i].reshape(1, -1), bf(ffn_w_down[i]), lay, n_rows)
    out = _final_norm(xs, final_norm_g, lay.lat_rows)
    return out.reshape(B, n, D)
```

```python
import collections
import functools
import math

import jax
import jax.numpy as jnp
from jax import lax
from jax.experimental import pallas as pl
from jax.experimental.pallas import tpu as pltpu

HEAD_DIM = 128
N_ATTN_HEADS = 12
N_KV_HEADS = 4
GRID_W = 64
ROPE_THETA = 10000.0
N_FOURIER_GROUPS = 4
RWKV_HEAD = 64
CONV_W = 3
NORM_EPS = 1e-6
GN_EPS = 64e-5

LANES = 128
MXU_TILE = 256
BF16_ROWS = 16
VMEM_LIMIT = 56 * 1024 * 1024
WKV_CHUNK = 64
WKV_GROUP = MXU_TILE

_F32 = jnp.float32
_BF16 = jnp.bfloat16
_HI = lax.Precision.HIGHEST

Layout = collections.namedtuple("Layout", "B n nc lat_rows rows")


def _cparams(*sem):
    return pltpu.CompilerParams(dimension_semantics=sem, vmem_limit_bytes=VMEM_LIMIT)


def _pick(total, pref):
    t = min(pref, total)
    while total % t:
        t -= 1
    return t


def _row_tile(lay, pref):
    return _pick(lay.B * lay.nc, _pick(lay.n, pref))


def _seg_index(lay, tm):
    def f(i):
        r0 = i * tm
        return jnp.where(r0 < lay.lat_rows, r0 // lay.n, lay.B)
    return f


def _seg_masks(i, tm, shape, lay):
    t = lax.broadcasted_iota(jnp.int32, shape, 0)
    r0 = i * tm
    is_lat = r0 < lay.lat_rows
    if tm >= lay.nc:
        pc = lax.rem(t, lay.nc)
    else:
        pc = t + lax.rem(r0 - lay.lat_rows, lay.nc)
    pos = jnp.where(is_lat, t + lax.rem(r0, lay.n), pc)
    last = jnp.where(is_lat, lay.n - 1, lay.nc - 1)
    return pos != 0, pos != last


def _normmod(x, g, shift, scale):
    ms = jnp.mean(x * x, axis=-1, keepdims=True)
    return (x * lax.rsqrt(ms + NORM_EPS)) * g * (1.0 + scale) + shift


def _sigmoid(x):
    return 1.0 / (1.0 + jnp.exp(-x))


def _dot(a, b, prec=None):
    return jnp.dot(a, b, preferred_element_type=_F32, precision=prec)


def _dot_nt(a, b):
    return lax.dot_general(a, b, (((1,), (1,)), ((), ())), preferred_element_type=_F32)


def _dot_tn(a, b):
    return lax.dot_general(a, b, (((0,), (0,)), ((), ())), preferred_element_type=_F32)


def _bf16_pieces(x, parts):
    pieces = []
    rem = x
    for i in range(parts):
        piece = rem.astype(_BF16)
        pieces.append(piece)
        if i + 1 < parts:
            rem = rem - piece.astype(_F32)
    return pieces


def _ones_dot_left(ones, x, parts):
    acc = None
    for piece in _bf16_pieces(x, parts):
        d = _dot(ones, piece)
        acc = d if acc is None else acc + d
    return acc


def _ones_dot_rows(ones, xs, parts):
    rows = xs[0].shape[0]
    stacked = jnp.concatenate([p for x in xs for p in _bf16_pieces(x, parts)], axis=0)
    res = _dot(stacked, ones)
    outs = []
    for n in range(len(xs)):
        base = n * parts * rows
        acc = res[base:base + rows]
        for i in range(1, parts):
            acc = acc + res[base + i * rows:base + (i + 1) * rows]
        outs.append(acc)
    return outs


def _mod_kernel(c_ref, w_ref, b_ref, o_ref):
    c = c_ref[...]
    s = c * _sigmoid(c)
    o_ref[...] = _dot(s, w_ref[...], _HI) + b_ref[...]


def _modulation(cvec, w_mod, b_mod):
    depth, D, N = w_mod.shape
    rows = cvec.shape[0]
    tn = _pick(N, 512)
    return pl.pallas_call(
        _mod_kernel,
        grid=(depth, N // tn),
        in_specs=[
            pl.BlockSpec((rows, D), lambda l, j: (0, 0)),
            pl.BlockSpec((None, D, tn), lambda l, j: (l, 0, j)),
            pl.BlockSpec((None, 1, tn), lambda l, j: (l, 0, j)),
        ],
        out_specs=pl.BlockSpec((None, rows, tn), lambda l, j: (l, 0, j)),
        out_shape=jax.ShapeDtypeStruct((depth, rows, N), _F32),
        compiler_params=_cparams("parallel", "parallel"),
        name="adaln_modulation",
    )(cvec, w_mod, b_mod.reshape(depth, 1, N))


def _linear_kernel(x_ref, w_ref, o_ref):
    o_ref[...] = _dot(x_ref[...], w_ref[...]).astype(o_ref.dtype)


def _linear(x, w, out_dtype=_F32, tm=512, tn=2048, name="linear"):
    M, K = x.shape
    N = w.shape[1]
    tm, tn = _pick(M, tm), _pick(N, tn)
    return pl.pallas_call(
        _linear_kernel,
        grid=(M // tm, N // tn),
        in_specs=[pl.BlockSpec((tm, K), lambda i, j: (i, 0)), pl.BlockSpec((K, tn), lambda i, j: (0, j))],
        out_specs=pl.BlockSpec((tm, tn), lambda i, j: (i, j)),
        out_shape=jax.ShapeDtypeStruct((M, N), out_dtype),
        compiler_params=_cparams("parallel", "arbitrary"),
        name=name,
    )(x, w)


def _lora_kernel(act, ranks, has_bias, x_ref, w1_ref, *refs):
    n_out = len(ranks)
    w2_refs = refs[:n_out]
    b_refs = refs[n_out:2 * n_out] if has_bias else ()
    o_refs = refs[-n_out:]
    t = _dot(x_ref[...], w1_ref[...])
    if act == "tanh":
        t = jnp.tanh(t)
    elif act == "sigmoid":
        t = _sigmoid(t)
    t = t.astype(_BF16)
    off = 0
    for d in range(n_out):
        y = _dot(t[:, off:off + ranks[d]], w2_refs[d][...])
        if has_bias:
            y = y + b_refs[d][...]
        o_refs[d][...] = y
        off += ranks[d]


def _lora(x, w1s, w2s, biases=None, act=None, tm=512, name="lora"):
    M, D = x.shape
    N = w2s[0].shape[1]
    tm = _pick(M, tm)
    w1 = jnp.concatenate([_pad_cols(w) for w in w1s], axis=1).astype(_BF16)
    w2p = [_pad_rows(w).astype(_BF16) for w in w2s]
    ranks = tuple(w.shape[0] for w in w2p)
    in_specs = [pl.BlockSpec((tm, D), lambda i: (i, 0)), pl.BlockSpec(w1.shape, lambda i: (0, 0))]
    in_specs += [pl.BlockSpec(w.shape, lambda i: (0, 0)) for w in w2p]
    args = [x, w1, *w2p]
    if biases is not None:
        in_specs += [pl.BlockSpec((1, N), lambda i: (0, 0)) for _ in biases]
        args += [b.reshape(1, N) for b in biases]
    return pl.pallas_call(
        functools.partial(_lora_kernel, act, ranks, biases is not None),
        grid=(M // tm,),
        in_specs=in_specs,
        out_specs=[pl.BlockSpec((tm, N), lambda i: (i, 0)) for _ in w2p],
        out_shape=[jax.ShapeDtypeStruct((M, N), _F32) for _ in w2p],
        compiler_params=_cparams("parallel"),
        name=name,
    )(*args)


def _attn_in_kernel(sw, n_q, n_qk, x_ref, g_ref, sh_ref, sc_ref, w_ref, qg_ref, kg_ref, cos_ref, sin_ref, o_ref):
    h = _normmod(x_ref[...], g_ref[...], sh_ref[...], sc_ref[...]).astype(_BF16)
    cos, sin = cos_ref[...], sin_ref[...]
    n_blocks = w_ref.shape[1] // sw

    def finish(s, acc):
        cols = slice(s * sw, (s + 1) * sw)
        if s >= n_qk:
            o_ref[:, cols] = acc.astype(o_ref.dtype)
            return
        gain = qg_ref[...] if s < n_q else kg_ref[...]
        for hd in range(sw // HEAD_DIM):
            a = acc[:, hd * HEAD_DIM:(hd + 1) * HEAD_DIM]
            y = a * lax.rsqrt(jnp.mean(a * a, axis=-1, keepdims=True) + NORM_EPS) * gain
            y = y * cos + pltpu.roll(y, HEAD_DIM // 2, 1) * sin
            o_ref[:, s * sw + hd * HEAD_DIM:s * sw + (hd + 1) * HEAD_DIM] = y.astype(o_ref.dtype)

    acc = _dot(h, w_ref[:, 0:sw])
    for s in range(n_blocks):
        nxt = _dot(h, w_ref[:, (s + 1) * sw:(s + 2) * sw]) if s + 1 < n_blocks else None
        finish(s, acc)
        acc = nxt


def _attn_in_proj(x, g, shift, scale, w, qg, kg, cos, sin, lay, tm=512):
    M, D = x.shape
    N = w.shape[1]
    tm = _row_tile(lay, tm)
    attn_w = N_ATTN_HEADS * HEAD_DIM
    kv_w = N_KV_HEADS * HEAD_DIM
    sw = math.gcd(math.gcd(attn_w, kv_w), MXU_TILE)
    seg = _seg_index(lay, tm)
    vec = lambda: pl.BlockSpec((None, 1, D), lambda i: (seg(i), 0, 0))
    return pl.pallas_call(
        functools.partial(_attn_in_kernel, sw, attn_w // sw, (attn_w + kv_w) // sw),
        grid=(M // tm,),
        in_specs=[
            pl.BlockSpec((tm, D), lambda i: (i, 0)),
            pl.BlockSpec((1, D), lambda i: (0, 0)),
            vec(), vec(),
            pl.BlockSpec((D, N), lambda i: (0, 0)),
            pl.BlockSpec((1, HEAD_DIM), lambda i: (0, 0)),
            pl.BlockSpec((1, HEAD_DIM), lambda i: (0, 0)),
            pl.BlockSpec((tm, HEAD_DIM), lambda i: (i, 0)),
            pl.BlockSpec((tm, HEAD_DIM), lambda i: (i, 0)),
        ],
        out_specs=pl.BlockSpec((tm, N), lambda i: (i, 0)),
        out_shape=jax.ShapeDtypeStruct((M, N), _BF16),
        compiler_params=_cparams("parallel"),
        name="attn_in_proj",
    )(x, g, shift, scale, w, qg, kg, cos, sin)


def _attn_kernel(n_lat_tiles, g_per_kv, kb, q_ref, kc_ref, vc_ref, kl_ref, vl_ref, o_ref):
    qi = pl.program_id(2)
    tq = q_ref.shape[0]
    q = jnp.concatenate([q_ref[:, g * HEAD_DIM:(g + 1) * HEAD_DIM] for g in range(g_per_kv)], axis=0)

    def finish(acc, l):
        o = acc / l
        for g in range(g_per_kv):
            o_ref[:, g * HEAD_DIM:(g + 1) * HEAD_DIM] = o[g * tq:(g + 1) * tq].astype(o_ref.dtype)

    s_c = _dot_nt(q, kc_ref[...])
    m = jnp.max(s_c, axis=-1, keepdims=True)
    p = jnp.exp(s_c - m)
    l = jnp.sum(p, axis=-1, keepdims=True)
    acc = _dot(p.astype(_BF16), vc_ref[...])

    @pl.when(qi < n_lat_tiles)
    def _():
        n_blocks = kl_ref.shape[0] // kb
        mj, lj, accj = m, l, acc
        s_next = _dot_nt(q, kl_ref[0:kb, :])
        for j in range(n_blocks):
            s = s_next
            if j + 1 < n_blocks:
                s_next = _dot_nt(q, kl_ref[(j + 1) * kb:(j + 2) * kb, :])
            m_new = jnp.maximum(mj, jnp.max(s, axis=-1, keepdims=True))
            alpha = jnp.exp(mj - m_new)
            pj = jnp.exp(s - m_new)
            lj = lj * alpha + jnp.sum(pj, axis=-1, keepdims=True)
            accj = accj * alpha + _dot(pj.astype(_BF16), vl_ref[j * kb:(j + 1) * kb, :])
            mj = m_new
        finish(accj, lj)

    @pl.when(qi >= n_lat_tiles)
    def _():
        finish(acc, l)


def _attention(u, lay, tq=256, kb=512):
    M = u.shape[0]
    g_per_kv = N_ATTN_HEADS // N_KV_HEADS
    qw = g_per_kv * HEAD_DIM
    attn_w = N_ATTN_HEADS * HEAD_DIM
    tq = _pick(lay.nc, tq)
    nlt, nct = lay.n // tq, lay.nc // tq
    k_blk = attn_w // HEAD_DIM
    v_blk = k_blk + N_KV_HEADS
    ctx_blk0 = lay.lat_rows // lay.nc

    def q_map(b, kv, qi):
        row = jnp.where(qi < nlt, b * nlt + qi, lay.lat_rows // tq + b * nct + (qi - nlt))
        return (row, kv)

    return pl.pallas_call(
        functools.partial(_attn_kernel, nlt, g_per_kv, _pick(lay.n, kb)),
        grid=(lay.B, N_KV_HEADS, nlt + nct),
        in_specs=[
            pl.BlockSpec((tq, qw), q_map),
            pl.BlockSpec((lay.nc, HEAD_DIM), lambda b, kv, qi: (ctx_blk0 + b, k_blk + kv)),
            pl.BlockSpec((lay.nc, HEAD_DIM), lambda b, kv, qi: (ctx_blk0 + b, v_blk + kv)),
            pl.BlockSpec((lay.n, HEAD_DIM), lambda b, kv, qi: (b, k_blk + kv)),
            pl.BlockSpec((lay.n, HEAD_DIM), lambda b, kv, qi: (b, v_blk + kv)),
        ],
        out_specs=pl.BlockSpec((tq, qw), q_map),
        out_shape=jax.ShapeDtypeStruct((M, attn_w), _BF16),
        compiler_params=_cparams("parallel", "parallel", "arbitrary"),
        name="gqa_attention",
    )(u, u, u, u, u)


def _dft_kernel(f_ref, cc_ref, sc_ref, cn_ref, sn_ref, o_ref, a_ref, b_ref):
    @pl.when(pl.program_id(1) == 0)
    def _():
        f = f_ref[...]
        a_ref[...] = _dot(f, cc_ref[...]).astype(_BF16)
        b_ref[...] = _dot(f, sc_ref[...]).astype(_BF16)

    o_ref[...] = (_dot(cn_ref[...], a_ref[...]) - _dot(sn_ref[...], b_ref[...])).astype(o_ref.dtype)


def _cos_sin_matrix(m, split=64):
    j = jnp.arange(m, dtype=jnp.int32)[:, None]
    w = 2.0 * math.pi / m

    def cs(k):
        ang = w * ((j * k[None, :]) % m).astype(_F32)
        return jnp.cos(ang), jnp.sin(ang)

    if m % split or m <= 8 * split:
        return cs(jnp.arange(m, dtype=jnp.int32))
    ca, sa = cs(split * jnp.arange(m // split, dtype=jnp.int32))
    cb, sb = cs(jnp.arange(split, dtype=jnp.int32))
    cos = ca[:, :, None] * cb[:, None, :] - sa[:, :, None] * sb[:, None, :]
    sin = sa[:, :, None] * cb[:, None, :] + ca[:, :, None] * sb[:, None, :]
    return cos.reshape(m, m), sin.reshape(m, m)


def _dft_tables(n, group, n_groups):
    cn, sn = _cos_sin_matrix(n)
    scale = 1.0 / math.sqrt(n * group)
    cg, sg = _cos_sin_matrix(group)
    eye = jnp.eye(n_groups, dtype=_F32)
    return ((cn * scale).astype(_BF16), (sn * scale).astype(_BF16),
            jnp.kron(eye, cg).astype(_BF16), jnp.kron(eye, sg).astype(_BF16))


def _fourier_mix(u, lay, fw, ctx=False, tm=512):
    M, N = u.shape
    n = lay.nc if ctx else lay.n
    tm = _pick(n, tm)
    nt = n // tm
    cn, sn, cc, sc = _dft_tables(n, fw // N_FOURIER_GROUPS, N_FOURIER_GROUPS)
    f_blk = N // fw - 1
    seg0 = lay.lat_rows // n if ctx else 0
    return pl.pallas_call(
        _dft_kernel,
        grid=(lay.B, nt),
        in_specs=[
            pl.BlockSpec((n, fw), lambda b, i: (seg0 + b, f_blk)),
            pl.BlockSpec((fw, fw), lambda b, i: (0, 0)),
            pl.BlockSpec((fw, fw), lambda b, i: (0, 0)),
            pl.BlockSpec((tm, n), lambda b, i: (i, 0)),
            pl.BlockSpec((tm, n), lambda b, i: (i, 0)),
        ],
        out_specs=pl.BlockSpec((tm, fw), lambda b, i: (b * nt + i, 0)),
        out_shape=jax.ShapeDtypeStruct((lay.B * n, fw), _BF16),
        scratch_shapes=[pltpu.VMEM((n, fw), _BF16), pltpu.VMEM((n, fw), _BF16)],
        compiler_params=_cparams("parallel", "arbitrary"),
        name="fourier_mix_ctx" if ctx else "fourier_mix_lat",
    )(u, cc, sc, cn, sn)


def _out_proj_kernel(n_lhs, *refs):
    lhs = refs[:n_lhs]
    ws = refs[n_lhs:2 * n_lhs]
    x_ref, gate_ref, o_ref = refs[2 * n_lhs], refs[2 * n_lhs + 1], refs[2 * n_lhs + 2]
    acc = _dot(lhs[0][...], ws[0][...])
    for a, w in zip(lhs[1:], ws[1:]):
        acc = acc + _dot(a[...], w[...])
    o_ref[...] = x_ref[...] + gate_ref[...] * acc


def _out_proj(lhs, ws, x, gate, lay, tm=512):
    M, D = x.shape
    tm = _row_tile(lay, tm)
    seg = _seg_index(lay, tm)
    in_specs = [pl.BlockSpec((tm, a.shape[1]), lambda i: (i, 0)) for a in lhs]
    in_specs += [pl.BlockSpec(w.shape, lambda i: (0, 0)) for w in ws]
    in_specs += [pl.BlockSpec((tm, D), lambda i: (i, 0)), pl.BlockSpec((None, 1, D), lambda i: (seg(i), 0, 0))]
    return pl.pallas_call(
        functools.partial(_out_proj_kernel, len(lhs)),
        grid=(M // tm,),
        in_specs=in_specs,
        out_specs=pl.BlockSpec((tm, D), lambda i: (i, 0)),
        out_shape=jax.ShapeDtypeStruct((M, D), _F32),
        compiler_params=_cparams("parallel"),
        name="out_proj_residual",
    )(*lhs, *ws, x, gate)


def _ffn_kernel(n_lat_tiles, n_ctx_tiles, tiles_per_seg, nc, tm, halo, n_sub, x_ref, xp_ref, xn_ref, g_ref, sh_ref,
                sc_ref, gate_ref, wg_ref, wv_ref, cwg_ref, cwv_ref, cbg_ref, cbv_ref, wd_ref, o_ref, h_ref, acc_ref,
                u_ref):
    i, j = pl.program_id(0), pl.program_id(1)
    ext = tm + 2 * halo
    is_lat = i < n_lat_tiles

    @pl.when(j == 0)
    def _():
        g, sh, sc = g_ref[...], sh_ref[...], sc_ref[...]
        pos = lax.rem(i, tiles_per_seg)
        keep_prev = jnp.logical_and(is_lat, pos != 0)
        keep_next = jnp.logical_and(is_lat, pos != tiles_per_seg - 1)
        hp = _normmod(xp_ref[...], g, sh, sc)
        hn = _normmod(xn_ref[...], g, sh, sc)
        h_ref[0:halo, :] = jnp.where(keep_prev, hp, 0.0).astype(_BF16)
        h_ref[halo:halo + tm, :] = _normmod(x_ref[...], g, sh, sc).astype(_BF16)
        h_ref[halo + tm:ext, :] = jnp.where(keep_next, hn, 0.0).astype(_BF16)
        acc_ref[...] = jnp.zeros_like(acc_ref)

    def step(inner_boundaries):
        h = h_ref[...]
        tf = wg_ref.shape[1]
        ts = tf // n_sub
        subs = [slice(s * ts, (s + 1) * ts) for s in range(n_sub)]

        def conv(u_ref, cw, cb):
            up = u_ref[halo - 1:halo - 1 + tm, :]
            un = u_ref[halo + 1:halo + 1 + tm, :]
            if inner_boundaries:
                t = lax.rem(lax.broadcasted_iota(jnp.int32, up.shape, 0), nc)
                up = jnp.where(t != 0, up, 0.0)
                un = jnp.where(t != nc - 1, un, 0.0)
            return up * cw[0:1] + u_ref[halo:halo + tm, :] * cw[1:2] + un * cw[2:3] + cb

        def up_proj(s):
            u_ref[s, 0] = _dot(h, wg_ref[:, subs[s]])
            u_ref[s, 1] = _dot(h, wv_ref[:, subs[s]])

        up_proj(0)
        for s, sl in enumerate(subs):
            if s + 1 < n_sub:
                up_proj(s + 1)
            gt = conv(u_ref.at[s, 0], cwg_ref[:, sl], cbg_ref[:, sl])
            vl = conv(u_ref.at[s, 1], cwv_ref[:, sl], cbv_ref[:, sl])
            act = (gt * _sigmoid(gt) * vl).astype(_BF16)
            acc_ref[...] += _dot(act, wd_ref[sl, :])

    if n_ctx_tiles == 0:
        step(False)
    else:
        pl.when(is_lat)(functools.partial(step, False))
        pl.when(jnp.logical_not(is_lat))(functools.partial(step, tm > nc))

    @pl.when(j == pl.num_programs(1) - 1)
    def _():
        o_ref[...] = x_ref[...] + gate_ref[...] * acc_ref[...]


def _ffn(x, g, shift, scale, gate, w_up, conv_w, conv_b, w_down, lay, n_rows, tm=512, tf=512, sub=MXU_TILE):
    M, D = x.shape
    F = w_down.shape[0]
    tm = _row_tile(lay, tm)
    assert tm % lay.nc == 0, "context tiles must hold whole segments"
    tiles_per_seg = lay.n // tm
    n_lat_tiles = lay.lat_rows // tm
    n_tiles = n_rows // tm
    tf = _pick(F, tf)
    nf = F // tf
    halo = BF16_ROWS
    per = tm // halo
    seg = _seg_index(lay, tm)
    vec = lambda: pl.BlockSpec((None, 1, D), lambda i, j: (seg(i), 0, 0))
    n_sub = tf // _pick(tf, sub)
    return pl.pallas_call(
        functools.partial(_ffn_kernel, n_lat_tiles, n_tiles - n_lat_tiles, tiles_per_seg, lay.nc, tm, halo, n_sub),
        grid=(n_tiles, nf),
        in_specs=[
            pl.BlockSpec((tm, D), lambda i, j: (i, 0)),
            pl.BlockSpec((halo, D), lambda i, j: (jnp.maximum(i * per - 1, 0), 0)),
            pl.BlockSpec((halo, D), lambda i, j: (jnp.minimum((i + 1) * per, M // halo - 1), 0)),
            pl.BlockSpec((1, D), lambda i, j: (0, 0)),
            vec(), vec(), vec(),
            pl.BlockSpec((D, tf), lambda i, j: (0, j)),
            pl.BlockSpec((D, tf), lambda i, j: (0, nf + j)),
            pl.BlockSpec((CONV_W, tf), lambda i, j: (0, j)),
            pl.BlockSpec((CONV_W, tf), lambda i, j: (0, nf + j)),
            pl.BlockSpec((1, tf), lambda i, j: (0, j)),
            pl.BlockSpec((1, tf), lambda i, j: (0, nf + j)),
            pl.BlockSpec((tf, D), lambda i, j: (j, 0)),
        ],
        out_specs=pl.BlockSpec((tm, D), lambda i, j: (i, 0)),
        out_shape=jax.ShapeDtypeStruct((n_rows, D), _F32),
        scratch_shapes=[pltpu.VMEM((tm + 2 * halo, D), _BF16), pltpu.VMEM((tm, D), _F32),
                        pltpu.VMEM((n_sub, 2, tm + 2 * halo, tf // n_sub), _F32)],
        compiler_params=_cparams("parallel", "arbitrary"),
        name="conv_gated_ffn",
    )(x, x, x, g, shift, scale, gate, w_up, w_up, conv_w, conv_w, conv_b, conv_b, w_down)


def _shift_mix_kernel(lay, tm, halo, x_ref, xp_ref, xn_ref, g_ref, sh_ref, sc_ref, mu_ref, *o_refs):
    i = pl.program_id(0)
    g, sh, sc = g_ref[...], sh_ref[...], sc_ref[...]
    h = _normmod(x_ref[...], g, sh, sc)
    hp = _normmod(xp_ref[...], g, sh, sc)
    hn = _normmod(xn_ref[...], g, sh, sc)
    ext = jnp.concatenate([hp, h, hn], axis=0)
    n_ext = tm + 2 * halo
    has_prev, has_next = _seg_masks(i, tm, h.shape, lay)
    prev = jnp.where(has_prev, pltpu.roll(ext, 1, 0)[halo:halo + tm], 0.0)
    nxt = jnp.where(has_next, pltpu.roll(ext, n_ext - 1, 0)[halo:halo + tm], 0.0)
    dx = 0.5 * (prev + nxt) - h
    mu = mu_ref[...]
    for m, o_ref in enumerate(o_refs):
        o_ref[...] = (h + dx * mu[m:m + 1]).astype(o_ref.dtype)


def _shift_mix(x, g, shift, scale, mu, lay, tm=256):
    M, D = x.shape
    tm = _row_tile(lay, tm)
    halo = 8
    per = tm // halo
    seg = _seg_index(lay, tm)
    vec = lambda: pl.BlockSpec((None, 1, D), lambda i: (seg(i), 0, 0))
    n_mix = mu.shape[0]
    return pl.pallas_call(
        functools.partial(_shift_mix_kernel, lay, tm, halo),
        grid=(M // tm,),
        in_specs=[
            pl.BlockSpec((tm, D), lambda i: (i, 0)),
            pl.BlockSpec((halo, D), lambda i: (jnp.maximum(i * per - 1, 0), 0)),
            pl.BlockSpec((halo, D), lambda i: (jnp.minimum((i + 1) * per, M // halo - 1), 0)),
            pl.BlockSpec((1, D), lambda i: (0, 0)),
            vec(), vec(),
            pl.BlockSpec((n_mix, D), lambda i: (0, 0)),
        ],
        out_specs=[pl.BlockSpec((tm, D), lambda i: (i, 0)) for _ in range(n_mix)],
        out_shape=[jax.ShapeDtypeStruct((M, D), _BF16) for _ in range(n_mix)],
        compiler_params=_cparams("parallel"),
        name="rwkv_shift_mix",
    )(x, x, x, g, shift, scale, mu)


def _vres_kernel(x_ref, w1_ref, w2_ref, b_ref, v_ref, vf_ref, o_ref):
    t = _dot(x_ref[...], w1_ref[...]).astype(_BF16)
    gate = _sigmoid(_dot(t, w2_ref[...]) + b_ref[...])
    v = v_ref[...]
    o_ref[...] = v + (vf_ref[...] - v) * gate


def _vres(x, v1, v2, v0, v, v_first, tm=512):
    M, D = x.shape
    w1 = _pad_cols(v1).astype(_BF16)
    w2 = _pad_rows(v2).astype(_BF16)
    R = w1.shape[1]
    tm = _pick(M, tm)
    return pl.pallas_call(
        _vres_kernel,
        grid=(M // tm,),
        in_specs=[
            pl.BlockSpec((tm, D), lambda i: (i, 0)),
            pl.BlockSpec((D, R), lambda i: (0, 0)),
            pl.BlockSpec((R, D), lambda i: (0, 0)),
            pl.BlockSpec((1, D), lambda i: (0, 0)),
            pl.BlockSpec((tm, D), lambda i: (i, 0)),
            pl.BlockSpec((tm, D), lambda i: (i, 0)),
        ],
        out_specs=pl.BlockSpec((tm, D), lambda i: (i, 0)),
        out_shape=jax.ShapeDtypeStruct((M, D), _F32),
        compiler_params=_cparams("parallel"),
        name="rwkv_value_residual",
    )(x, w1, w2, v0.reshape(1, D), v, v_first)


def _wkv_kernel(reverse, final, C, LW, r_ref, k_ref, v_ref, wp_ref, ap_ref, kk_ref, ka_ref, rk_ref, *refs):
    if final:
        y0_ref, b0_ref, g_ref, lng_ref, lnb_ref, o_ref, s_ref = refs
    else:
        y_ref, bon_ref, s_ref = refs
    j = pl.program_id(2)

    @pl.when(j == 0)
    def _():
        s_ref[...] = jnp.zeros_like(s_ref)

    hpg = WKV_GROUP // RWKV_HEAD
    groups = range(LW // WKV_GROUP)
    ng = len(groups)
    sls = [slice(g * WKV_GROUP, (g + 1) * WKV_GROUP) for g in groups]
    row = lax.broadcasted_iota(jnp.int32, (C, C), 0)
    col = lax.broadcasted_iota(jnp.int32, (C, C), 1)
    tri = jnp.where((col >= row) if reverse else (col <= row), 1.0, 0.0).astype(_BF16)
    rc = lax.broadcasted_iota(jnp.int32, (C, WKV_GROUP), 0)
    cc = lax.rem(lax.broadcasted_iota(jnp.int32, (C, WKV_GROUP), 1), C)
    strict = (cc > rc) if reverse else (cc < rc)
    incl = (cc >= rc) if reverse else (cc <= rc)
    eye_cat = jnp.where(cc == rc, 1.0, 0.0).astype(_F32)
    br = lax.broadcasted_iota(jnp.int32, (WKV_GROUP, WKV_GROUP), 0) // RWKV_HEAD
    bc = lax.broadcasted_iota(jnp.int32, (WKV_GROUP, WKV_GROUP), 1) // RWKV_HEAD
    same_head = br == bc
    head_ones = jnp.where(same_head, 1.0, 0.0).astype(_BF16)

    def bd(x):
        return jnp.where(same_head, jnp.concatenate([x] * hpg, axis=0), jnp.zeros((), x.dtype))

    cast = lambda x: x.astype(_BF16)

    n_chunks = r_ref.shape[0] // C
    units = [(c, g) for c in range(n_chunks) for g in groups]
    nu = len(units)
    tile = lambda ref, u: ref[u[0] * C:(u[0] + 1) * C, sls[u[1]]]
    lane = lambda ref, u: ref[:, sls[u[1]]]

    v = [tile(v_ref, u) for u in units]
    lw = []
    for u in units:
        wp = tile(wp_ref, u)
        sp = jnp.maximum(-wp, 0.0) + jnp.log(1.0 + jnp.exp(-jnp.abs(wp)))
        lw.append(-jnp.exp(-sp - 0.5))
    cl = [_ones_dot_left(tri, x, 3) for x in lw]
    total = [c[0:1] if reverse else c[C - 1:C] for c in cl]

    a = [_sigmoid(tile(ap_ref, u)) for u in units]
    kk = [tile(k_ref, u) * lane(kk_ref, u) for u in units]
    kd = [tile(k_ref, u) * (1.0 + (a[i] - 1.0) * lane(ka_ref, u)) for i, u in enumerate(units)]
    rkd = [tile(r_ref, u) * kd[i] * lane(rk_ref, u) for i, u in enumerate(units)]
    sums = _ones_dot_rows(head_ones, [x * x for x in kk] + rkd, 2)
    kk = [x / jnp.maximum(jnp.sqrt(n), 1e-12) for x, n in zip(kk, sums[:nu])]
    bonus = [s * x for s, x in zip(sums[nu:], v)]
    b = [kk[i] * a[i] for i in range(nu)]

    AR = [jnp.concatenate([cast(-kk[i] * jnp.exp(cl[i] - lw[i])), cast(tile(r_ref, u) * jnp.exp(cl[i]))], axis=0)
          for i, u in enumerate(units)]
    BKbd, BKp = [], []
    for i in range(nu):
        e_neg = jnp.exp(-cl[i])
        e_rem = jnp.exp(total[i] - cl[i])
        BKbd.append(jnp.concatenate([bd(cast(b[i] * e_neg)), bd(cast(kd[i] * e_neg))], axis=0))
        BKp.append(cast(jnp.concatenate([b[i] * e_rem, kd[i] * e_rem], axis=0)))
    vb = [cast(x) for x in v]
    Vbd = [bd(x) for x in vb]

    M = [_dot_nt(AR[i], BKbd[i]) for i in range(nu)]
    Lab = [jnp.where(strict, m[:C, :WKV_GROUP], 0.0) for m in M]
    LMk = [cast(jnp.concatenate([jnp.where(strict, m[:C, WKV_GROUP:], 0.0),
                                 jnp.where(incl, m[C:, WKV_GROUP:], 0.0)], axis=0)) for m in M]
    Mrb = [cast(jnp.where(incl, m[C:, :WKV_GROUP], 0.0)) for m in M]
    LMkV = [_dot(LMk[i], Vbd[i]) for i in range(nu)]

    Pw = [cast(x) for x in Lab]
    T = [eye_cat + x for x in Lab]
    Pbd = [bd(x) for x in Pw]
    for _ in range(int(math.log2(C)) - 1):
        Pw = [cast(_dot(x, xb)) for x, xb in zip(Pw, Pbd)]
        Pbd = [bd(x) for x in Pw]
        T = [t + _dot(cast(t), xb) for t, xb in zip(T, Pbd)]
    Tb = [cast(t) for t in T]

    S = [s_ref[g] for g in groups]
    y = [None] * nu
    for c in (reversed(range(n_chunks)) if reverse else range(n_chunks)):
        us = [c * ng + g for g in groups]
        ARS = [_dot_nt(AR[i], cast(S[g])) for g, i in enumerate(us)]
        X = [ARS[g][:C] + LMkV[i][:C] for g, i in enumerate(us)]
        Z = [cast(_dot(Tb[i], bd(cast(X[g])))) for g, i in enumerate(us)]
        for g, i in enumerate(us):
            y[i] = ARS[g][C:] + LMkV[i][C:] + _dot(Mrb[i], bd(Z[g]))
        dS = [_dot_tn(jnp.concatenate([Z[g], vb[i]], axis=0), BKp[i]) for g, i in enumerate(us)]
        S = [S[g] * jnp.exp(total[i]) + jnp.where(same_head, dS[g], 0.0) for g, i in enumerate(us)]
    s_ref[...] = jnp.stack(S, axis=0)

    def assemble(xs):
        return jnp.concatenate([jnp.concatenate(xs[c * ng:(c + 1) * ng], axis=1) for c in range(n_chunks)], axis=0)

    if final:
        inv = 1.0 / RWKV_HEAD
        wkv = [y[i] + tile(y0_ref, u) for i, u in enumerate(units)]
        cen = [x - s * inv for x, s in zip(wkv, _ones_dot_rows(head_ones, wkv, 3))]
        var = [s * inv for s in _ones_dot_rows(head_ones, [x * x for x in cen], 2)]
        outs = []
        for i, u in enumerate(units):
            normed = cen[i] * lax.rsqrt(var[i] + GN_EPS) * lane(lng_ref, u) + lane(lnb_ref, u)
            outs.append(((normed + bonus[i] + tile(b0_ref, u)) * tile(g_ref, u)).astype(o_ref.dtype))
        o_ref[...] = assemble(outs)
    else:
        y_ref[...] = assemble(y)
        bon_ref[...] = assemble(bonus)


def _wkv(r, k, v, wpre, apre, k_k, k_a, r_k, lay, reverse, final_args=None, lw=2048, chunks_per_step=2):
    M, D = r.shape
    C = WKV_CHUNK
    rows = chunks_per_step * C
    assert C == RWKV_HEAD and lay.nc % rows == 0 and lay.n % rows == 0 and D % WKV_GROUP == 0
    lw = _pick(D, lw)
    assert lw % WKV_GROUP == 0
    ncc, ncl = lay.nc // rows, lay.n // rows
    ctx_blk0 = lay.lat_rows // rows

    def row_blk(b, j):
        if reverse:
            return jnp.where(j < ncc, ctx_blk0 + b * ncc + (ncc - 1 - j), b * ncl + (ncl - 1 - (j - ncc)))
        return jnp.where(j < ncc, ctx_blk0 + b * ncc + j, b * ncl + (j - ncc))

    tile = lambda: pl.BlockSpec((rows, lw), lambda b, h, j: (row_blk(b, j), h))
    vec = lambda: pl.BlockSpec((1, lw), lambda b, h, j: (0, h))
    in_specs = [tile() for _ in range(5)] + [vec() for _ in range(3)]
    args = [r, k, v, wpre, apre, k_k.reshape(1, D), k_a.reshape(1, D), r_k.reshape(1, D)]
    final = final_args is not None
    if final:
        y0, b0, g, lng, lnb = final_args
        in_specs += [tile(), tile(), tile(), vec(), vec()]
        args += [y0, b0, g, lng.reshape(1, D), lnb.reshape(1, D)]
        out_specs = tile()
        out_shape = jax.ShapeDtypeStruct((M, D), _BF16)
    else:
        out_specs = [tile(), tile()]
        out_shape = [jax.ShapeDtypeStruct((M, D), _F32), jax.ShapeDtypeStruct((M, D), _F32)]
    return pl.pallas_call(
        functools.partial(_wkv_kernel, reverse, final, C, lw),
        grid=(lay.B, D // lw, ncc + ncl),
        in_specs=in_specs,
        out_specs=out_specs,
        out_shape=out_shape,
        scratch_shapes=[pltpu.VMEM((lw // WKV_GROUP, WKV_GROUP, WKV_GROUP), _F32)],
        compiler_params=_cparams("parallel", "parallel", "arbitrary"),
        name="wkv7_scan_rev" if reverse else "wkv7_scan_fwd",
    )(*args)


def _final_norm_kernel(x_ref, g_ref, o_ref):
    x = x_ref[...]
    o_ref[...] = x * lax.rsqrt(jnp.mean(x * x, axis=-1, keepdims=True) + NORM_EPS) * g_ref[...]


def _final_norm(x, g, n_rows, tm=512):
    D = x.shape[1]
    tm = _pick(n_rows, tm)
    return pl.pallas_call(
        _final_norm_kernel,
        grid=(n_rows // tm,),
        in_specs=[pl.BlockSpec((tm, D), lambda i: (i, 0)), pl.BlockSpec((1, D), lambda i: (0, 0))],
        out_specs=pl.BlockSpec((tm, D), lambda i: (i, 0)),
        out_shape=jax.ShapeDtypeStruct((n_rows, D), _F32),
        compiler_params=_cparams("parallel"),
        name="final_rms_norm",
    )(x, g.reshape(1, D))


def _rope_tables(lay):
    n = lay.n
    t = jnp.arange(n, dtype=jnp.int32)
    row = (t // GRID_W).astype(_F32)
    col = (t % GRID_W).astype(_F32)
    axis_dim = HEAD_DIM // 2
    inv_freq = ROPE_THETA ** (-jnp.arange(0, axis_dim, 2, dtype=_F32) / axis_dim)
    ang = jnp.concatenate([row[:, None] * inv_freq, col[:, None] * inv_freq], axis=-1)
    cos, sin = jnp.cos(ang), jnp.sin(ang)
    cos_l = jnp.concatenate([cos, cos], axis=-1)
    sin_l = jnp.concatenate([-sin, sin], axis=-1)
    n_ctx_rows = lay.B * lay.nc
    cos_f = jnp.concatenate([jnp.tile(cos_l, (lay.B, 1)), jnp.ones((n_ctx_rows, HEAD_DIM), _F32)], axis=0)
    sin_f = jnp.concatenate([jnp.tile(sin_l, (lay.B, 1)), jnp.zeros((n_ctx_rows, HEAD_DIM), _F32)], axis=0)
    return cos_f, sin_f


def _deinterleave_heads(w, n_heads):
    lead = w.shape[:-1]
    w = w.reshape(*lead, n_heads, HEAD_DIM // 2, 2)
    w = jnp.swapaxes(w, -1, -2)
    return w.reshape(*lead, n_heads * HEAD_DIM)


def _pad_cols(w, mult=LANES):
    pad = (-w.shape[-1]) % mult
    return jnp.pad(w, [(0, 0)] * (w.ndim - 1) + [(0, pad)]) if pad else w


def _pad_rows(w, mult=LANES):
    pad = (-w.shape[-2]) % mult
    return jnp.pad(w, [(0, 0)] * (w.ndim - 2) + [(0, pad), (0, 0)]) if pad else w


def kernel(x, c, ctx, c_ctx, w_mod, b_mod, norm1_g, norm2_g, attn_w_in, attn_w_out, q_norm_g, k_norm_g, rwkv_mu, rwkv_w_r, rwkv_w_k, rwkv_w_v, rwkv_w_o, rwkv_decay_w0, rwkv_decay_w1, rwkv_decay_w2, rwkv_iclr_a0, rwkv_iclr_a1, rwkv_iclr_a2, rwkv_gate_g1, rwkv_gate_g2, rwkv_k_k, rwkv_k_a, rwkv_r_k, rwkv_lnx_g, rwkv_lnx_b, rwkv_vres_v0, rwkv_vres_v1, rwkv_vres_v2, ffn_w_up, ffn_conv_w, ffn_conv_b, ffn_w_down, final_norm_g):
    B, n, D = x.shape
    nc = ctx.shape[1]
    depth = w_mod.shape[0]
    lay = Layout(B, n, nc, B * n, B * (n + nc))
    attn_w = N_ATTN_HEADS * HEAD_DIM
    kv_w = N_KV_HEADS * HEAD_DIM
    fw = D - attn_w
    bf = lambda w: w.astype(_BF16)

    xs = jnp.concatenate([x.reshape(B * n, D), ctx.reshape(B * nc, D)], axis=0)

    cvec = jnp.concatenate([c, c_ctx[None, :], jnp.zeros((7 - B % 8, D), _F32)], axis=0)
    mods = _modulation(cvec, w_mod, b_mod)[:, :B + 1]
    mods = mods.reshape(depth, B + 1, 6, 1, D)
    cos_f, sin_f = _rope_tables(lay)
    qg_scale = HEAD_DIM ** -0.5

    v_first = None
    for i in range(depth):
        last = i == depth - 1
        jl = i // 2
        m = [mods[i, :, t] for t in range(6)]
        g1 = norm1_g[i].reshape(1, D)
        if i % 2 == 0:
            w_in = attn_w_in[jl]
            w_in = bf(jnp.concatenate([
                _deinterleave_heads(w_in[:, :attn_w], N_ATTN_HEADS),
                _deinterleave_heads(w_in[:, attn_w:attn_w + kv_w], N_KV_HEADS),
                w_in[:, attn_w + kv_w:]], axis=1))
            qg = (_deinterleave_heads(q_norm_g[jl], 1) * qg_scale).reshape(1, HEAD_DIM)
            kg = _deinterleave_heads(k_norm_g[jl], 1).reshape(1, HEAD_DIM)
            u = _attn_in_proj(xs, g1, m[0], m[1], w_in, qg, kg, cos_f, sin_f, lay)
            o = _attention(u, lay)
            fm = jnp.concatenate([_fourier_mix(u, lay, fw), _fourier_mix(u, lay, fw, ctx=True)], axis=0)
            w_out = bf(attn_w_out[jl])
            xs = _out_proj([o, fm], [w_out[:attn_w], w_out[attn_w:]], xs, m[2], lay)
        else:
            x_r, x_w, x_k, x_v, x_a, x_g = _shift_mix(xs, g1, m[0], m[1], rwkv_mu[jl], lay)
            r = _linear(x_r, bf(rwkv_w_r[jl]), name="rwkv_r")
            k = _linear(x_k, bf(rwkv_w_k[jl]), name="rwkv_k")
            v = _linear(x_v, bf(rwkv_w_v[jl]), name="rwkv_v")
            if jl == 0:
                v_first = v
            else:
                v = _vres(x_v, rwkv_vres_v1[jl - 1], rwkv_vres_v2[jl - 1], rwkv_vres_v0[jl - 1], v, v_first)
            (g,) = _lora(x_g, [rwkv_gate_g1[jl]], [rwkv_gate_g2[jl]], act="sigmoid", name="rwkv_gate")
            wpre = _lora(x_w, list(rwkv_decay_w1[jl]), list(rwkv_decay_w2[jl]), list(rwkv_decay_w0[jl]),
                         act="tanh", name="rwkv_decay")
            apre = _lora(x_a, list(rwkv_iclr_a1[jl]), list(rwkv_iclr_a2[jl]), list(rwkv_iclr_a0[jl]),
                         name="rwkv_iclr")
            kk_, ka_, rk_ = rwkv_k_k[jl], rwkv_k_a[jl], rwkv_r_k[jl].reshape(D)
            y0, bon0 = _wkv(r, k, v, wpre[0], apre[0], kk_, ka_, rk_, lay, reverse=False)
            z = _wkv(r, k, v, wpre[1], apre[1], kk_, ka_, rk_, lay, reverse=True,
                     final_args=(y0, bon0, g, rwkv_lnx_g[jl], rwkv_lnx_b[jl]))
            xs = _out_proj([z], [bf(rwkv_w_o[jl])], xs, m[2], lay)
        n_rows = lay.lat_rows if last else lay.rows
        xs = _ffn(xs, norm2_g[i].reshape(1, D), m[3], m[4], m[5], bf(ffn_w_up[i]), ffn_conv_w[i],
                  ffn_conv_b[i].reshape(1, -1), bf(ffn_w_down[i]), lay, n_rows)
    out = _final_norm(xs, final_norm_g, lay.lat_rows)
    return out.reshape(B, n, D)
```

```python
import collections
import functools
import math

import jax
import jax.numpy as jnp
from jax import lax
from jax.experimental import pallas as pl
from jax.experimental.pallas import tpu as pltpu

HEAD_DIM = 128
N_ATTN_HEADS = 12
N_KV_HEADS = 4
GRID_W = 64
ROPE_THETA = 10000.0
N_FOURIER_GROUPS = 4
RWKV_HEAD = 64
CONV_W = 3
NORM_EPS = 1e-6
GN_EPS = 64e-5

LANES = 128
MXU_TILE = 256
BF16_ROWS = 16
VMEM_LIMIT = 56 * 1024 * 1024
WKV_CHUNK = 64
WKV_GROUP = MXU_TILE

_F32 = jnp.float32
_BF16 = jnp.bfloat16
_HI = lax.Precision.HIGHEST

Layout = collections.namedtuple("Layout", "B n nc lat_rows rows")


def _cparams(*sem):
    return pltpu.CompilerParams(dimension_semantics=sem, vmem_limit_bytes=VMEM_LIMIT)


def _pick(total, pref):
    t = min(pref, total)
    while total % t:
        t -= 1
    return t


def _row_tile(lay, pref):
    return _pick(lay.B * lay.nc, _pick(lay.n, pref))


def _seg_index(lay, tm):
    def f(i):
        r0 = i * tm
        return jnp.where(r0 < lay.lat_rows, r0 // lay.n, lay.B)
    return f


def _seg_masks(i, tm, shape, lay):
    t = lax.broadcasted_iota(jnp.int32, shape, 0)
    r0 = i * tm
    is_lat = r0 < lay.lat_rows
    if tm >= lay.nc:
        pc = lax.rem(t, lay.nc)
    else:
        pc = t + lax.rem(r0 - lay.lat_rows, lay.nc)
    pos = jnp.where(is_lat, t + lax.rem(r0, lay.n), pc)
    last = jnp.where(is_lat, lay.n - 1, lay.nc - 1)
    return pos != 0, pos != last


def _normmod(x, g, shift, scale):
    ms = jnp.mean(x * x, axis=-1, keepdims=True)
    return (x * lax.rsqrt(ms + NORM_EPS)) * g * (1.0 + scale) + shift


def _sigmoid(x):
    return 1.0 / (1.0 + jnp.exp(-x))


def _dot(a, b, prec=None):
    return jnp.dot(a, b, preferred_element_type=_F32, precision=prec)


def _dot_nt(a, b):
    return lax.dot_general(a, b, (((1,), (1,)), ((), ())), preferred_element_type=_F32)


def _dot_tn(a, b):
    return lax.dot_general(a, b, (((0,), (0,)), ((), ())), preferred_element_type=_F32)


def _bf16_pieces(x, parts):
    pieces = []
    rem = x
    for i in range(parts):
        piece = rem.astype(_BF16)
        pieces.append(piece)
        if i + 1 < parts:
            rem = rem - piece.astype(_F32)
    return pieces


def _ones_dot_left(ones, x, parts):
    acc = None
    for piece in _bf16_pieces(x, parts):
        d = _dot(ones, piece)
        acc = d if acc is None else acc + d
    return acc


def _ones_dot_rows(ones, xs, parts):
    rows = xs[0].shape[0]
    stacked = jnp.concatenate([p for x in xs for p in _bf16_pieces(x, parts)], axis=0)
    res = _dot(stacked, ones)
    outs = []
    for n in range(len(xs)):
        base = n * parts * rows
        acc = res[base:base + rows]
        for i in range(1, parts):
            acc = acc + res[base + i * rows:base + (i + 1) * rows]
        outs.append(acc)
    return outs


def _mod_kernel(c_ref, w_ref, b_ref, o_ref):
    c = c_ref[...]
    s = c * _sigmoid(c)
    o_ref[...] = _dot(s, w_ref[...], _HI) + b_ref[...]


def _modulation(cvec, w_mod, b_mod):
    depth, D, N = w_mod.shape
    rows = cvec.shape[0]
    tn = _pick(N, 512)
    return pl.pallas_call(
        _mod_kernel,
        grid=(depth, N // tn),
        in_specs=[
            pl.BlockSpec((rows, D), lambda l, j: (0, 0)),
            pl.BlockSpec((None, D, tn), lambda l, j: (l, 0, j)),
            pl.BlockSpec((None, 1, tn), lambda l, j: (l, 0, j)),
        ],
        out_specs=pl.BlockSpec((None, rows, tn), lambda l, j: (l, 0, j)),
        out_shape=jax.ShapeDtypeStruct((depth, rows, N), _F32),
        compiler_params=_cparams("parallel", "parallel"),
        name="adaln_modulation",
    )(cvec, w_mod, b_mod.reshape(depth, 1, N))


def _linear_kernel(x_ref, w_ref, o_ref):
    o_ref[...] = _dot(x_ref[...], w_ref[...]).astype(o_ref.dtype)


def _linear(x, w, out_dtype=_F32, tm=512, tn=2048, name="linear"):
    M, K = x.shape
    N = w.shape[1]
    tm, tn = _pick(M, tm), _pick(N, tn)
    return pl.pallas_call(
        _linear_kernel,
        grid=(M // tm, N // tn),
        in_specs=[pl.BlockSpec((tm, K), lambda i, j: (i, 0)), pl.BlockSpec((K, tn), lambda i, j: (0, j))],
        out_specs=pl.BlockSpec((tm, tn), lambda i, j: (i, j)),
        out_shape=jax.ShapeDtypeStruct((M, N), out_dtype),
        compiler_params=_cparams("parallel", "arbitrary"),
        name=name,
    )(x, w)


def _lora_kernel(act, ranks, has_bias, x_ref, w1_ref, *refs):
    n_out = len(ranks)
    w2_refs = refs[:n_out]
    b_refs = refs[n_out:2 * n_out] if has_bias else ()
    o_refs = refs[-n_out:]
    t = _dot(x_ref[...], w1_ref[...])
    if act == "tanh":
        t = jnp.tanh(t)
    elif act == "sigmoid":
        t = _sigmoid(t)
    t = t.astype(_BF16)
    off = 0
    for d in range(n_out):
        y = _dot(t[:, off:off + ranks[d]], w2_refs[d][...])
        if has_bias:
            y = y + b_refs[d][...]
        o_refs[d][...] = y
        off += ranks[d]


def _lora(x, w1s, w2s, biases=None, act=None, tm=512, name="lora"):
    M, D = x.shape
    N = w2s[0].shape[1]
    tm = _pick(M, tm)
    w1 = jnp.concatenate([_pad_cols(w) for w in w1s], axis=1).astype(_BF16)
    w2p = [_pad_rows(w).astype(_BF16) for w in w2s]
    ranks = tuple(w.shape[0] for w in w2p)
    in_specs = [pl.BlockSpec((tm, D), lambda i: (i, 0)), pl.BlockSpec(w1.shape, lambda i: (0, 0))]
    in_specs += [pl.BlockSpec(w.shape, lambda i: (0, 0)) for w in w2p]
    args = [x, w1, *w2p]
    if biases is not None:
        in_specs += [pl.BlockSpec((1, N), lambda i: (0, 0)) for _ in biases]
        args += [b.reshape(1, N) for b in biases]
    return pl.pallas_call(
        functools.partial(_lora_kernel, act, ranks, biases is not None),
        grid=(M // tm,),
        in_specs=in_specs,
        out_specs=[pl.BlockSpec((tm, N), lambda i: (i, 0)) for _ in w2p],
        out_shape=[jax.ShapeDtypeStruct((M, N), _F32) for _ in w2p],
        compiler_params=_cparams("parallel"),
        name=name,
    )(*args)


def _attn_in_kernel(sw, n_q, n_qk, x_ref, g_ref, sh_ref, sc_ref, w_ref, qg_ref, kg_ref, cos_ref, sin_ref, o_ref):
    h = _normmod(x_ref[...], g_ref[...], sh_ref[...], sc_ref[...]).astype(_BF16)
    cos, sin = cos_ref[...], sin_ref[...]
    n_blocks = w_ref.shape[1] // sw

    def finish(s, acc):
        cols = slice(s * sw, (s + 1) * sw)
        if s >= n_qk:
            o_ref[:, cols] = acc.astype(o_ref.dtype)
            return
        gain = qg_ref[...] if s < n_q else kg_ref[...]
        for hd in range(sw // HEAD_DIM):
            a = acc[:, hd * HEAD_DIM:(hd + 1) * HEAD_DIM]
            y = a * lax.rsqrt(jnp.mean(a * a, axis=-1, keepdims=True) + NORM_EPS) * gain
            y = y * cos + pltpu.roll(y, HEAD_DIM // 2, 1) * sin
            o_ref[:, s * sw + hd * HEAD_DIM:s * sw + (hd + 1) * HEAD_DIM] = y.astype(o_ref.dtype)

    acc = _dot(h, w_ref[:, 0:sw])
    for s in range(n_blocks):
        nxt = _dot(h, w_ref[:, (s + 1) * sw:(s + 2) * sw]) if s + 1 < n_blocks else None
        finish(s, acc)
        acc = nxt


def _attn_in_proj(x, g, shift, scale, w, qg, kg, cos, sin, lay, tm=512):
    M, D = x.shape
    N = w.shape[1]
    tm = _row_tile(lay, tm)
    attn_w = N_ATTN_HEADS * HEAD_DIM
    kv_w = N_KV_HEADS * HEAD_DIM
    sw = math.gcd(math.gcd(attn_w, kv_w), MXU_TILE)
    seg = _seg_index(lay, tm)
    vec = lambda: pl.BlockSpec((None, 1, D), lambda i: (seg(i), 0, 0))
    return pl.pallas_call(
        functools.partial(_attn_in_kernel, sw, attn_w // sw, (attn_w + kv_w) // sw),
        grid=(M // tm,),
        in_specs=[
            pl.BlockSpec((tm, D), lambda i: (i, 0)),
            pl.BlockSpec((1, D), lambda i: (0, 0)),
            vec(), vec(),
            pl.BlockSpec((D, N), lambda i: (0, 0)),
            pl.BlockSpec((1, HEAD_DIM), lambda i: (0, 0)),
            pl.BlockSpec((1, HEAD_DIM), lambda i: (0, 0)),
            pl.BlockSpec((tm, HEAD_DIM), lambda i: (i, 0)),
            pl.BlockSpec((tm, HEAD_DIM), lambda i: (i, 0)),
        ],
        out_specs=pl.BlockSpec((tm, N), lambda i: (i, 0)),
        out_shape=jax.ShapeDtypeStruct((M, N), _BF16),
        compiler_params=_cparams("parallel"),
        name="attn_in_proj",
    )(x, g, shift, scale, w, qg, kg, cos, sin)


def _attn_kernel(n_lat_tiles, g_per_kv, kb, q_ref, kc_ref, vc_ref, kl_ref, vl_ref, o_ref):
    qi = pl.program_id(2)
    tq = q_ref.shape[0]
    q = jnp.concatenate([q_ref[:, g * HEAD_DIM:(g + 1) * HEAD_DIM] for g in range(g_per_kv)], axis=0)

    def finish(acc, l):
        o = acc / l
        for g in range(g_per_kv):
            o_ref[:, g * HEAD_DIM:(g + 1) * HEAD_DIM] = o[g * tq:(g + 1) * tq].astype(o_ref.dtype)

    s_c = _dot_nt(q, kc_ref[...])
    m = jnp.max(s_c, axis=-1, keepdims=True)
    p = jnp.exp(s_c - m)
    l = jnp.sum(p, axis=-1, keepdims=True)
    acc = _dot(p.astype(_BF16), vc_ref[...])

    @pl.when(qi < n_lat_tiles)
    def _():
        n_blocks = kl_ref.shape[0] // kb
        mj, lj, accj = m, l, acc
        s_next = _dot_nt(q, kl_ref[0:kb, :])
        for j in range(n_blocks):
            s = s_next
            if j + 1 < n_blocks:
                s_next = _dot_nt(q, kl_ref[(j + 1) * kb:(j + 2) * kb, :])
            m_new = jnp.maximum(mj, jnp.max(s, axis=-1, keepdims=True))
            alpha = jnp.exp(mj - m_new)
            pj = jnp.exp(s - m_new)
            lj = lj * alpha + jnp.sum(pj, axis=-1, keepdims=True)
            accj = accj * alpha + _dot(pj.astype(_BF16), vl_ref[j * kb:(j + 1) * kb, :])
            mj = m_new
        finish(accj, lj)

    @pl.when(qi >= n_lat_tiles)
    def _():
        finish(acc, l)


def _attention(u, lay, tq=256, kb=512):
    M = u.shape[0]
    g_per_kv = N_ATTN_HEADS // N_KV_HEADS
    qw = g_per_kv * HEAD_DIM
    attn_w = N_ATTN_HEADS * HEAD_DIM
    tq = _pick(lay.nc, tq)
    nlt, nct = lay.n // tq, lay.nc // tq
    k_blk = attn_w // HEAD_DIM
    v_blk = k_blk + N_KV_HEADS
    ctx_blk0 = lay.lat_rows // lay.nc

    def q_map(b, kv, qi):
        row = jnp.where(qi < nlt, b * nlt + qi, lay.lat_rows // tq + b * nct + (qi - nlt))
        return (row, kv)

    return pl.pallas_call(
        functools.partial(_attn_kernel, nlt, g_per_kv, _pick(lay.n, kb)),
        grid=(lay.B, N_KV_HEADS, nlt + nct),
        in_specs=[
            pl.BlockSpec((tq, qw), q_map),
            pl.BlockSpec((lay.nc, HEAD_DIM), lambda b, kv, qi: (ctx_blk0 + b, k_blk + kv)),
            pl.BlockSpec((lay.nc, HEAD_DIM), lambda b, kv, qi: (ctx_blk0 + b, v_blk + kv)),
            pl.BlockSpec((lay.n, HEAD_DIM), lambda b, kv, qi: (b, k_blk + kv)),
            pl.BlockSpec((lay.n, HEAD_DIM), lambda b, kv, qi: (b, v_blk + kv)),
        ],
        out_specs=pl.BlockSpec((tq, qw), q_map),
        out_shape=jax.ShapeDtypeStruct((M, attn_w), _BF16),
        compiler_params=_cparams("parallel", "parallel", "arbitrary"),
        name="gqa_attention",
    )(u, u, u, u, u)


def _dft_kernel(f_ref, cc_ref, sc_ref, cn_ref, sn_ref, o_ref, a_ref, b_ref):
    @pl.when(pl.program_id(1) == 0)
    def _():
        f = f_ref[...]
        a_ref[...] = _dot(f, cc_ref[...]).astype(_BF16)
        b_ref[...] = _dot(f, sc_ref[...]).astype(_BF16)

    o_ref[...] = (_dot(cn_ref[...], a_ref[...]) - _dot(sn_ref[...], b_ref[...])).astype(o_ref.dtype)


def _cos_sin_matrix(m, split=64):
    j = jnp.arange(m, dtype=jnp.int32)[:, None]
    w = 2.0 * math.pi / m

    def cs(k):
        ang = w * ((j * k[None, :]) % m).astype(_F32)
        return jnp.cos(ang), jnp.sin(ang)

    if m % split or m <= 8 * split:
        return cs(jnp.arange(m, dtype=jnp.int32))
    ca, sa = cs(split * jnp.arange(m // split, dtype=jnp.int32))
    cb, sb = cs(jnp.arange(split, dtype=jnp.int32))
    cos = ca[:, :, None] * cb[:, None, :] - sa[:, :, None] * sb[:, None, :]
    sin = sa[:, :, None] * cb[:, None, :] + ca[:, :, None] * sb[:, None, :]
    return cos.reshape(m, m), sin.reshape(m, m)


def _dft_tables(n, group, n_groups):
    cn, sn = _cos_sin_matrix(n)
    scale = 1.0 / math.sqrt(n * group)
    cg, sg = _cos_sin_matrix(group)
    eye = jnp.eye(n_groups, dtype=_F32)
    return ((cn * scale).astype(_BF16), (sn * scale).astype(_BF16),
            jnp.kron(eye, cg).astype(_BF16), jnp.kron(eye, sg).astype(_BF16))


def _fourier_mix(u, lay, fw, ctx=False, tm=512):
    M, N = u.shape
    n = lay.nc if ctx else lay.n
    tm = _pick(n, tm)
    nt = n // tm
    cn, sn, cc, sc = _dft_tables(n, fw // N_FOURIER_GROUPS, N_FOURIER_GROUPS)
    f_blk = N // fw - 1
    seg0 = lay.lat_rows // n if ctx else 0
    return pl.pallas_call(
        _dft_kernel,
        grid=(lay.B, nt),
        in_specs=[
            pl.BlockSpec((n, fw), lambda b, i: (seg0 + b, f_blk)),
            pl.BlockSpec((fw, fw), lambda b, i: (0, 0)),
            pl.BlockSpec((fw, fw), lambda b, i: (0, 0)),
            pl.BlockSpec((tm, n), lambda b, i: (i, 0)),
            pl.BlockSpec((tm, n), lambda b, i: (i, 0)),
        ],
        out_specs=pl.BlockSpec((tm, fw), lambda b, i: (b * nt + i, 0)),
        out_shape=jax.ShapeDtypeStruct((lay.B * n, fw), _BF16),
        scratch_shapes=[pltpu.VMEM((n, fw), _BF16), pltpu.VMEM((n, fw), _BF16)],
        compiler_params=_cparams("parallel", "arbitrary"),
        name="fourier_mix_ctx" if ctx else "fourier_mix_lat",
    )(u, cc, sc, cn, sn)


def _out_proj_kernel(n_lhs, *refs):
    lhs = refs[:n_lhs]
    ws = refs[n_lhs:2 * n_lhs]
    x_ref, gate_ref, o_ref = refs[2 * n_lhs], refs[2 * n_lhs + 1], refs[2 * n_lhs + 2]
    acc = _dot(lhs[0][...], ws[0][...])
    for a, w in zip(lhs[1:], ws[1:]):
        acc = acc + _dot(a[...], w[...])
    o_ref[...] = x_ref[...] + gate_ref[...] * acc


def _out_proj(lhs, ws, x, gate, lay, tm=512):
    M, D = x.shape
    tm = _row_tile(lay, tm)
    seg = _seg_index(lay, tm)
    in_specs = [pl.BlockSpec((tm, a.shape[1]), lambda i: (i, 0)) for a in lhs]
    in_specs += [pl.BlockSpec(w.shape, lambda i: (0, 0)) for w in ws]
    in_specs += [pl.BlockSpec((tm, D), lambda i: (i, 0)), pl.BlockSpec((None, 1, D), lambda i: (seg(i), 0, 0))]
    return pl.pallas_call(
        functools.partial(_out_proj_kernel, len(lhs)),
        grid=(M // tm,),
        in_specs=in_specs,
        out_specs=pl.BlockSpec((tm, D), lambda i: (i, 0)),
        out_shape=jax.ShapeDtypeStruct((M, D), _F32),
        compiler_params=_cparams("parallel"),
        name="out_proj_residual",
    )(*lhs, *ws, x, gate)


def _ffn_kernel(n_lat_tiles, n_ctx_tiles, tiles_per_seg, nc, tm, halo, n_sub, final_norm, x_ref, xp_ref, xn_ref,
                g_ref, sh_ref, sc_ref, gate_ref, wg_ref, wv_ref, cwg_ref, cwv_ref, cbg_ref, cbv_ref, wd_ref, fg_ref,
                o_ref, h_ref, acc_ref, u_ref):
    i, j = pl.program_id(0), pl.program_id(1)
    ext = tm + 2 * halo
    is_lat = i < n_lat_tiles

    @pl.when(j == 0)
    def _():
        g, sh, sc = g_ref[...], sh_ref[...], sc_ref[...]
        pos = lax.rem(i, tiles_per_seg)
        keep_prev = jnp.logical_and(is_lat, pos != 0)
        keep_next = jnp.logical_and(is_lat, pos != tiles_per_seg - 1)
        hp = _normmod(xp_ref[...], g, sh, sc)
        hn = _normmod(xn_ref[...], g, sh, sc)
        h_ref[0:halo, :] = jnp.where(keep_prev, hp, 0.0).astype(_BF16)
        h_ref[halo:halo + tm, :] = _normmod(x_ref[...], g, sh, sc).astype(_BF16)
        h_ref[halo + tm:ext, :] = jnp.where(keep_next, hn, 0.0).astype(_BF16)
        acc_ref[...] = jnp.zeros_like(acc_ref)

    def step(inner_boundaries):
        h = h_ref[...]
        tf = wg_ref.shape[1]
        ts = tf // n_sub
        subs = [slice(s * ts, (s + 1) * ts) for s in range(n_sub)]

        def conv(u_ref, cw, cb):
            up = u_ref[halo - 1:halo - 1 + tm, :]
            un = u_ref[halo + 1:halo + 1 + tm, :]
            if inner_boundaries:
                t = lax.rem(lax.broadcasted_iota(jnp.int32, up.shape, 0), nc)
                up = jnp.where(t != 0, up, 0.0)
                un = jnp.where(t != nc - 1, un, 0.0)
            return up * cw[0:1] + u_ref[halo:halo + tm, :] * cw[1:2] + un * cw[2:3] + cb

        def up_proj(s):
            u_ref[s, 0] = _dot(h, wg_ref[:, subs[s]])
            u_ref[s, 1] = _dot(h, wv_ref[:, subs[s]])

        up_proj(0)
        for s, sl in enumerate(subs):
            if s + 1 < n_sub:
                up_proj(s + 1)
            gt = conv(u_ref.at[s, 0], cwg_ref[:, sl], cbg_ref[:, sl])
            vl = conv(u_ref.at[s, 1], cwv_ref[:, sl], cbv_ref[:, sl])
            act = (gt * _sigmoid(gt) * vl).astype(_BF16)
            acc_ref[...] += _dot(act, wd_ref[sl, :])

    if n_ctx_tiles == 0:
        step(False)
    else:
        pl.when(is_lat)(functools.partial(step, False))
        pl.when(jnp.logical_not(is_lat))(functools.partial(step, tm > nc))

    @pl.when(j == pl.num_programs(1) - 1)
    def _():
        y = x_ref[...] + gate_ref[...] * acc_ref[...]
        if final_norm:
            y = y * lax.rsqrt(jnp.mean(y * y, axis=-1, keepdims=True) + NORM_EPS) * fg_ref[...]
        o_ref[...] = y


def _ffn(x, g, shift, scale, gate, w_up, conv_w, conv_b, w_down, final_g, lay, n_rows, final_norm, tm=512, tf=512,
         sub=MXU_TILE):
    M, D = x.shape
    F = w_down.shape[0]
    tm = _row_tile(lay, tm)
    assert tm % lay.nc == 0, "context tiles must hold whole segments"
    tiles_per_seg = lay.n // tm
    n_lat_tiles = lay.lat_rows // tm
    n_tiles = n_rows // tm
    tf = _pick(F, tf)
    nf = F // tf
    halo = BF16_ROWS
    per = tm // halo
    seg = _seg_index(lay, tm)
    vec = lambda: pl.BlockSpec((None, 1, D), lambda i, j: (seg(i), 0, 0))
    n_sub = tf // _pick(tf, sub)
    return pl.pallas_call(
        functools.partial(_ffn_kernel, n_lat_tiles, n_tiles - n_lat_tiles, tiles_per_seg, lay.nc, tm, halo, n_sub,
                          final_norm),
        grid=(n_tiles, nf),
        in_specs=[
            pl.BlockSpec((tm, D), lambda i, j: (i, 0)),
            pl.BlockSpec((halo, D), lambda i, j: (jnp.maximum(i * per - 1, 0), 0)),
            pl.BlockSpec((halo, D), lambda i, j: (jnp.minimum((i + 1) * per, M // halo - 1), 0)),
            pl.BlockSpec((1, D), lambda i, j: (0, 0)),
            vec(), vec(), vec(),
            pl.BlockSpec((D, tf), lambda i, j: (0, j)),
            pl.BlockSpec((D, tf), lambda i, j: (0, nf + j)),
            pl.BlockSpec((CONV_W, tf), lambda i, j: (0, j)),
            pl.BlockSpec((CONV_W, tf), lambda i, j: (0, nf + j)),
            pl.BlockSpec((1, tf), lambda i, j: (0, j)),
            pl.BlockSpec((1, tf), lambda i, j: (0, nf + j)),
            pl.BlockSpec((tf, D), lambda i, j: (j, 0)),
            pl.BlockSpec((1, D), lambda i, j: (0, 0)),
        ],
        out_specs=pl.BlockSpec((tm, D), lambda i, j: (i, 0)),
        out_shape=jax.ShapeDtypeStruct((n_rows, D), _F32),
        scratch_shapes=[pltpu.VMEM((tm + 2 * halo, D), _BF16), pltpu.VMEM((tm, D), _F32),
                        pltpu.VMEM((n_sub, 2, tm + 2 * halo, tf // n_sub), _F32)],
        compiler_params=_cparams("parallel", "arbitrary"),
        name="conv_gated_ffn",
    )(x, x, x, g, shift, scale, gate, w_up, w_up, conv_w, conv_w, conv_b, conv_b, w_down, final_g.reshape(1, D))


def _shift_mix_kernel(lay, tm, halo, x_ref, xp_ref, xn_ref, g_ref, sh_ref, sc_ref, mu_ref, *o_refs):
    i = pl.program_id(0)
    g, sh, sc = g_ref[...], sh_ref[...], sc_ref[...]
    h = _normmod(x_ref[...], g, sh, sc)
    hp = _normmod(xp_ref[...], g, sh, sc)
    hn = _normmod(xn_ref[...], g, sh, sc)
    ext = jnp.concatenate([hp, h, hn], axis=0)
    n_ext = tm + 2 * halo
    has_prev, has_next = _seg_masks(i, tm, h.shape, lay)
    prev = jnp.where(has_prev, pltpu.roll(ext, 1, 0)[halo:halo + tm], 0.0)
    nxt = jnp.where(has_next, pltpu.roll(ext, n_ext - 1, 0)[halo:halo + tm], 0.0)
    dx = 0.5 * (prev + nxt) - h
    mu = mu_ref[...]
    for m, o_ref in enumerate(o_refs):
        o_ref[...] = (h + dx * mu[m:m + 1]).astype(o_ref.dtype)


def _shift_mix(x, g, shift, scale, mu, lay, tm=256):
    M, D = x.shape
    tm = _row_tile(lay, tm)
    halo = 8
    per = tm // halo
    seg = _seg_index(lay, tm)
    vec = lambda: pl.BlockSpec((None, 1, D), lambda i: (seg(i), 0, 0))
    n_mix = mu.shape[0]
    return pl.pallas_call(
        functools.partial(_shift_mix_kernel, lay, tm, halo),
        grid=(M // tm,),
        in_specs=[
            pl.BlockSpec((tm, D), lambda i: (i, 0)),
            pl.BlockSpec((halo, D), lambda i: (jnp.maximum(i * per - 1, 0), 0)),
            pl.BlockSpec((halo, D), lambda i: (jnp.minimum((i + 1) * per, M // halo - 1), 0)),
            pl.BlockSpec((1, D), lambda i: (0, 0)),
            vec(), vec(),
            pl.BlockSpec((n_mix, D), lambda i: (0, 0)),
        ],
        out_specs=[pl.BlockSpec((tm, D), lambda i: (i, 0)) for _ in range(n_mix)],
        out_shape=[jax.ShapeDtypeStruct((M, D), _BF16) for _ in range(n_mix)],
        compiler_params=_cparams("parallel"),
        name="rwkv_shift_mix",
    )(x, x, x, g, shift, scale, mu)


def _vres_kernel(x_ref, w1_ref, w2_ref, b_ref, v_ref, vf_ref, o_ref):
    t = _dot(x_ref[...], w1_ref[...]).astype(_BF16)
    gate = _sigmoid(_dot(t, w2_ref[...]) + b_ref[...])
    v = v_ref[...]
    o_ref[...] = v + (vf_ref[...] - v) * gate


def _vres(x, v1, v2, v0, v, v_first, tm=512):
    M, D = x.shape
    w1 = _pad_cols(v1).astype(_BF16)
    w2 = _pad_rows(v2).astype(_BF16)
    R = w1.shape[1]
    tm = _pick(M, tm)
    return pl.pallas_call(
        _vres_kernel,
        grid=(M // tm,),
        in_specs=[
            pl.BlockSpec((tm, D), lambda i: (i, 0)),
            pl.BlockSpec((D, R), lambda i: (0, 0)),
            pl.BlockSpec((R, D), lambda i: (0, 0)),
            pl.BlockSpec((1, D), lambda i: (0, 0)),
            pl.BlockSpec((tm, D), lambda i: (i, 0)),
            pl.BlockSpec((tm, D), lambda i: (i, 0)),
        ],
        out_specs=pl.BlockSpec((tm, D), lambda i: (i, 0)),
        out_shape=jax.ShapeDtypeStruct((M, D), _F32),
        compiler_params=_cparams("parallel"),
        name="rwkv_value_residual",
    )(x, w1, w2, v0.reshape(1, D), v, v_first)


def _wkv_kernel(reverse, final, C, LW, r_ref, k_ref, v_ref, wp_ref, ap_ref, kk_ref, ka_ref, rk_ref, *refs):
    if final:
        y0_ref, b0_ref, g_ref, lng_ref, lnb_ref, o_ref, s_ref = refs
    else:
        y_ref, bon_ref, s_ref = refs
    j = pl.program_id(2)

    @pl.when(j == 0)
    def _():
        s_ref[...] = jnp.zeros_like(s_ref)

    hpg = WKV_GROUP // RWKV_HEAD
    groups = range(LW // WKV_GROUP)
    ng = len(groups)
    sls = [slice(g * WKV_GROUP, (g + 1) * WKV_GROUP) for g in groups]
    row = lax.broadcasted_iota(jnp.int32, (C, C), 0)
    col = lax.broadcasted_iota(jnp.int32, (C, C), 1)
    tri = jnp.where((col >= row) if reverse else (col <= row), 1.0, 0.0).astype(_BF16)
    rc = lax.broadcasted_iota(jnp.int32, (C, WKV_GROUP), 0)
    cc = lax.rem(lax.broadcasted_iota(jnp.int32, (C, WKV_GROUP), 1), C)
    strict = (cc > rc) if reverse else (cc < rc)
    incl = (cc >= rc) if reverse else (cc <= rc)
    eye_cat = jnp.where(cc == rc, 1.0, 0.0).astype(_F32)
    br = lax.broadcasted_iota(jnp.int32, (WKV_GROUP, WKV_GROUP), 0) // RWKV_HEAD
    bc = lax.broadcasted_iota(jnp.int32, (WKV_GROUP, WKV_GROUP), 1) // RWKV_HEAD
    same_head = br == bc
    head_ones = jnp.where(same_head, 1.0, 0.0).astype(_BF16)

    def bd(x):
        return jnp.where(same_head, jnp.concatenate([x] * hpg, axis=0), jnp.zeros((), x.dtype))

    cast = lambda x: x.astype(_BF16)

    n_chunks = r_ref.shape[0] // C
    units = [(c, g) for c in range(n_chunks) for g in groups]
    nu = len(units)
    tile = lambda ref, u: ref[u[0] * C:(u[0] + 1) * C, sls[u[1]]]
    lane = lambda ref, u: ref[:, sls[u[1]]]

    v = [tile(v_ref, u) for u in units]
    lw = []
    for u in units:
        wp = tile(wp_ref, u)
        sp = jnp.maximum(-wp, 0.0) + jnp.log(1.0 + jnp.exp(-jnp.abs(wp)))
        lw.append(-jnp.exp(-sp - 0.5))
    cl = [_ones_dot_left(tri, x, 3) for x in lw]
    total = [c[0:1] if reverse else c[C - 1:C] for c in cl]

    a = [_sigmoid(tile(ap_ref, u)) for u in units]
    kk = [tile(k_ref, u) * lane(kk_ref, u) for u in units]
    kd = [tile(k_ref, u) * (1.0 + (a[i] - 1.0) * lane(ka_ref, u)) for i, u in enumerate(units)]
    rkd = [tile(r_ref, u) * kd[i] * lane(rk_ref, u) for i, u in enumerate(units)]
    sums = _ones_dot_rows(head_ones, [x * x for x in kk] + rkd, 2)
    kk = [x / jnp.maximum(jnp.sqrt(n), 1e-12) for x, n in zip(kk, sums[:nu])]
    bonus = [s * x for s, x in zip(sums[nu:], v)]
    b = [kk[i] * a[i] for i in range(nu)]

    AR = [jnp.concatenate([cast(-kk[i] * jnp.exp(cl[i] - lw[i])), cast(tile(r_ref, u) * jnp.exp(cl[i]))], axis=0)
          for i, u in enumerate(units)]
    BKbd, BKp = [], []
    for i in range(nu):
        e_neg = jnp.exp(-cl[i])
        e_rem = jnp.exp(total[i] - cl[i])
        BKbd.append(jnp.concatenate([bd(cast(b[i] * e_neg)), bd(cast(kd[i] * e_neg))], axis=0))
        BKp.append(cast(jnp.concatenate([b[i] * e_rem, kd[i] * e_rem], axis=0)))
    vb = [cast(x) for x in v]
    Vbd = [bd(x) for x in vb]

    M = [_dot_nt(AR[i], BKbd[i]) for i in range(nu)]
    Lab = [jnp.where(strict, m[:C, :WKV_GROUP], 0.0) for m in M]
    LMk = [cast(jnp.concatenate([jnp.where(strict, m[:C, WKV_GROUP:], 0.0),
                                 jnp.where(incl, m[C:, WKV_GROUP:], 0.0)], axis=0)) for m in M]
    Mrb = [cast(jnp.where(incl, m[C:, :WKV_GROUP], 0.0)) for m in M]
    LMkV = [_dot(LMk[i], Vbd[i]) for i in range(nu)]

    Pw = [cast(x) for x in Lab]
    T = [eye_cat + x for x in Lab]
    Pw = [cast(_dot(x, bd(x))) for x in Pw]
    for _ in range(int(math.log2(C)) - 2):
        both = [_dot(jnp.concatenate([cast(t), x], axis=0), bd(x)) for t, x in zip(T, Pw)]
        T = [t + bt[:C] for t, bt in zip(T, both)]
        Pw = [cast(bt[C:]) for bt in both]
    Tb = [cast(t + _dot(cast(t), bd(x))) for t, x in zip(T, Pw)]

    S = [s_ref[g] for g in groups]
    y = [None] * nu
    for c in (reversed(range(n_chunks)) if reverse else range(n_chunks)):
        us = [c * ng + g for g in groups]
        ARS = [_dot_nt(AR[i], cast(S[g])) for g, i in enumerate(us)]
        X = [ARS[g][:C] + LMkV[i][:C] for g, i in enumerate(us)]
        Z = [cast(_dot(Tb[i], bd(cast(X[g])))) for g, i in enumerate(us)]
        for g, i in enumerate(us):
            y[i] = ARS[g][C:] + LMkV[i][C:] + _dot(Mrb[i], bd(Z[g]))
        dS = [_dot_tn(jnp.concatenate([Z[g], vb[i]], axis=0), BKp[i]) for g, i in enumerate(us)]
        S = [S[g] * jnp.exp(total[i]) + jnp.where(same_head, dS[g], 0.0) for g, i in enumerate(us)]
    s_ref[...] = jnp.stack(S, axis=0)

    def assemble(xs):
        return jnp.concatenate([jnp.concatenate(xs[c * ng:(c + 1) * ng], axis=1) for c in range(n_chunks)], axis=0)

    if final:
        inv = 1.0 / RWKV_HEAD
        wkv = [y[i] + tile(y0_ref, u) for i, u in enumerate(units)]
        cen = [x - s * inv for x, s in zip(wkv, _ones_dot_rows(head_ones, wkv, 3))]
        var = [s * inv for s in _ones_dot_rows(head_ones, [x * x for x in cen], 2)]
        outs = []
        for i, u in enumerate(units):
            normed = cen[i] * lax.rsqrt(var[i] + GN_EPS) * lane(lng_ref, u) + lane(lnb_ref, u)
            outs.append(((normed + bonus[i] + tile(b0_ref, u)) * tile(g_ref, u)).astype(o_ref.dtype))
        o_ref[...] = assemble(outs)
    else:
        y_ref[...] = assemble(y)
        bon_ref[...] = assemble(bonus)


def _wkv(r, k, v, wpre, apre, k_k, k_a, r_k, lay, reverse, final_args=None, lw=2048, chunks_per_step=2):
    M, D = r.shape
    C = WKV_CHUNK
    rows = chunks_per_step * C
    assert C == RWKV_HEAD and lay.nc % rows == 0 and lay.n % rows == 0 and D % WKV_GROUP == 0
    lw = _pick(D, lw)
    assert lw % WKV_GROUP == 0
    ncc, ncl = lay.nc // rows, lay.n // rows
    ctx_blk0 = lay.lat_rows // rows

    def row_blk(b, j):
        if reverse:
            return jnp.where(j < ncc, ctx_blk0 + b * ncc + (ncc - 1 - j), b * ncl + (ncl - 1 - (j - ncc)))
        return jnp.where(j < ncc, ctx_blk0 + b * ncc + j, b * ncl + (j - ncc))

    tile = lambda: pl.BlockSpec((rows, lw), lambda b, h, j: (row_blk(b, j), h))
    vec = lambda: pl.BlockSpec((1, lw), lambda b, h, j: (0, h))
    in_specs = [tile() for _ in range(5)] + [vec() for _ in range(3)]
    args = [r, k, v, wpre, apre, k_k.reshape(1, D), k_a.reshape(1, D), r_k.reshape(1, D)]
    final = final_args is not None
    if final:
        y0, b0, g, lng, lnb = final_args
        in_specs += [tile(), tile(), tile(), vec(), vec()]
        args += [y0, b0, g, lng.reshape(1, D), lnb.reshape(1, D)]
        out_specs = tile()
        out_shape = jax.ShapeDtypeStruct((M, D), _BF16)
    else:
        out_specs = [tile(), tile()]
        out_shape = [jax.ShapeDtypeStruct((M, D), _F32), jax.ShapeDtypeStruct((M, D), _F32)]
    return pl.pallas_call(
        functools.partial(_wkv_kernel, reverse, final, C, lw),
        grid=(lay.B, D // lw, ncc + ncl),
        in_specs=in_specs,
        out_specs=out_specs,
        out_shape=out_shape,
        scratch_shapes=[pltpu.VMEM((lw // WKV_GROUP, WKV_GROUP, WKV_GROUP), _F32)],
        compiler_params=_cparams("parallel", "parallel", "arbitrary"),
        name="wkv7_scan_rev" if reverse else "wkv7_scan_fwd",
    )(*args)


def _rope_tables(lay):
    n = lay.n
    t = jnp.arange(n, dtype=jnp.int32)
    row = (t // GRID_W).astype(_F32)
    col = (t % GRID_W).astype(_F32)
    axis_dim = HEAD_DIM // 2
    inv_freq = ROPE_THETA ** (-jnp.arange(0, axis_dim, 2, dtype=_F32) / axis_dim)
    ang = jnp.concatenate([row[:, None] * inv_freq, col[:, None] * inv_freq], axis=-1)
    cos, sin = jnp.cos(ang), jnp.sin(ang)
    cos_l = jnp.concatenate([cos, cos], axis=-1)
    sin_l = jnp.concatenate([-sin, sin], axis=-1)
    n_ctx_rows = lay.B * lay.nc
    cos_f = jnp.concatenate([jnp.tile(cos_l, (lay.B, 1)), jnp.ones((n_ctx_rows, HEAD_DIM), _F32)], axis=0)
    sin_f = jnp.concatenate([jnp.tile(sin_l, (lay.B, 1)), jnp.zeros((n_ctx_rows, HEAD_DIM), _F32)], axis=0)
    return cos_f, sin_f


def _deinterleave_heads(w, n_heads):
    lead = w.shape[:-1]
    w = w.reshape(*lead, n_heads, HEAD_DIM // 2, 2)
    w = jnp.swapaxes(w, -1, -2)
    return w.reshape(*lead, n_heads * HEAD_DIM)


def _pad_cols(w, mult=LANES):
    pad = (-w.shape[-1]) % mult
    return jnp.pad(w, [(0, 0)] * (w.ndim - 1) + [(0, pad)]) if pad else w


def _pad_rows(w, mult=LANES):
    pad = (-w.shape[-2]) % mult
    return jnp.pad(w, [(0, 0)] * (w.ndim - 2) + [(0, pad), (0, 0)]) if pad else w


def kernel(x, c, ctx, c_ctx, w_mod, b_mod, norm1_g, norm2_g, attn_w_in, attn_w_out, q_norm_g, k_norm_g, rwkv_mu, rwkv_w_r, rwkv_w_k, rwkv_w_v, rwkv_w_o, rwkv_decay_w0, rwkv_decay_w1, rwkv_decay_w2, rwkv_iclr_a0, rwkv_iclr_a1, rwkv_iclr_a2, rwkv_gate_g1, rwkv_gate_g2, rwkv_k_k, rwkv_k_a, rwkv_r_k, rwkv_lnx_g, rwkv_lnx_b, rwkv_vres_v0, rwkv_vres_v1, rwkv_vres_v2, ffn_w_up, ffn_conv_w, ffn_conv_b, ffn_w_down, final_norm_g):
    B, n, D = x.shape
    nc = ctx.shape[1]
    depth = w_mod.shape[0]
    lay = Layout(B, n, nc, B * n, B * (n + nc))
    attn_w = N_ATTN_HEADS * HEAD_DIM
    kv_w = N_KV_HEADS * HEAD_DIM
    fw = D - attn_w
    bf = lambda w: w.astype(_BF16)

    xs = jnp.concatenate([x.reshape(B * n, D), ctx.reshape(B * nc, D)], axis=0)

    cvec = jnp.concatenate([c, c_ctx[None, :], jnp.zeros((7 - B % 8, D), _F32)], axis=0)
    mods = _modulation(cvec, w_mod, b_mod)[:, :B + 1]
    mods = mods.reshape(depth, B + 1, 6, 1, D)
    cos_f, sin_f = _rope_tables(lay)
    qg_scale = HEAD_DIM ** -0.5

    v_first = None
    for i in range(depth):
        last = i == depth - 1
        jl = i // 2
        m = [mods[i, :, t] for t in range(6)]
        g1 = norm1_g[i].reshape(1, D)
        if i % 2 == 0:
            w_in = bf(attn_w_in[jl])
            w_in = jnp.concatenate([
                _deinterleave_heads(w_in[:, :attn_w], N_ATTN_HEADS),
                _deinterleave_heads(w_in[:, attn_w:attn_w + kv_w], N_KV_HEADS),
                w_in[:, attn_w + kv_w:]], axis=1)
            qg = (_deinterleave_heads(q_norm_g[jl], 1) * qg_scale).reshape(1, HEAD_DIM)
            kg = _deinterleave_heads(k_norm_g[jl], 1).reshape(1, HEAD_DIM)
            u = _attn_in_proj(xs, g1, m[0], m[1], w_in, qg, kg, cos_f, sin_f, lay)
            o = _attention(u, lay)
            fm = jnp.concatenate([_fourier_mix(u, lay, fw), _fourier_mix(u, lay, fw, ctx=True)], axis=0)
            w_out = bf(attn_w_out[jl])
            xs = _out_proj([o, fm], [w_out[:attn_w], w_out[attn_w:]], xs, m[2], lay)
        else:
            x_r, x_w, x_k, x_v, x_a, x_g = _shift_mix(xs, g1, m[0], m[1], rwkv_mu[jl], lay)
            r = _linear(x_r, bf(rwkv_w_r[jl]), name="rwkv_r")
            k = _linear(x_k, bf(rwkv_w_k[jl]), name="rwkv_k")
            v = _linear(x_v, bf(rwkv_w_v[jl]), name="rwkv_v")
            if jl == 0:
                v_first = v
            else:
                v = _vres(x_v, rwkv_vres_v1[jl - 1], rwkv_vres_v2[jl - 1], rwkv_vres_v0[jl - 1], v, v_first)
            (g,) = _lora(x_g, [rwkv_gate_g1[jl]], [rwkv_gate_g2[jl]], act="sigmoid", name="rwkv_gate")
            wpre = _lora(x_w, list(rwkv_decay_w1[jl]), list(rwkv_decay_w2[jl]), list(rwkv_decay_w0[jl]),
                         act="tanh", name="rwkv_decay")
            apre = _lora(x_a, list(rwkv_iclr_a1[jl]), list(rwkv_iclr_a2[jl]), list(rwkv_iclr_a0[jl]),
                         name="rwkv_iclr")
            kk_, ka_, rk_ = rwkv_k_k[jl], rwkv_k_a[jl], rwkv_r_k[jl].reshape(D)
            y0, bon0 = _wkv(r, k, v, wpre[0], apre[0], kk_, ka_, rk_, lay, reverse=False)
            z = _wkv(r, k, v, wpre[1], apre[1], kk_, ka_, rk_, lay, reverse=True,
                     final_args=(y0, bon0, g, rwkv_lnx_g[jl], rwkv_lnx_b[jl]))
            xs = _out_proj([z], [bf(rwkv_w_o[jl])], xs, m[2], lay)
        n_rows = lay.lat_rows if last else lay.rows
        xs = _ffn(xs, norm2_g[i].reshape(1, D), m[3], m[4], m[5], bf(ffn_w_up[i]), ffn_conv_w[i],
                  ffn_conv_b[i].reshape(1, -1), bf(ffn_w_down[i]), final_norm_g, lay, n_rows, final_norm=last)
    return xs.reshape(B, n, D)
```

```python
import collections
import functools
import math

import jax
import jax.numpy as jnp
from jax import lax
from jax.experimental import pallas as pl
from jax.experimental.pallas import tpu as pltpu

HEAD_DIM = 128
N_ATTN_HEADS = 12
N_KV_HEADS = 4
GRID_W = 64
ROPE_THETA = 10000.0
N_FOURIER_GROUPS = 4
RWKV_HEAD = 64
CONV_W = 3
NORM_EPS = 1e-6
GN_EPS = 64e-5

LANES = 128
MXU_TILE = 256
BF16_ROWS = 16
VMEM_LIMIT = 56 * 1024 * 1024
WKV_CHUNK = 64
WKV_GROUP = MXU_TILE

_F32 = jnp.float32
_BF16 = jnp.bfloat16

Layout = collections.namedtuple("Layout", "B n nc lat_rows rows")


def _cparams(*sem):
    return pltpu.CompilerParams(dimension_semantics=sem, vmem_limit_bytes=VMEM_LIMIT)


def _pick(total, pref):
    t = min(pref, total)
    while total % t:
        t -= 1
    return t


def _row_tile(lay, pref):
    return _pick(lay.B * lay.nc, _pick(lay.n, pref))


def _seg_index(lay, tm):
    def f(i):
        r0 = i * tm
        return jnp.where(r0 < lay.lat_rows, r0 // lay.n, lay.B)
    return f


def _seg_masks(i, tm, shape, lay):
    t = lax.broadcasted_iota(jnp.int32, shape, 0)
    r0 = i * tm
    is_lat = r0 < lay.lat_rows
    if tm >= lay.nc:
        pc = lax.rem(t, lay.nc)
    else:
        pc = t + lax.rem(r0 - lay.lat_rows, lay.nc)
    pos = jnp.where(is_lat, t + lax.rem(r0, lay.n), pc)
    last = jnp.where(is_lat, lay.n - 1, lay.nc - 1)
    return pos != 0, pos != last


def _normmod(x, g, shift, scale):
    ms = jnp.mean(x * x, axis=-1, keepdims=True)
    return (x * lax.rsqrt(ms + NORM_EPS)) * g * (1.0 + scale) + shift


def _sigmoid(x):
    return 1.0 / (1.0 + jnp.exp(-x))


def _dot(a, b, prec=None):
    return jnp.dot(a, b, preferred_element_type=_F32, precision=prec)


def _dot_nt(a, b):
    return lax.dot_general(a, b, (((1,), (1,)), ((), ())), preferred_element_type=_F32)


def _dot_tn(a, b):
    return lax.dot_general(a, b, (((0,), (0,)), ((), ())), preferred_element_type=_F32)


def _bf16_pieces(x, parts):
    pieces = []
    rem = x
    for i in range(parts):
        piece = rem.astype(_BF16)
        pieces.append(piece)
        if i + 1 < parts:
            rem = rem - piece.astype(_F32)
    return pieces


def _ones_dot_left(ones, x, parts):
    acc = None
    for piece in _bf16_pieces(x, parts):
        d = _dot(ones, piece)
        acc = d if acc is None else acc + d
    return acc


def _ones_dot_rows(ones, xs, parts):
    rows = xs[0].shape[0]
    stacked = jnp.concatenate([p for x in xs for p in _bf16_pieces(x, parts)], axis=0)
    res = _dot(stacked, ones)
    outs = []
    for n in range(len(xs)):
        base = n * parts * rows
        acc = res[base:base + rows]
        for i in range(1, parts):
            acc = acc + res[base + i * rows:base + (i + 1) * rows]
        outs.append(acc)
    return outs


def _mod_kernel(c_ref, w_ref, b_ref, o_ref):
    c = c_ref[...]
    s = c * _sigmoid(c)
    rows = s.shape[0]
    res = _dot(jnp.concatenate(_bf16_pieces(s, 3), axis=0), w_ref[...].astype(_BF16))
    o_ref[...] = res[:rows] + res[rows:2 * rows] + res[2 * rows:] + b_ref[...]


def _modulation(cvec, w_mod, b_mod):
    depth, D, N = w_mod.shape
    rows = cvec.shape[0]
    tn = _pick(N, 512)
    return pl.pallas_call(
        _mod_kernel,
        grid=(depth, N // tn),
        in_specs=[
            pl.BlockSpec((rows, D), lambda l, j: (0, 0)),
            pl.BlockSpec((None, D, tn), lambda l, j: (l, 0, j)),
            pl.BlockSpec((None, 1, tn), lambda l, j: (l, 0, j)),
        ],
        out_specs=pl.BlockSpec((None, rows, tn), lambda l, j: (l, 0, j)),
        out_shape=jax.ShapeDtypeStruct((depth, rows, N), _F32),
        compiler_params=_cparams("parallel", "parallel"),
        name="adaln_modulation",
    )(cvec, w_mod, b_mod.reshape(depth, 1, N))


def _linear_kernel(x_ref, w_ref, o_ref):
    o_ref[...] = _dot(x_ref[...], w_ref[...]).astype(o_ref.dtype)


def _linear(x, w, out_dtype=_F32, tm=512, tn=2048, name="linear"):
    M, K = x.shape
    N = w.shape[1]
    tm, tn = _pick(M, tm), _pick(N, tn)
    return pl.pallas_call(
        _linear_kernel,
        grid=(M // tm, N // tn),
        in_specs=[pl.BlockSpec((tm, K), lambda i, j: (i, 0)), pl.BlockSpec((K, tn), lambda i, j: (0, j))],
        out_specs=pl.BlockSpec((tm, tn), lambda i, j: (i, j)),
        out_shape=jax.ShapeDtypeStruct((M, N), out_dtype),
        compiler_params=_cparams("parallel", "arbitrary"),
        name=name,
    )(x, w)


def _lora_w1(w1s):
    return jnp.concatenate([_pad_cols(w) for w in w1s], axis=1).astype(_BF16)


def _lora_out_kernel(ranks, has_bias, t_ref, *refs):
    n_out = len(ranks)
    w2_refs = refs[:n_out]
    b_refs = refs[n_out:2 * n_out] if has_bias else ()
    o_refs = refs[-n_out:]
    off = 0
    for d in range(n_out):
        y = _dot(t_ref[:, off:off + ranks[d]], w2_refs[d][...])
        if has_bias:
            y = y + b_refs[d][...]
        o_refs[d][...] = y
        off += ranks[d]


def _lora_out(t, w2s, biases=None, tm=512, name="lora_out"):
    M, R = t.shape
    N = w2s[0].shape[1]
    tm = _pick(M, tm)
    w2p = [_pad_rows(w).astype(_BF16) for w in w2s]
    ranks = tuple(w.shape[0] for w in w2p)
    assert sum(ranks) == R
    in_specs = [pl.BlockSpec((tm, R), lambda i: (i, 0))]
    in_specs += [pl.BlockSpec(w.shape, lambda i: (0, 0)) for w in w2p]
    args = [t, *w2p]
    if biases is not None:
        in_specs += [pl.BlockSpec((1, N), lambda i: (0, 0)) for _ in biases]
        args += [b.reshape(1, N) for b in biases]
    return pl.pallas_call(
        functools.partial(_lora_out_kernel, ranks, biases is not None),
        grid=(M // tm,),
        in_specs=in_specs,
        out_specs=[pl.BlockSpec((tm, N), lambda i: (i, 0)) for _ in w2p],
        out_shape=[jax.ShapeDtypeStruct((M, N), _F32) for _ in w2p],
        compiler_params=_cparams("parallel"),
        name=name,
    )(*args)


def _attn_in_kernel(sw, n_q, n_qk, x_ref, g_ref, sh_ref, sc_ref, w_ref, qg_ref, kg_ref, cos_ref, sin_ref, o_ref):
    h = _normmod(x_ref[...], g_ref[...], sh_ref[...], sc_ref[...]).astype(_BF16)
    cos, sin = cos_ref[...], sin_ref[...]
    n_blocks = w_ref.shape[1] // sw

    def finish(s, acc):
        cols = slice(s * sw, (s + 1) * sw)
        if s >= n_qk:
            o_ref[:, cols] = acc.astype(o_ref.dtype)
            return
        gain = qg_ref[...] if s < n_q else kg_ref[...]
        for hd in range(sw // HEAD_DIM):
            a = acc[:, hd * HEAD_DIM:(hd + 1) * HEAD_DIM]
            y = a * lax.rsqrt(jnp.mean(a * a, axis=-1, keepdims=True) + NORM_EPS) * gain
            y = y * cos + pltpu.roll(y, HEAD_DIM // 2, 1) * sin
            o_ref[:, s * sw + hd * HEAD_DIM:s * sw + (hd + 1) * HEAD_DIM] = y.astype(o_ref.dtype)

    acc = _dot(h, w_ref[:, 0:sw])
    for s in range(n_blocks):
        nxt = _dot(h, w_ref[:, (s + 1) * sw:(s + 2) * sw]) if s + 1 < n_blocks else None
        finish(s, acc)
        acc = nxt


def _attn_in_proj(x, g, shift, scale, w, qg, kg, cos, sin, lay, tm=512):
    M, D = x.shape
    N = w.shape[1]
    tm = _row_tile(lay, tm)
    attn_w = N_ATTN_HEADS * HEAD_DIM
    kv_w = N_KV_HEADS * HEAD_DIM
    sw = math.gcd(math.gcd(attn_w, kv_w), MXU_TILE)
    seg = _seg_index(lay, tm)
    vec = lambda: pl.BlockSpec((None, 1, D), lambda i: (seg(i), 0, 0))
    return pl.pallas_call(
        functools.partial(_attn_in_kernel, sw, attn_w // sw, (attn_w + kv_w) // sw),
        grid=(M // tm,),
        in_specs=[
            pl.BlockSpec((tm, D), lambda i: (i, 0)),
            pl.BlockSpec((1, D), lambda i: (0, 0)),
            vec(), vec(),
            pl.BlockSpec((D, N), lambda i: (0, 0)),
            pl.BlockSpec((1, HEAD_DIM), lambda i: (0, 0)),
            pl.BlockSpec((1, HEAD_DIM), lambda i: (0, 0)),
            pl.BlockSpec((tm, HEAD_DIM), lambda i: (i, 0)),
            pl.BlockSpec((tm, HEAD_DIM), lambda i: (i, 0)),
        ],
        out_specs=pl.BlockSpec((tm, N), lambda i: (i, 0)),
        out_shape=jax.ShapeDtypeStruct((M, N), _BF16),
        compiler_params=_cparams("parallel"),
        name="attn_in_proj",
    )(x, g, shift, scale, w, qg, kg, cos, sin)


def _attn_kernel(n_lat_tiles, g_per_kv, kb, q_ref, kc_ref, vc_ref, kl_ref, vl_ref, o_ref):
    qi = pl.program_id(2)
    tq = q_ref.shape[0]
    q = jnp.concatenate([q_ref[:, g * HEAD_DIM:(g + 1) * HEAD_DIM] for g in range(g_per_kv)], axis=0)

    def finish(acc, l):
        o = acc / l
        for g in range(g_per_kv):
            o_ref[:, g * HEAD_DIM:(g + 1) * HEAD_DIM] = o[g * tq:(g + 1) * tq].astype(o_ref.dtype)

    s_c = _dot_nt(q, kc_ref[...])
    m = jnp.max(s_c, axis=-1, keepdims=True)
    p = jnp.exp(s_c - m)
    l = jnp.sum(p, axis=-1, keepdims=True)
    acc = _dot(p.astype(_BF16), vc_ref[...])

    @pl.when(qi < n_lat_tiles)
    def _():
        n_blocks = kl_ref.shape[0] // kb
        mj, lj, accj = m, l, acc
        s_next = _dot_nt(q, kl_ref[0:kb, :])
        for j in range(n_blocks):
            s = s_next
            if j + 1 < n_blocks:
                s_next = _dot_nt(q, kl_ref[(j + 1) * kb:(j + 2) * kb, :])
            m_new = jnp.maximum(mj, jnp.max(s, axis=-1, keepdims=True))
            alpha = jnp.exp(mj - m_new)
            pj = jnp.exp(s - m_new)
            lj = lj * alpha + jnp.sum(pj, axis=-1, keepdims=True)
            accj = accj * alpha + _dot(pj.astype(_BF16), vl_ref[j * kb:(j + 1) * kb, :])
            mj = m_new
        finish(accj, lj)

    @pl.when(qi >= n_lat_tiles)
    def _():
        finish(acc, l)


def _attention(u, lay, tq=256, kb=512):
    M = u.shape[0]
    g_per_kv = N_ATTN_HEADS // N_KV_HEADS
    qw = g_per_kv * HEAD_DIM
    attn_w = N_ATTN_HEADS * HEAD_DIM
    tq = _pick(lay.nc, tq)
    nlt, nct = lay.n // tq, lay.nc // tq
    k_blk = attn_w // HEAD_DIM
    v_blk = k_blk + N_KV_HEADS
    ctx_blk0 = lay.lat_rows // lay.nc

    def q_map(b, kv, qi):
        row = jnp.where(qi < nlt, b * nlt + qi, lay.lat_rows // tq + b * nct + (qi - nlt))
        return (row, kv)

    return pl.pallas_call(
        functools.partial(_attn_kernel, nlt, g_per_kv, _pick(lay.n, kb)),
        grid=(lay.B, N_KV_HEADS, nlt + nct),
        in_specs=[
            pl.BlockSpec((tq, qw), q_map),
            pl.BlockSpec((lay.nc, HEAD_DIM), lambda b, kv, qi: (ctx_blk0 + b, k_blk + kv)),
            pl.BlockSpec((lay.nc, HEAD_DIM), lambda b, kv, qi: (ctx_blk0 + b, v_blk + kv)),
            pl.BlockSpec((lay.n, HEAD_DIM), lambda b, kv, qi: (b, k_blk + kv)),
            pl.BlockSpec((lay.n, HEAD_DIM), lambda b, kv, qi: (b, v_blk + kv)),
        ],
        out_specs=pl.BlockSpec((tq, qw), q_map),
        out_shape=jax.ShapeDtypeStruct((M, attn_w), _BF16),
        compiler_params=_cparams("parallel", "parallel", "arbitrary"),
        name="gqa_attention",
    )(u, u, u, u, u)


def _dft_kernel(f_ref, cc_ref, sc_ref, cn_ref, sn_ref, o_ref, a_ref, b_ref):
    @pl.when(pl.program_id(1) == 0)
    def _():
        f = f_ref[...]
        a_ref[...] = _dot(f, cc_ref[...]).astype(_BF16)
        b_ref[...] = _dot(f, sc_ref[...]).astype(_BF16)

    o_ref[...] = (_dot(cn_ref[...], a_ref[...]) - _dot(sn_ref[...], b_ref[...])).astype(o_ref.dtype)


def _cos_sin_matrix(m, split=64):
    j = jnp.arange(m, dtype=jnp.int32)[:, None]
    w = 2.0 * math.pi / m

    def cs(k):
        ang = w * ((j * k[None, :]) % m).astype(_F32)
        return jnp.cos(ang), jnp.sin(ang)

    if m % split or m <= 8 * split:
        return cs(jnp.arange(m, dtype=jnp.int32))
    ca, sa = cs(split * jnp.arange(m // split, dtype=jnp.int32))
    cb, sb = cs(jnp.arange(split, dtype=jnp.int32))
    cos = ca[:, :, None] * cb[:, None, :] - sa[:, :, None] * sb[:, None, :]
    sin = sa[:, :, None] * cb[:, None, :] + ca[:, :, None] * sb[:, None, :]
    return cos.reshape(m, m), sin.reshape(m, m)


def _dft_tables(n, group, n_groups):
    cn, sn = _cos_sin_matrix(n)
    scale = 1.0 / math.sqrt(n * group)
    cg, sg = _cos_sin_matrix(group)
    eye = jnp.eye(n_groups, dtype=_F32)
    return ((cn * scale).astype(_BF16), (sn * scale).astype(_BF16),
            jnp.kron(eye, cg).astype(_BF16), jnp.kron(eye, sg).astype(_BF16))


def _fourier_mix(u, lay, fw, ctx=False, tm=512):
    M, N = u.shape
    n = lay.nc if ctx else lay.n
    tm = _pick(n, tm)
    nt = n // tm
    cn, sn, cc, sc = _dft_tables(n, fw // N_FOURIER_GROUPS, N_FOURIER_GROUPS)
    f_blk = N // fw - 1
    seg0 = lay.lat_rows // n if ctx else 0
    return pl.pallas_call(
        _dft_kernel,
        grid=(lay.B, nt),
        in_specs=[
            pl.BlockSpec((n, fw), lambda b, i: (seg0 + b, f_blk)),
            pl.BlockSpec((fw, fw), lambda b, i: (0, 0)),
            pl.BlockSpec((fw, fw), lambda b, i: (0, 0)),
            pl.BlockSpec((tm, n), lambda b, i: (i, 0)),
            pl.BlockSpec((tm, n), lambda b, i: (i, 0)),
        ],
        out_specs=pl.BlockSpec((tm, fw), lambda b, i: (b * nt + i, 0)),
        out_shape=jax.ShapeDtypeStruct((lay.B * n, fw), _BF16),
        scratch_shapes=[pltpu.VMEM((n, fw), _BF16), pltpu.VMEM((n, fw), _BF16)],
        compiler_params=_cparams("parallel", "arbitrary"),
        name="fourier_mix_ctx" if ctx else "fourier_mix_lat",
    )(u, cc, sc, cn, sn)


def _out_proj_kernel(n_lhs, *refs):
    lhs = refs[:n_lhs]
    ws = refs[n_lhs:2 * n_lhs]
    x_ref, gate_ref, o_ref = refs[2 * n_lhs], refs[2 * n_lhs + 1], refs[2 * n_lhs + 2]
    acc = _dot(lhs[0][...], ws[0][...])
    for a, w in zip(lhs[1:], ws[1:]):
        acc = acc + _dot(a[...], w[...])
    o_ref[...] = x_ref[...] + gate_ref[...] * acc


def _out_proj(lhs, ws, x, gate, lay, tm=512):
    M, D = x.shape
    tm = _row_tile(lay, tm)
    seg = _seg_index(lay, tm)
    in_specs = [pl.BlockSpec((tm, a.shape[1]), lambda i: (i, 0)) for a in lhs]
    in_specs += [pl.BlockSpec(w.shape, lambda i: (0, 0)) for w in ws]
    in_specs += [pl.BlockSpec((tm, D), lambda i: (i, 0)), pl.BlockSpec((None, 1, D), lambda i: (seg(i), 0, 0))]
    return pl.pallas_call(
        functools.partial(_out_proj_kernel, len(lhs)),
        grid=(M // tm,),
        in_specs=in_specs,
        out_specs=pl.BlockSpec((tm, D), lambda i: (i, 0)),
        out_shape=jax.ShapeDtypeStruct((M, D), _F32),
        compiler_params=_cparams("parallel"),
        name="out_proj_residual",
    )(*lhs, *ws, x, gate)


def _ffn_kernel(n_lat_tiles, n_ctx_tiles, tiles_per_seg, nc, tm, halo, n_sub, final_norm, x_ref, xp_ref, xn_ref,
                g_ref, sh_ref, sc_ref, gate_ref, wg_ref, wv_ref, cwg_ref, cwv_ref, cbg_ref, cbv_ref, wd_ref, fg_ref,
                o_ref, h_ref, acc_ref, u_ref):
    i, j = pl.program_id(0), pl.program_id(1)
    ext = tm + 2 * halo
    is_lat = i < n_lat_tiles

    @pl.when(j == 0)
    def _():
        g, sh, sc = g_ref[...], sh_ref[...], sc_ref[...]
        pos = lax.rem(i, tiles_per_seg)
        keep_prev = jnp.logical_and(is_lat, pos != 0)
        keep_next = jnp.logical_and(is_lat, pos != tiles_per_seg - 1)
        hp = _normmod(xp_ref[...], g, sh, sc)
        hn = _normmod(xn_ref[...], g, sh, sc)
        h_ref[0:halo, :] = jnp.where(keep_prev, hp, 0.0).astype(_BF16)
        h_ref[halo:halo + tm, :] = _normmod(x_ref[...], g, sh, sc).astype(_BF16)
        h_ref[halo + tm:ext, :] = jnp.where(keep_next, hn, 0.0).astype(_BF16)
        acc_ref[...] = jnp.zeros_like(acc_ref)

    def step(inner_boundaries):
        h = h_ref[...]
        tf = wg_ref.shape[1]
        ts = tf // n_sub
        subs = [slice(s * ts, (s + 1) * ts) for s in range(n_sub)]

        def conv(u_ref, cw, cb):
            up = u_ref[halo - 1:halo - 1 + tm, :]
            un = u_ref[halo + 1:halo + 1 + tm, :]
            if inner_boundaries:
                t = lax.rem(lax.broadcasted_iota(jnp.int32, up.shape, 0), nc)
                up = jnp.where(t != 0, up, 0.0)
                un = jnp.where(t != nc - 1, un, 0.0)
            return up * cw[0:1] + u_ref[halo:halo + tm, :] * cw[1:2] + un * cw[2:3] + cb

        def up_proj(s):
            u_ref[s, 0] = _dot(h, wg_ref[:, subs[s]])
            u_ref[s, 1] = _dot(h, wv_ref[:, subs[s]])

        up_proj(0)
        for s, sl in enumerate(subs):
            if s + 1 < n_sub:
                up_proj(s + 1)
            gt = conv(u_ref.at[s, 0], cwg_ref[:, sl], cbg_ref[:, sl])
            vl = conv(u_ref.at[s, 1], cwv_ref[:, sl], cbv_ref[:, sl])
            act = (gt * _sigmoid(gt) * vl).astype(_BF16)
            acc_ref[...] += _dot(act, wd_ref[sl, :])

    if n_ctx_tiles == 0:
        step(False)
    else:
        pl.when(is_lat)(functools.partial(step, False))
        pl.when(jnp.logical_not(is_lat))(functools.partial(step, tm > nc))

    @pl.when(j == pl.num_programs(1) - 1)
    def _():
        y = x_ref[...] + gate_ref[...] * acc_ref[...]
        if final_norm:
            y = y * lax.rsqrt(jnp.mean(y * y, axis=-1, keepdims=True) + NORM_EPS) * fg_ref[...]
        o_ref[...] = y


def _ffn(x, g, shift, scale, gate, w_up, conv_w, conv_b, w_down, final_g, lay, n_rows, final_norm, tm=512, tf=512,
         sub=MXU_TILE):
    M, D = x.shape
    F = w_down.shape[0]
    tm = _row_tile(lay, tm)
    assert tm % lay.nc == 0, "context tiles must hold whole segments"
    tiles_per_seg = lay.n // tm
    n_lat_tiles = lay.lat_rows // tm
    n_tiles = n_rows // tm
    tf = _pick(F, tf)
    nf = F // tf
    halo = BF16_ROWS
    per = tm // halo
    seg = _seg_index(lay, tm)
    vec = lambda: pl.BlockSpec((None, 1, D), lambda i, j: (seg(i), 0, 0))
    n_sub = tf // _pick(tf, sub)
    return pl.pallas_call(
        functools.partial(_ffn_kernel, n_lat_tiles, n_tiles - n_lat_tiles, tiles_per_seg, lay.nc, tm, halo, n_sub,
                          final_norm),
        grid=(n_tiles, nf),
        in_specs=[
            pl.BlockSpec((tm, D), lambda i, j: (i, 0)),
            pl.BlockSpec((halo, D), lambda i, j: (jnp.maximum(i * per - 1, 0), 0)),
            pl.BlockSpec((halo, D), lambda i, j: (jnp.minimum((i + 1) * per, M // halo - 1), 0)),
            pl.BlockSpec((1, D), lambda i, j: (0, 0)),
            vec(), vec(), vec(),
            pl.BlockSpec((D, tf), lambda i, j: (0, j)),
            pl.BlockSpec((D, tf), lambda i, j: (0, nf + j)),
            pl.BlockSpec((CONV_W, tf), lambda i, j: (0, j)),
            pl.BlockSpec((CONV_W, tf), lambda i, j: (0, nf + j)),
            pl.BlockSpec((1, tf), lambda i, j: (0, j)),
            pl.BlockSpec((1, tf), lambda i, j: (0, nf + j)),
            pl.BlockSpec((tf, D), lambda i, j: (j, 0)),
            pl.BlockSpec((1, D), lambda i, j: (0, 0)),
        ],
        out_specs=pl.BlockSpec((tm, D), lambda i, j: (i, 0)),
        out_shape=jax.ShapeDtypeStruct((n_rows, D), _F32),
        scratch_shapes=[pltpu.VMEM((tm + 2 * halo, D), _BF16), pltpu.VMEM((tm, D), _F32),
                        pltpu.VMEM((n_sub, 2, tm + 2 * halo, tf // n_sub), _F32)],
        compiler_params=_cparams("parallel", "arbitrary"),
        name="conv_gated_ffn",
    )(x, x, x, g, shift, scale, gate, w_up, w_up, conv_w, conv_w, conv_b, conv_b, w_down, final_g.reshape(1, D))


def _shift_mix_kernel(lay, tm, halo, full, loras, x_ref, xp_ref, xn_ref, g_ref, sh_ref, sc_ref, mu_ref, *refs):
    w_refs = refs[:len(loras)]
    o_refs = refs[len(loras):]
    i = pl.program_id(0)
    g, sh, sc = g_ref[...], sh_ref[...], sc_ref[...]
    h = _normmod(x_ref[...], g, sh, sc)
    hp = _normmod(xp_ref[...], g, sh, sc)
    hn = _normmod(xn_ref[...], g, sh, sc)
    ext = jnp.concatenate([hp, h, hn], axis=0)
    n_ext = tm + 2 * halo
    has_prev, has_next = _seg_masks(i, tm, h.shape, lay)
    prev = jnp.where(has_prev, pltpu.roll(ext, 1, 0)[halo:halo + tm], 0.0)
    nxt = jnp.where(has_next, pltpu.roll(ext, n_ext - 1, 0)[halo:halo + tm], 0.0)
    dx = 0.5 * (prev + nxt) - h
    mu = mu_ref[...]
    mixes = {}

    def mix(m):
        if m not in mixes:
            mixes[m] = (h + dx * mu[m:m + 1]).astype(_BF16)
        return mixes[m]

    for m, o_ref in zip(full, o_refs):
        o_ref[...] = mix(m)
    for (m, act), w_ref, o_ref in zip(loras, w_refs, o_refs[len(full):]):
        t = _dot(mix(m), w_ref[...])
        if act == "tanh":
            t = jnp.tanh(t)
        elif act == "sigmoid":
            t = _sigmoid(t)
        o_ref[...] = t.astype(o_ref.dtype)


def _shift_mix(x, g, shift, scale, mu, full, loras, lay, tm=256):
    M, D = x.shape
    tm = _row_tile(lay, tm)
    halo = 8
    per = tm // halo
    seg = _seg_index(lay, tm)
    vec = lambda: pl.BlockSpec((None, 1, D), lambda i: (seg(i), 0, 0))
    w1s = [w for _, w, _ in loras]
    outs = pl.pallas_call(
        functools.partial(_shift_mix_kernel, lay, tm, halo, tuple(full), tuple((m, act) for m, _, act in loras)),
        grid=(M // tm,),
        in_specs=[
            pl.BlockSpec((tm, D), lambda i: (i, 0)),
            pl.BlockSpec((halo, D), lambda i: (jnp.maximum(i * per - 1, 0), 0)),
            pl.BlockSpec((halo, D), lambda i: (jnp.minimum((i + 1) * per, M // halo - 1), 0)),
            pl.BlockSpec((1, D), lambda i: (0, 0)),
            vec(), vec(),
            pl.BlockSpec(mu.shape, lambda i: (0, 0)),
        ] + [pl.BlockSpec(w.shape, lambda i: (0, 0)) for w in w1s],
        out_specs=[pl.BlockSpec((tm, D), lambda i: (i, 0)) for _ in full]
        + [pl.BlockSpec((tm, w.shape[1]), lambda i: (i, 0)) for w in w1s],
        out_shape=[jax.ShapeDtypeStruct((M, D), _BF16) for _ in full]
        + [jax.ShapeDtypeStruct((M, w.shape[1]), _BF16) for w in w1s],
        compiler_params=_cparams("parallel"),
        name="rwkv_shift_mix",
    )(x, x, x, g, shift, scale, mu, *w1s)
    return outs[:len(full)], outs[len(full):]


def _vres_kernel(t_ref, w2_ref, b_ref, v_ref, vf_ref, o_ref):
    gate = _sigmoid(_dot(t_ref[...], w2_ref[...]) + b_ref[...])
    v = v_ref[...]
    o_ref[...] = v + (vf_ref[...] - v) * gate


def _vres(t, v2, v0, v, v_first, tm=512):
    M, R = t.shape
    D = v.shape[1]
    w2 = _pad_rows(v2).astype(_BF16)
    tm = _pick(M, tm)
    return pl.pallas_call(
        _vres_kernel,
        grid=(M // tm,),
        in_specs=[
            pl.BlockSpec((tm, R), lambda i: (i, 0)),
            pl.BlockSpec((R, D), lambda i: (0, 0)),
            pl.BlockSpec((1, D), lambda i: (0, 0)),
            pl.BlockSpec((tm, D), lambda i: (i, 0)),
            pl.BlockSpec((tm, D), lambda i: (i, 0)),
        ],
        out_specs=pl.BlockSpec((tm, D), lambda i: (i, 0)),
        out_shape=jax.ShapeDtypeStruct((M, D), _F32),
        compiler_params=_cparams("parallel"),
        name="rwkv_value_residual",
    )(t, w2, v0.reshape(1, D), v, v_first)


def _wkv_kernel(reverse, final, C, LW, r_ref, k_ref, v_ref, wp_ref, ap_ref, kk_ref, ka_ref, rk_ref, *refs):
    if final:
        y0_ref, b0_ref, g_ref, lng_ref, lnb_ref, o_ref, s_ref = refs
    else:
        y_ref, bon_ref, s_ref = refs
    j = pl.program_id(2)

    @pl.when(j == 0)
    def _():
        s_ref[...] = jnp.zeros_like(s_ref)

    hpg = WKV_GROUP // RWKV_HEAD
    groups = range(LW // WKV_GROUP)
    ng = len(groups)
    sls = [slice(g * WKV_GROUP, (g + 1) * WKV_GROUP) for g in groups]
    row = lax.broadcasted_iota(jnp.int32, (C, C), 0)
    col = lax.broadcasted_iota(jnp.int32, (C, C), 1)
    tri = jnp.where((col >= row) if reverse else (col <= row), 1.0, 0.0).astype(_BF16)
    rc = lax.broadcasted_iota(jnp.int32, (C, WKV_GROUP), 0)
    cc = lax.rem(lax.broadcasted_iota(jnp.int32, (C, WKV_GROUP), 1), C)
    strict = (cc > rc) if reverse else (cc < rc)
    incl = (cc >= rc) if reverse else (cc <= rc)
    eye_cat = jnp.where(cc == rc, 1.0, 0.0).astype(_F32)
    br = lax.broadcasted_iota(jnp.int32, (WKV_GROUP, WKV_GROUP), 0) // RWKV_HEAD
    bc = lax.broadcasted_iota(jnp.int32, (WKV_GROUP, WKV_GROUP), 1) // RWKV_HEAD
    same_head = br == bc
    head_ones = jnp.where(same_head, 1.0, 0.0).astype(_BF16)

    def bd(x):
        return jnp.where(same_head, jnp.concatenate([x] * hpg, axis=0), jnp.zeros((), x.dtype))

    cast = lambda x: x.astype(_BF16)

    n_chunks = r_ref.shape[0] // C
    units = [(c, g) for c in range(n_chunks) for g in groups]
    nu = len(units)
    tile = lambda ref, u: ref[u[0] * C:(u[0] + 1) * C, sls[u[1]]]
    lane = lambda ref, u: ref[:, sls[u[1]]]

    v = [tile(v_ref, u) for u in units]
    lw = []
    for u in units:
        wp = tile(wp_ref, u)
        sp = jnp.maximum(-wp, 0.0) + jnp.log(1.0 + jnp.exp(-jnp.abs(wp)))
        lw.append(-jnp.exp(-sp - 0.5))
    cl = [_ones_dot_left(tri, x, 3) for x in lw]
    total = [c[0:1] if reverse else c[C - 1:C] for c in cl]

    a = [_sigmoid(tile(ap_ref, u)) for u in units]
    kk = [tile(k_ref, u) * lane(kk_ref, u) for u in units]
    kd = [tile(k_ref, u) * (1.0 + (a[i] - 1.0) * lane(ka_ref, u)) for i, u in enumerate(units)]
    rkd = [tile(r_ref, u) * kd[i] * lane(rk_ref, u) for i, u in enumerate(units)]
    sums = _ones_dot_rows(head_ones, [x * x for x in kk] + rkd, 2)
    kk = [x / jnp.maximum(jnp.sqrt(n), 1e-12) for x, n in zip(kk, sums[:nu])]
    bonus = [s * x for s, x in zip(sums[nu:], v)]
    b = [kk[i] * a[i] for i in range(nu)]

    AR = [jnp.concatenate([cast(-kk[i] * jnp.exp(cl[i] - lw[i])), cast(tile(r_ref, u) * jnp.exp(cl[i]))], axis=0)
          for i, u in enumerate(units)]
    BKbd, BKp = [], []
    for i in range(nu):
        e_neg = jnp.exp(-cl[i])
        e_rem = jnp.exp(total[i] - cl[i])
        BKbd.append(jnp.concatenate([bd(cast(b[i] * e_neg)), bd(cast(kd[i] * e_neg))], axis=0))
        BKp.append(cast(jnp.concatenate([b[i] * e_rem, kd[i] * e_rem], axis=0)))
    vb = [cast(x) for x in v]
    Vbd = [bd(x) for x in vb]

    M = [_dot_nt(AR[i], BKbd[i]) for i in range(nu)]
    Lab = [jnp.where(strict, m[:C, :WKV_GROUP], 0.0) for m in M]
    LMk = [cast(jnp.concatenate([jnp.where(strict, m[:C, WKV_GROUP:], 0.0),
                                 jnp.where(incl, m[C:, WKV_GROUP:], 0.0)], axis=0)) for m in M]
    Mrb = [cast(jnp.where(incl, m[C:, :WKV_GROUP], 0.0)) for m in M]
    LMkV = [_dot(LMk[i], Vbd[i]) for i in range(nu)]

    Pw = [cast(x) for x in Lab]
    T = [eye_cat + x for x in Lab]
    Pw = [cast(_dot(x, bd(x))) for x in Pw]
    for _ in range(int(math.log2(C)) - 2):
        both = [_dot(jnp.concatenate([cast(t), x], axis=0), bd(x)) for t, x in zip(T, Pw)]
        T = [t + bt[:C] for t, bt in zip(T, both)]
        Pw = [cast(bt[C:]) for bt in both]
    Tb = [cast(t + _dot(cast(t), bd(x))) for t, x in zip(T, Pw)]

    S = [s_ref[g] for g in groups]
    y = [None] * nu
    for c in (reversed(range(n_chunks)) if reverse else range(n_chunks)):
        us = [c * ng + g for g in groups]
        ARS = [_dot_nt(AR[i], cast(S[g])) for g, i in enumerate(us)]
        X = [ARS[g][:C] + LMkV[i][:C] for g, i in enumerate(us)]
        Z = [cast(_dot(Tb[i], bd(cast(X[g])))) for g, i in enumerate(us)]
        for g, i in enumerate(us):
            y[i] = ARS[g][C:] + LMkV[i][C:] + _dot(Mrb[i], bd(Z[g]))
        dS = [_dot_tn(jnp.concatenate([Z[g], vb[i]], axis=0), BKp[i]) for g, i in enumerate(us)]
        S = [S[g] * jnp.exp(total[i]) + jnp.where(same_head, dS[g], 0.0) for g, i in enumerate(us)]
    s_ref[...] = jnp.stack(S, axis=0)

    def assemble(xs):
        return jnp.concatenate([jnp.concatenate(xs[c * ng:(c + 1) * ng], axis=1) for c in range(n_chunks)], axis=0)

    if final:
        inv = 1.0 / RWKV_HEAD
        wkv = [y[i] + tile(y0_ref, u) for i, u in enumerate(units)]
        cen = [x - s * inv for x, s in zip(wkv, _ones_dot_rows(head_ones, wkv, 3))]
        var = [s * inv for s in _ones_dot_rows(head_ones, [x * x for x in cen], 2)]
        outs = []
        for i, u in enumerate(units):
            normed = cen[i] * lax.rsqrt(var[i] + GN_EPS) * lane(lng_ref, u) + lane(lnb_ref, u)
            outs.append(((normed + bonus[i] + tile(b0_ref, u)) * tile(g_ref, u)).astype(o_ref.dtype))
        o_ref[...] = assemble(outs)
    else:
        y_ref[...] = assemble(y)
        bon_ref[...] = assemble(bonus)


def _wkv(r, k, v, wpre, apre, k_k, k_a, r_k, lay, reverse, final_args=None, lw=2048, chunks_per_step=2):
    M, D = r.shape
    C = WKV_CHUNK
    rows = chunks_per_step * C
    assert C == RWKV_HEAD and lay.nc % rows == 0 and lay.n % rows == 0 and D % WKV_GROUP == 0
    lw = _pick(D, lw)
    assert lw % WKV_GROUP == 0
    ncc, ncl = lay.nc // rows, lay.n // rows
    ctx_blk0 = lay.lat_rows // rows

    def row_blk(b, j):
        if reverse:
            return jnp.where(j < ncc, ctx_blk0 + b * ncc + (ncc - 1 - j), b * ncl + (ncl - 1 - (j - ncc)))
        return jnp.where(j < ncc, ctx_blk0 + b * ncc + j, b * ncl + (j - ncc))

    tile = lambda: pl.BlockSpec((rows, lw), lambda b, h, j: (row_blk(b, j), h))
    vec = lambda: pl.BlockSpec((1, lw), lambda b, h, j: (0, h))
    in_specs = [tile() for _ in range(5)] + [vec() for _ in range(3)]
    args = [r, k, v, wpre, apre, k_k.reshape(1, D), k_a.reshape(1, D), r_k.reshape(1, D)]
    final = final_args is not None
    if final:
        y0, b0, g, lng, lnb = final_args
        in_specs += [tile(), tile(), tile(), vec(), vec()]
        args += [y0, b0, g, lng.reshape(1, D), lnb.reshape(1, D)]
        out_specs = tile()
        out_shape = jax.ShapeDtypeStruct((M, D), _BF16)
    else:
        out_specs = [tile(), tile()]
        out_shape = [jax.ShapeDtypeStruct((M, D), _F32), jax.ShapeDtypeStruct((M, D), _F32)]
    return pl.pallas_call(
        functools.partial(_wkv_kernel, reverse, final, C, lw),
        grid=(lay.B, D // lw, ncc + ncl),
        in_specs=in_specs,
        out_specs=out_specs,
        out_shape=out_shape,
        scratch_shapes=[pltpu.VMEM((lw // WKV_GROUP, WKV_GROUP, WKV_GROUP), _F32)],
        compiler_params=_cparams("parallel", "parallel", "arbitrary"),
        name="wkv7_scan_rev" if reverse else "wkv7_scan_fwd",
    )(*args)


def _rope_tables(lay):
    n = lay.n
    t = jnp.arange(n, dtype=jnp.int32)
    row = (t // GRID_W).astype(_F32)
    col = (t % GRID_W).astype(_F32)
    axis_dim = HEAD_DIM // 2
    inv_freq = ROPE_THETA ** (-jnp.arange(0, axis_dim, 2, dtype=_F32) / axis_dim)
    ang = jnp.concatenate([row[:, None] * inv_freq, col[:, None] * inv_freq], axis=-1)
    cos, sin = jnp.cos(ang), jnp.sin(ang)
    cos_l = jnp.concatenate([cos, cos], axis=-1)
    sin_l = jnp.concatenate([-sin, sin], axis=-1)
    n_ctx_rows = lay.B * lay.nc
    cos_f = jnp.concatenate([jnp.tile(cos_l, (lay.B, 1)), jnp.ones((n_ctx_rows, HEAD_DIM), _F32)], axis=0)
    sin_f = jnp.concatenate([jnp.tile(sin_l, (lay.B, 1)), jnp.zeros((n_ctx_rows, HEAD_DIM), _F32)], axis=0)
    return cos_f, sin_f


def _deinterleave_heads(w, n_heads):
    lead = w.shape[:-1]
    w = w.reshape(*lead, n_heads, HEAD_DIM // 2, 2)
    w = jnp.swapaxes(w, -1, -2)
    return w.reshape(*lead, n_heads * HEAD_DIM)


def _pad_cols(w, mult=LANES):
    pad = (-w.shape[-1]) % mult
    return jnp.pad(w, [(0, 0)] * (w.ndim - 1) + [(0, pad)]) if pad else w


def _pad_rows(w, mult=LANES):
    pad = (-w.shape[-2]) % mult
    return jnp.pad(w, [(0, 0)] * (w.ndim - 2) + [(0, pad), (0, 0)]) if pad else w


def kernel(x, c, ctx, c_ctx, w_mod, b_mod, norm1_g, norm2_g, attn_w_in, attn_w_out, q_norm_g, k_norm_g, rwkv_mu, rwkv_w_r, rwkv_w_k, rwkv_w_v, rwkv_w_o, rwkv_decay_w0, rwkv_decay_w1, rwkv_decay_w2, rwkv_iclr_a0, rwkv_iclr_a1, rwkv_iclr_a2, rwkv_gate_g1, rwkv_gate_g2, rwkv_k_k, rwkv_k_a, rwkv_r_k, rwkv_lnx_g, rwkv_lnx_b, rwkv_vres_v0, rwkv_vres_v1, rwkv_vres_v2, ffn_w_up, ffn_conv_w, ffn_conv_b, ffn_w_down, final_norm_g):
    B, n, D = x.shape
    nc = ctx.shape[1]
    depth = w_mod.shape[0]
    lay = Layout(B, n, nc, B * n, B * (n + nc))
    attn_w = N_ATTN_HEADS * HEAD_DIM
    kv_w = N_KV_HEADS * HEAD_DIM
    fw = D - attn_w
    bf = lambda w: w.astype(_BF16)

    xs = jnp.concatenate([x.reshape(B * n, D), ctx.reshape(B * nc, D)], axis=0)

    cvec = jnp.concatenate([c, c_ctx[None, :], jnp.zeros((-(B + 1) % BF16_ROWS, D), _F32)], axis=0)
    mods = _modulation(cvec, w_mod, b_mod)[:, :B + 1]
    mods = mods.reshape(depth, B + 1, 6, 1, D)
    cos_f, sin_f = _rope_tables(lay)
    qg_scale = HEAD_DIM ** -0.5

    v_first = None
    for i in range(depth):
        last = i == depth - 1
        jl = i // 2
        m = [mods[i, :, t] for t in range(6)]
        g1 = norm1_g[i].reshape(1, D)
        if i % 2 == 0:
            w_in = bf(attn_w_in[jl])
            w_in = jnp.concatenate([
                _deinterleave_heads(w_in[:, :attn_w], N_ATTN_HEADS),
                _deinterleave_heads(w_in[:, attn_w:attn_w + kv_w], N_KV_HEADS),
                w_in[:, attn_w + kv_w:]], axis=1)
            qg = (_deinterleave_heads(q_norm_g[jl], 1) * qg_scale).reshape(1, HEAD_DIM)
            kg = _deinterleave_heads(k_norm_g[jl], 1).reshape(1, HEAD_DIM)
            u = _attn_in_proj(xs, g1, m[0], m[1], w_in, qg, kg, cos_f, sin_f, lay)
            o = _attention(u, lay)
            fm = jnp.concatenate([_fourier_mix(u, lay, fw), _fourier_mix(u, lay, fw, ctx=True)], axis=0)
            w_out = bf(attn_w_out[jl])
            xs = _out_proj([o, fm], [w_out[:attn_w], w_out[attn_w:]], xs, m[2], lay)
        else:
            loras = [(1, _lora_w1(list(rwkv_decay_w1[jl])), "tanh"), (4, _lora_w1(list(rwkv_iclr_a1[jl])), None),
                     (5, _lora_w1([rwkv_gate_g1[jl]]), "sigmoid")]
            if jl > 0:
                loras.append((3, _lora_w1([rwkv_vres_v1[jl - 1]]), None))
            (x_r, x_k, x_v), acts = _shift_mix(xs, g1, m[0], m[1], rwkv_mu[jl], (0, 2, 3), loras, lay)
            r = _linear(x_r, bf(rwkv_w_r[jl]), name="rwkv_r")
            k = _linear(x_k, bf(rwkv_w_k[jl]), name="rwkv_k")
            v = _linear(x_v, bf(rwkv_w_v[jl]), name="rwkv_v")
            if jl == 0:
                v_first = v
            else:
                v = _vres(acts[3], rwkv_vres_v2[jl - 1], rwkv_vres_v0[jl - 1], v, v_first)
            wpre = _lora_out(acts[0], list(rwkv_decay_w2[jl]), list(rwkv_decay_w0[jl]), name="rwkv_decay")
            apre = _lora_out(acts[1], list(rwkv_iclr_a2[jl]), list(rwkv_iclr_a0[jl]), name="rwkv_iclr")
            (g,) = _lora_out(acts[2], [rwkv_gate_g2[jl]], name="rwkv_gate")
            kk_, ka_, rk_ = rwkv_k_k[jl], rwkv_k_a[jl], rwkv_r_k[jl].reshape(D)
            y0, bon0 = _wkv(r, k, v, wpre[0], apre[0], kk_, ka_, rk_, lay, reverse=False)
            z = _wkv(r, k, v, wpre[1], apre[1], kk_, ka_, rk_, lay, reverse=True,
                     final_args=(y0, bon0, g, rwkv_lnx_g[jl], rwkv_lnx_b[jl]))
            xs = _out_proj([z], [bf(rwkv_w_o[jl])], xs, m[2], lay)
        n_rows = lay.lat_rows if last else lay.rows
        xs = _ffn(xs, norm2_g[i].reshape(1, D), m[3], m[4], m[5], bf(ffn_w_up[i]), ffn_conv_w[i],
                  ffn_conv_b[i].reshape(1, -1), bf(ffn_w_down[i]), final_norm_g, lay, n_rows, final_norm=last)
    return xs.reshape(B, n, D)
```

```python
import collections
import functools
import math

import jax
import jax.numpy as jnp
from jax import lax
from jax.experimental import pallas as pl
from jax.experimental.pallas import tpu as pltpu

HEAD_DIM = 128
N_ATTN_HEADS = 12
N_KV_HEADS = 4
GRID_W = 64
ROPE_THETA = 10000.0
N_FOURIER_GROUPS = 4
RWKV_HEAD = 64
CONV_W = 3
NORM_EPS = 1e-6
GN_EPS = 64e-5

LANES = 128
MXU_TILE = 256
BF16_ROWS = 16
VMEM_LIMIT = 56 * 1024 * 1024
WKV_CHUNK = 64
WKV_GROUP = MXU_TILE

_F32 = jnp.float32
_BF16 = jnp.bfloat16

Layout = collections.namedtuple("Layout", "B n nc lat_rows rows")


def _cparams(*sem):
    return pltpu.CompilerParams(dimension_semantics=sem, vmem_limit_bytes=VMEM_LIMIT)


def _pick(total, pref):
    t = min(pref, total)
    while total % t:
        t -= 1
    return t


def _row_tile(lay, pref):
    return _pick(lay.B * lay.nc, _pick(lay.n, pref))


def _seg_index(lay, tm):
    def f(i):
        r0 = i * tm
        return jnp.where(r0 < lay.lat_rows, r0 // lay.n, lay.B)
    return f


def _seg_masks(i, tm, shape, lay):
    t = lax.broadcasted_iota(jnp.int32, shape, 0)
    r0 = i * tm
    is_lat = r0 < lay.lat_rows
    if tm >= lay.nc:
        pc = lax.rem(t, lay.nc)
    else:
        pc = t + lax.rem(r0 - lay.lat_rows, lay.nc)
    pos = jnp.where(is_lat, t + lax.rem(r0, lay.n), pc)
    last = jnp.where(is_lat, lay.n - 1, lay.nc - 1)
    return pos != 0, pos != last


def _normmod(x, g, shift, scale):
    ms = jnp.mean(x * x, axis=-1, keepdims=True)
    return (x * lax.rsqrt(ms + NORM_EPS)) * g * (1.0 + scale) + shift


def _sigmoid(x):
    return 1.0 / (1.0 + jnp.exp(-x))


def _dot(a, b, prec=None):
    return jnp.dot(a, b, preferred_element_type=_F32, precision=prec)


def _dot_nt(a, b):
    return lax.dot_general(a, b, (((1,), (1,)), ((), ())), preferred_element_type=_F32)


def _dot_tn(a, b):
    return lax.dot_general(a, b, (((0,), (0,)), ((), ())), preferred_element_type=_F32)


def _bf16_pieces(x, parts):
    pieces = []
    rem = x
    for i in range(parts):
        piece = rem.astype(_BF16)
        pieces.append(piece)
        if i + 1 < parts:
            rem = rem - piece.astype(_F32)
    return pieces


def _ones_dot_left(ones, x, parts):
    acc = None
    for piece in _bf16_pieces(x, parts):
        d = _dot(ones, piece)
        acc = d if acc is None else acc + d
    return acc


def _ones_dot_rows(ones, xs, parts):
    rows = xs[0].shape[0]
    stacked = jnp.concatenate([p for x in xs for p in _bf16_pieces(x, parts)], axis=0)
    res = _dot(stacked, ones)
    outs = []
    for n in range(len(xs)):
        base = n * parts * rows
        acc = res[base:base + rows]
        for i in range(1, parts):
            acc = acc + res[base + i * rows:base + (i + 1) * rows]
        outs.append(acc)
    return outs


def _mod_kernel(c_ref, w_ref, b_ref, o_ref):
    c = c_ref[...]
    s = c * _sigmoid(c)
    rows = s.shape[0]
    res = _dot(jnp.concatenate(_bf16_pieces(s, 3), axis=0), w_ref[...].astype(_BF16))
    o_ref[...] = res[:rows] + res[rows:2 * rows] + res[2 * rows:] + b_ref[...]


def _modulation(cvec, w_mod, b_mod):
    depth, D, N = w_mod.shape
    rows = cvec.shape[0]
    tn = _pick(N, 512)
    return pl.pallas_call(
        _mod_kernel,
        grid=(depth, N // tn),
        in_specs=[
            pl.BlockSpec((rows, D), lambda l, j: (0, 0)),
            pl.BlockSpec((None, D, tn), lambda l, j: (l, 0, j)),
            pl.BlockSpec((None, 1, tn), lambda l, j: (l, 0, j)),
        ],
        out_specs=pl.BlockSpec((None, rows, tn), lambda l, j: (l, 0, j)),
        out_shape=jax.ShapeDtypeStruct((depth, rows, N), _F32),
        compiler_params=_cparams("parallel", "parallel"),
        name="adaln_modulation",
    )(cvec, w_mod, b_mod.reshape(depth, 1, N))


def _linear_kernel(has_vres, x_ref, w_ref, *refs):
    o_ref = refs[-1]
    acc = _dot(x_ref[...], w_ref[...])
    if has_vres:
        t_ref, w2_ref, b_ref, vf_ref = refs[:4]
        gate = _sigmoid(_dot(t_ref[...], w2_ref[...]) + b_ref[...])
        acc = acc + (vf_ref[...].astype(_F32) - acc) * gate
    o_ref[...] = acc.astype(o_ref.dtype)


def _linear(x, w, vres=None, out_dtype=_BF16, tm=512, name="linear"):
    M, K = x.shape
    N = w.shape[1]
    tm = _pick(M, tm)
    in_specs = [pl.BlockSpec((tm, K), lambda i: (i, 0)), pl.BlockSpec((K, N), lambda i: (0, 0))]
    args = [x, w]
    if vres is not None:
        t, v2, v0, v_first = vres
        w2 = _pad_rows(v2).astype(_BF16)
        R = t.shape[1]
        assert w2.shape[0] == R
        in_specs += [pl.BlockSpec((tm, R), lambda i: (i, 0)), pl.BlockSpec((R, N), lambda i: (0, 0)),
                     pl.BlockSpec((1, N), lambda i: (0, 0)), pl.BlockSpec((tm, N), lambda i: (i, 0))]
        args += [t, w2, v0.reshape(1, N), v_first]
    return pl.pallas_call(
        functools.partial(_linear_kernel, vres is not None),
        grid=(M // tm,),
        in_specs=in_specs,
        out_specs=pl.BlockSpec((tm, N), lambda i: (i, 0)),
        out_shape=jax.ShapeDtypeStruct((M, N), out_dtype),
        compiler_params=_cparams("parallel"),
        name=name,
    )(*args)


def _lora_w1(w1s):
    return jnp.concatenate([_pad_cols(w) for w in w1s], axis=1).astype(_BF16)


def _lora_out_kernel(ranks, has_bias, t_ref, *refs):
    n_out = len(ranks)
    w2_refs = refs[:n_out]
    b_refs = refs[n_out:2 * n_out] if has_bias else ()
    o_refs = refs[-n_out:]
    off = 0
    for d in range(n_out):
        y = _dot(t_ref[:, off:off + ranks[d]], w2_refs[d][...])
        if has_bias:
            y = y + b_refs[d][...]
        o_refs[d][...] = y
        off += ranks[d]


def _lora_out(t, w2s, biases=None, tm=512, name="lora_out"):
    M, R = t.shape
    N = w2s[0].shape[1]
    tm = _pick(M, tm)
    w2p = [_pad_rows(w).astype(_BF16) for w in w2s]
    ranks = tuple(w.shape[0] for w in w2p)
    assert sum(ranks) == R
    in_specs = [pl.BlockSpec((tm, R), lambda i: (i, 0))]
    in_specs += [pl.BlockSpec(w.shape, lambda i: (0, 0)) for w in w2p]
    args = [t, *w2p]
    if biases is not None:
        in_specs += [pl.BlockSpec((1, N), lambda i: (0, 0)) for _ in biases]
        args += [b.reshape(1, N) for b in biases]
    return pl.pallas_call(
        functools.partial(_lora_out_kernel, ranks, biases is not None),
        grid=(M // tm,),
        in_specs=in_specs,
        out_specs=[pl.BlockSpec((tm, N), lambda i: (i, 0)) for _ in w2p],
        out_shape=[jax.ShapeDtypeStruct((M, N), _F32) for _ in w2p],
        compiler_params=_cparams("parallel"),
        name=name,
    )(*args)


def _attn_in_kernel(sw, n_q, n_qk, x_ref, g_ref, sh_ref, sc_ref, w_ref, qg_ref, kg_ref, cos_ref, sin_ref, o_ref):
    h = _normmod(x_ref[...], g_ref[...], sh_ref[...], sc_ref[...]).astype(_BF16)
    cos, sin = cos_ref[...], sin_ref[...]
    n_blocks = w_ref.shape[1] // sw

    def finish(s, acc):
        cols = slice(s * sw, (s + 1) * sw)
        if s >= n_qk:
            o_ref[:, cols] = acc.astype(o_ref.dtype)
            return
        gain = qg_ref[...] if s < n_q else kg_ref[...]
        for hd in range(sw // HEAD_DIM):
            a = acc[:, hd * HEAD_DIM:(hd + 1) * HEAD_DIM]
            y = a * lax.rsqrt(jnp.mean(a * a, axis=-1, keepdims=True) + NORM_EPS) * gain
            y = y * cos + pltpu.roll(y, HEAD_DIM // 2, 1) * sin
            o_ref[:, s * sw + hd * HEAD_DIM:s * sw + (hd + 1) * HEAD_DIM] = y.astype(o_ref.dtype)

    acc = _dot(h, w_ref[:, 0:sw])
    for s in range(n_blocks):
        nxt = _dot(h, w_ref[:, (s + 1) * sw:(s + 2) * sw]) if s + 1 < n_blocks else None
        finish(s, acc)
        acc = nxt


def _attn_in_proj(x, g, shift, scale, w, qg, kg, cos, sin, lay, tm=512):
    M, D = x.shape
    N = w.shape[1]
    tm = _row_tile(lay, tm)
    attn_w = N_ATTN_HEADS * HEAD_DIM
    kv_w = N_KV_HEADS * HEAD_DIM
    sw = math.gcd(math.gcd(attn_w, kv_w), MXU_TILE)
    seg = _seg_index(lay, tm)
    vec = lambda: pl.BlockSpec((None, 1, D), lambda i: (seg(i), 0, 0))
    return pl.pallas_call(
        functools.partial(_attn_in_kernel, sw, attn_w // sw, (attn_w + kv_w) // sw),
        grid=(M // tm,),
        in_specs=[
            pl.BlockSpec((tm, D), lambda i: (i, 0)),
            pl.BlockSpec((1, D), lambda i: (0, 0)),
            vec(), vec(),
            pl.BlockSpec((D, N), lambda i: (0, 0)),
            pl.BlockSpec((1, HEAD_DIM), lambda i: (0, 0)),
            pl.BlockSpec((1, HEAD_DIM), lambda i: (0, 0)),
            pl.BlockSpec((tm, HEAD_DIM), lambda i: (i, 0)),
            pl.BlockSpec((tm, HEAD_DIM), lambda i: (i, 0)),
        ],
        out_specs=pl.BlockSpec((tm, N), lambda i: (i, 0)),
        out_shape=jax.ShapeDtypeStruct((M, N), _BF16),
        compiler_params=_cparams("parallel"),
        name="attn_in_proj",
    )(x, g, shift, scale, w, qg, kg, cos, sin)


def _attn_kernel(n_lat_tiles, g_per_kv, kb, q_ref, kc_ref, vc_ref, kl_ref, vl_ref, o_ref):
    qi = pl.program_id(2)
    tq = q_ref.shape[0]
    q = jnp.concatenate([q_ref[:, g * HEAD_DIM:(g + 1) * HEAD_DIM] for g in range(g_per_kv)], axis=0)

    def finish(acc, l):
        o = acc / l
        for g in range(g_per_kv):
            o_ref[:, g * HEAD_DIM:(g + 1) * HEAD_DIM] = o[g * tq:(g + 1) * tq].astype(o_ref.dtype)

    s_c = _dot_nt(q, kc_ref[...])
    m = jnp.max(s_c, axis=-1, keepdims=True)
    p = jnp.exp(s_c - m)
    l = jnp.sum(p, axis=-1, keepdims=True)
    acc = _dot(p.astype(_BF16), vc_ref[...])

    @pl.when(qi < n_lat_tiles)
    def _():
        n_blocks = kl_ref.shape[0] // kb
        mj, lj, accj = m, l, acc
        s_next = _dot_nt(q, kl_ref[0:kb, :])
        for j in range(n_blocks):
            s = s_next
            if j + 1 < n_blocks:
                s_next = _dot_nt(q, kl_ref[(j + 1) * kb:(j + 2) * kb, :])
            m_new = jnp.maximum(mj, jnp.max(s, axis=-1, keepdims=True))
            alpha = jnp.exp(mj - m_new)
            pj = jnp.exp(s - m_new)
            lj = lj * alpha + jnp.sum(pj, axis=-1, keepdims=True)
            accj = accj * alpha + _dot(pj.astype(_BF16), vl_ref[j * kb:(j + 1) * kb, :])
            mj = m_new
        finish(accj, lj)

    @pl.when(qi >= n_lat_tiles)
    def _():
        finish(acc, l)


def _attention(u, lay, tq=256, kb=1024):
    M = u.shape[0]
    g_per_kv = N_ATTN_HEADS // N_KV_HEADS
    qw = g_per_kv * HEAD_DIM
    attn_w = N_ATTN_HEADS * HEAD_DIM
    tq = _pick(lay.nc, tq)
    nlt, nct = lay.n // tq, lay.nc // tq
    k_blk = attn_w // HEAD_DIM
    v_blk = k_blk + N_KV_HEADS
    ctx_blk0 = lay.lat_rows // lay.nc

    def q_map(b, kv, qi):
        row = jnp.where(qi < nlt, b * nlt + qi, lay.lat_rows // tq + b * nct + (qi - nlt))
        return (row, kv)

    return pl.pallas_call(
        functools.partial(_attn_kernel, nlt, g_per_kv, _pick(lay.n, kb)),
        grid=(lay.B, N_KV_HEADS, nlt + nct),
        in_specs=[
            pl.BlockSpec((tq, qw), q_map),
            pl.BlockSpec((lay.nc, HEAD_DIM), lambda b, kv, qi: (ctx_blk0 + b, k_blk + kv)),
            pl.BlockSpec((lay.nc, HEAD_DIM), lambda b, kv, qi: (ctx_blk0 + b, v_blk + kv)),
            pl.BlockSpec((lay.n, HEAD_DIM), lambda b, kv, qi: (b, k_blk + kv)),
            pl.BlockSpec((lay.n, HEAD_DIM), lambda b, kv, qi: (b, v_blk + kv)),
        ],
        out_specs=pl.BlockSpec((tq, qw), q_map),
        out_shape=jax.ShapeDtypeStruct((M, attn_w), _BF16),
        compiler_params=_cparams("parallel", "parallel", "arbitrary"),
        name="gqa_attention",
    )(u, u, u, u, u)


def _dft_kernel(f_ref, cc_ref, sc_ref, cn_ref, sn_ref, o_ref, a_ref, b_ref):
    @pl.when(pl.program_id(1) == 0)
    def _():
        f = f_ref[...]
        a_ref[...] = _dot(f, cc_ref[...]).astype(_BF16)
        b_ref[...] = _dot(f, sc_ref[...]).astype(_BF16)

    o_ref[...] = (_dot(cn_ref[...], a_ref[...]) - _dot(sn_ref[...], b_ref[...])).astype(o_ref.dtype)


def _cos_sin_matrix(m, split=64):
    j = jnp.arange(m, dtype=jnp.int32)[:, None]
    w = 2.0 * math.pi / m

    def cs(k):
        ang = w * ((j * k[None, :]) % m).astype(_F32)
        return jnp.cos(ang), jnp.sin(ang)

    if m % split or m <= 8 * split:
        return cs(jnp.arange(m, dtype=jnp.int32))
    ca, sa = cs(split * jnp.arange(m // split, dtype=jnp.int32))
    cb, sb = cs(jnp.arange(split, dtype=jnp.int32))
    cos = ca[:, :, None] * cb[:, None, :] - sa[:, :, None] * sb[:, None, :]
    sin = sa[:, :, None] * cb[:, None, :] + ca[:, :, None] * sb[:, None, :]
    return cos.reshape(m, m), sin.reshape(m, m)


def _dft_tables(n, group, n_groups):
    cn, sn = _cos_sin_matrix(n)
    scale = 1.0 / math.sqrt(n * group)
    cg, sg = _cos_sin_matrix(group)
    eye = jnp.eye(n_groups, dtype=_F32)
    return ((cn * scale).astype(_BF16), (sn * scale).astype(_BF16),
            jnp.kron(eye, cg).astype(_BF16), jnp.kron(eye, sg).astype(_BF16))


def _fourier_mix(u, lay, fw, ctx=False, tm=512):
    M, N = u.shape
    n = lay.nc if ctx else lay.n
    tm = _pick(n, tm)
    nt = n // tm
    cn, sn, cc, sc = _dft_tables(n, fw // N_FOURIER_GROUPS, N_FOURIER_GROUPS)
    f_blk = N // fw - 1
    seg0 = lay.lat_rows // n if ctx else 0
    return pl.pallas_call(
        _dft_kernel,
        grid=(lay.B, nt),
        in_specs=[
            pl.BlockSpec((n, fw), lambda b, i: (seg0 + b, f_blk)),
            pl.BlockSpec((fw, fw), lambda b, i: (0, 0)),
            pl.BlockSpec((fw, fw), lambda b, i: (0, 0)),
            pl.BlockSpec((tm, n), lambda b, i: (i, 0)),
            pl.BlockSpec((tm, n), lambda b, i: (i, 0)),
        ],
        out_specs=pl.BlockSpec((tm, fw), lambda b, i: (b * nt + i, 0)),
        out_shape=jax.ShapeDtypeStruct((lay.B * n, fw), _BF16),
        scratch_shapes=[pltpu.VMEM((n, fw), _BF16), pltpu.VMEM((n, fw), _BF16)],
        compiler_params=_cparams("parallel", "arbitrary"),
        name="fourier_mix_ctx" if ctx else "fourier_mix_lat",
    )(u, cc, sc, cn, sn)


def _out_proj_kernel(n_lhs, *refs):
    lhs = refs[:n_lhs]
    ws = refs[n_lhs:2 * n_lhs]
    x_ref, gate_ref, o_ref = refs[2 * n_lhs], refs[2 * n_lhs + 1], refs[2 * n_lhs + 2]
    acc = _dot(lhs[0][...], ws[0][...])
    for a, w in zip(lhs[1:], ws[1:]):
        acc = acc + _dot(a[...], w[...])
    o_ref[...] = x_ref[...] + gate_ref[...] * acc


def _out_proj(lhs, ws, x, gate, lay, tm=512):
    M, D = x.shape
    tm = _row_tile(lay, tm)
    seg = _seg_index(lay, tm)
    in_specs = [pl.BlockSpec((tm, a.shape[1]), lambda i: (i, 0)) for a in lhs]
    in_specs += [pl.BlockSpec(w.shape, lambda i: (0, 0)) for w in ws]
    in_specs += [pl.BlockSpec((tm, D), lambda i: (i, 0)), pl.BlockSpec((None, 1, D), lambda i: (seg(i), 0, 0))]
    return pl.pallas_call(
        functools.partial(_out_proj_kernel, len(lhs)),
        grid=(M // tm,),
        in_specs=in_specs,
        out_specs=pl.BlockSpec((tm, D), lambda i: (i, 0)),
        out_shape=jax.ShapeDtypeStruct((M, D), _F32),
        compiler_params=_cparams("parallel"),
        name="out_proj_residual",
    )(*lhs, *ws, x, gate)


def _ffn_kernel(n_lat_tiles, n_ctx_tiles, tiles_per_seg, nc, tm, halo, n_sub, final_norm, x_ref, xp_ref, xn_ref,
                g_ref, sh_ref, sc_ref, gate_ref, wg_ref, wv_ref, cwg_ref, cwv_ref, cbg_ref, cbv_ref, wd_ref, fg_ref,
                o_ref, h_ref, acc_ref, u_ref):
    i, j = pl.program_id(0), pl.program_id(1)
    ext = tm + 2 * halo
    is_lat = i < n_lat_tiles

    @pl.when(j == 0)
    def _():
        g, sh, sc = g_ref[...], sh_ref[...], sc_ref[...]
        pos = lax.rem(i, tiles_per_seg)
        keep_prev = jnp.logical_and(is_lat, pos != 0)
        keep_next = jnp.logical_and(is_lat, pos != tiles_per_seg - 1)
        hp = _normmod(xp_ref[...], g, sh, sc)
        hn = _normmod(xn_ref[...], g, sh, sc)
        h_ref[0:halo, :] = jnp.where(keep_prev, hp, 0.0).astype(_BF16)
        h_ref[halo:halo + tm, :] = _normmod(x_ref[...], g, sh, sc).astype(_BF16)
        h_ref[halo + tm:ext, :] = jnp.where(keep_next, hn, 0.0).astype(_BF16)
        acc_ref[...] = jnp.zeros_like(acc_ref)

    def step(inner_boundaries):
        h = h_ref[...]
        tf = wg_ref.shape[1]
        ts = tf // n_sub
        subs = [slice(s * ts, (s + 1) * ts) for s in range(n_sub)]

        def conv(u_ref, cw, cb):
            up = u_ref[halo - 1:halo - 1 + tm, :]
            un = u_ref[halo + 1:halo + 1 + tm, :]
            if inner_boundaries:
                t = lax.rem(lax.broadcasted_iota(jnp.int32, up.shape, 0), nc)
                up = jnp.where(t != 0, up, 0.0)
                un = jnp.where(t != nc - 1, un, 0.0)
            return up * cw[0:1] + u_ref[halo:halo + tm, :] * cw[1:2] + un * cw[2:3] + cb

        def up_proj(s):
            u_ref[s, 0] = _dot(h, wg_ref[:, subs[s]])
            u_ref[s, 1] = _dot(h, wv_ref[:, subs[s]])

        up_proj(0)
        for s, sl in enumerate(subs):
            if s + 1 < n_sub:
                up_proj(s + 1)
            gt = conv(u_ref.at[s, 0], cwg_ref[:, sl], cbg_ref[:, sl])
            vl = conv(u_ref.at[s, 1], cwv_ref[:, sl], cbv_ref[:, sl])
            act = (gt * _sigmoid(gt) * vl).astype(_BF16)
            acc_ref[...] += _dot(act, wd_ref[sl, :])

    if n_ctx_tiles == 0:
        step(False)
    else:
        pl.when(is_lat)(functools.partial(step, False))
        pl.when(jnp.logical_not(is_lat))(functools.partial(step, tm > nc))

    @pl.when(j == pl.num_programs(1) - 1)
    def _():
        y = x_ref[...] + gate_ref[...] * acc_ref[...]
        if final_norm:
            y = y * lax.rsqrt(jnp.mean(y * y, axis=-1, keepdims=True) + NORM_EPS) * fg_ref[...]
        o_ref[...] = y


def _ffn(x, g, shift, scale, gate, w_up, conv_w, conv_b, w_down, final_g, lay, n_rows, final_norm, tm=512, tf=512,
         sub=MXU_TILE):
    M, D = x.shape
    F = w_down.shape[0]
    tm = _row_tile(lay, tm)
    assert tm % lay.nc == 0, "context tiles must hold whole segments"
    tiles_per_seg = lay.n // tm
    n_lat_tiles = lay.lat_rows // tm
    n_tiles = n_rows // tm
    tf = _pick(F, tf)
    nf = F // tf
    halo = BF16_ROWS
    per = tm // halo
    seg = _seg_index(lay, tm)
    vec = lambda: pl.BlockSpec((None, 1, D), lambda i, j: (seg(i), 0, 0))
    n_sub = tf // _pick(tf, sub)
    return pl.pallas_call(
        functools.partial(_ffn_kernel, n_lat_tiles, n_tiles - n_lat_tiles, tiles_per_seg, lay.nc, tm, halo, n_sub,
                          final_norm),
        grid=(n_tiles, nf),
        in_specs=[
            pl.BlockSpec((tm, D), lambda i, j: (i, 0)),
            pl.BlockSpec((halo, D), lambda i, j: (jnp.maximum(i * per - 1, 0), 0)),
            pl.BlockSpec((halo, D), lambda i, j: (jnp.minimum((i + 1) * per, M // halo - 1), 0)),
            pl.BlockSpec((1, D), lambda i, j: (0, 0)),
            vec(), vec(), vec(),
            pl.BlockSpec((D, tf), lambda i, j: (0, j)),
            pl.BlockSpec((D, tf), lambda i, j: (0, nf + j)),
            pl.BlockSpec((CONV_W, tf), lambda i, j: (0, j)),
            pl.BlockSpec((CONV_W, tf), lambda i, j: (0, nf + j)),
            pl.BlockSpec((1, tf), lambda i, j: (0, j)),
            pl.BlockSpec((1, tf), lambda i, j: (0, nf + j)),
            pl.BlockSpec((tf, D), lambda i, j: (j, 0)),
            pl.BlockSpec((1, D), lambda i, j: (0, 0)),
        ],
        out_specs=pl.BlockSpec((tm, D), lambda i, j: (i, 0)),
        out_shape=jax.ShapeDtypeStruct((n_rows, D), _F32),
        scratch_shapes=[pltpu.VMEM((tm + 2 * halo, D), _BF16), pltpu.VMEM((tm, D), _F32),
                        pltpu.VMEM((n_sub, 2, tm + 2 * halo, tf // n_sub), _F32)],
        compiler_params=_cparams("parallel", "arbitrary"),
        name="conv_gated_ffn",
    )(x, x, x, g, shift, scale, gate, w_up, w_up, conv_w, conv_w, conv_b, conv_b, w_down, final_g.reshape(1, D))


def _shift_mix_kernel(lay, tm, halo, full, loras, x_ref, xp_ref, xn_ref, g_ref, sh_ref, sc_ref, mu_ref, *refs):
    w_refs = refs[:len(loras)]
    o_refs = refs[len(loras):]
    i = pl.program_id(0)
    g, sh, sc = g_ref[...], sh_ref[...], sc_ref[...]
    h = _normmod(x_ref[...], g, sh, sc)
    hp = _normmod(xp_ref[...], g, sh, sc)
    hn = _normmod(xn_ref[...], g, sh, sc)
    ext = jnp.concatenate([hp, h, hn], axis=0)
    n_ext = tm + 2 * halo
    has_prev, has_next = _seg_masks(i, tm, h.shape, lay)
    prev = jnp.where(has_prev, pltpu.roll(ext, 1, 0)[halo:halo + tm], 0.0)
    nxt = jnp.where(has_next, pltpu.roll(ext, n_ext - 1, 0)[halo:halo + tm], 0.0)
    dx = 0.5 * (prev + nxt) - h
    mu = mu_ref[...]
    mixes = {}

    def mix(m):
        if m not in mixes:
            mixes[m] = (h + dx * mu[m:m + 1]).astype(_BF16)
        return mixes[m]

    for m, o_ref in zip(full, o_refs):
        o_ref[...] = mix(m)
    for (m, act), w_ref, o_ref in zip(loras, w_refs, o_refs[len(full):]):
        t = _dot(mix(m), w_ref[...])
        if act == "tanh":
            t = jnp.tanh(t)
        elif act == "sigmoid":
            t = _sigmoid(t)
        o_ref[...] = t.astype(o_ref.dtype)


def _shift_mix(x, g, shift, scale, mu, full, loras, lay, tm=256):
    M, D = x.shape
    tm = _row_tile(lay, tm)
    halo = 8
    per = tm // halo
    seg = _seg_index(lay, tm)
    vec = lambda: pl.BlockSpec((None, 1, D), lambda i: (seg(i), 0, 0))
    w1s = [w for _, w, _ in loras]
    outs = pl.pallas_call(
        functools.partial(_shift_mix_kernel, lay, tm, halo, tuple(full), tuple((m, act) for m, _, act in loras)),
        grid=(M // tm,),
        in_specs=[
            pl.BlockSpec((tm, D), lambda i: (i, 0)),
            pl.BlockSpec((halo, D), lambda i: (jnp.maximum(i * per - 1, 0), 0)),
            pl.BlockSpec((halo, D), lambda i: (jnp.minimum((i + 1) * per, M // halo - 1), 0)),
            pl.BlockSpec((1, D), lambda i: (0, 0)),
            vec(), vec(),
            pl.BlockSpec(mu.shape, lambda i: (0, 0)),
        ] + [pl.BlockSpec(w.shape, lambda i: (0, 0)) for w in w1s],
        out_specs=[pl.BlockSpec((tm, D), lambda i: (i, 0)) for _ in full]
        + [pl.BlockSpec((tm, w.shape[1]), lambda i: (i, 0)) for w in w1s],
        out_shape=[jax.ShapeDtypeStruct((M, D), _BF16) for _ in full]
        + [jax.ShapeDtypeStruct((M, w.shape[1]), _BF16) for w in w1s],
        compiler_params=_cparams("parallel"),
        name="rwkv_shift_mix",
    )(x, x, x, g, shift, scale, mu, *w1s)
    return outs[:len(full)], outs[len(full):]


def _wkv_kernel(reverse, final, C, LW, r_ref, k_ref, v_ref, wp_ref, ap_ref, kk_ref, ka_ref, rk_ref, *refs):
    if final:
        y0_ref, b0_ref, g_ref, lng_ref, lnb_ref, o_ref, s_ref = refs
    else:
        y_ref, bon_ref, s_ref = refs
    j = pl.program_id(2)

    @pl.when(j == 0)
    def _():
        s_ref[...] = jnp.zeros_like(s_ref)

    hpg = WKV_GROUP // RWKV_HEAD
    groups = range(LW // WKV_GROUP)
    ng = len(groups)
    sls = [slice(g * WKV_GROUP, (g + 1) * WKV_GROUP) for g in groups]
    row = lax.broadcasted_iota(jnp.int32, (C, C), 0)
    col = lax.broadcasted_iota(jnp.int32, (C, C), 1)
    tri = jnp.where((col >= row) if reverse else (col <= row), 1.0, 0.0).astype(_BF16)
    rc = lax.broadcasted_iota(jnp.int32, (C, WKV_GROUP), 0)
    cc = lax.rem(lax.broadcasted_iota(jnp.int32, (C, WKV_GROUP), 1), C)
    strict = (cc > rc) if reverse else (cc < rc)
    incl = (cc >= rc) if reverse else (cc <= rc)
    eye_cat = jnp.where(cc == rc, 1.0, 0.0).astype(_F32)
    br = lax.broadcasted_iota(jnp.int32, (WKV_GROUP, WKV_GROUP), 0) // RWKV_HEAD
    bc = lax.broadcasted_iota(jnp.int32, (WKV_GROUP, WKV_GROUP), 1) // RWKV_HEAD
    same_head = br == bc
    head_ones = jnp.where(same_head, 1.0, 0.0).astype(_BF16)

    def bd(x):
        return jnp.where(same_head, jnp.concatenate([x] * hpg, axis=0), jnp.zeros((), x.dtype))

    cast = lambda x: x.astype(_BF16)

    n_chunks = r_ref.shape[0] // C
    units = [(c, g) for c in range(n_chunks) for g in groups]
    nu = len(units)
    tile = lambda ref, u: ref[u[0] * C:(u[0] + 1) * C, sls[u[1]]]
    lane = lambda ref, u: ref[:, sls[u[1]]]

    v = [tile(v_ref, u) for u in units]
    lw = []
    for u in units:
        wp = tile(wp_ref, u)
        sp = jnp.maximum(-wp, 0.0) + jnp.log(1.0 + jnp.exp(-jnp.abs(wp)))
        lw.append(-jnp.exp(-sp - 0.5))
    cl = [_ones_dot_left(tri, x, 3) for x in lw]
    total = [c[0:1] if reverse else c[C - 1:C] for c in cl]

    a = [_sigmoid(tile(ap_ref, u)) for u in units]
    kk = [tile(k_ref, u) * lane(kk_ref, u) for u in units]
    kd = [tile(k_ref, u) * (1.0 + (a[i] - 1.0) * lane(ka_ref, u)) for i, u in enumerate(units)]
    rkd = [tile(r_ref, u) * kd[i] * lane(rk_ref, u) for i, u in enumerate(units)]
    sums = _ones_dot_rows(head_ones, [x * x for x in kk] + rkd, 2)
    kk = [x / jnp.maximum(jnp.sqrt(n), 1e-12) for x, n in zip(kk, sums[:nu])]
    bonus = [s * x for s, x in zip(sums[nu:], v)]
    b = [kk[i] * a[i] for i in range(nu)]

    AR = [jnp.concatenate([cast(-kk[i] * jnp.exp(cl[i] - lw[i])), cast(tile(r_ref, u) * jnp.exp(cl[i]))], axis=0)
          for i, u in enumerate(units)]
    BKbd, BKp = [], []
    for i in range(nu):
        e_neg = jnp.exp(-cl[i])
        e_rem = jnp.exp(total[i] - cl[i])
        BKbd.append(jnp.concatenate([bd(cast(b[i] * e_neg)), bd(cast(kd[i] * e_neg))], axis=0))
        BKp.append(cast(jnp.concatenate([b[i] * e_rem, kd[i] * e_rem], axis=0)))
    vb = [cast(x) for x in v]
    Vbd = [bd(x) for x in vb]

    M = [_dot_nt(AR[i], BKbd[i]) for i in range(nu)]
    Lab = [jnp.where(strict, m[:C, :WKV_GROUP], 0.0) for m in M]
    LMk = [cast(jnp.concatenate([jnp.where(strict, m[:C, WKV_GROUP:], 0.0),
                                 jnp.where(incl, m[C:, WKV_GROUP:], 0.0)], axis=0)) for m in M]
    Mrb = [cast(jnp.where(incl, m[C:, :WKV_GROUP], 0.0)) for m in M]
    LMkV = [_dot(LMk[i], Vbd[i]) for i in range(nu)]

    Pw = [cast(x) for x in Lab]
    T = [eye_cat + x for x in Lab]
    Pw = [cast(_dot(x, bd(x))) for x in Pw]
    for _ in range(int(math.log2(C)) - 2):
        both = [_dot(jnp.concatenate([cast(t), x], axis=0), bd(x)) for t, x in zip(T, Pw)]
        T = [t + bt[:C] for t, bt in zip(T, both)]
        Pw = [cast(bt[C:]) for bt in both]
    Tb = [cast(t + _dot(cast(t), bd(x))) for t, x in zip(T, Pw)]

    S = [s_ref[g] for g in groups]
    y = [None] * nu
    for c in (reversed(range(n_chunks)) if reverse else range(n_chunks)):
        us = [c * ng + g for g in groups]
        ARS = [_dot_nt(AR[i], cast(S[g])) for g, i in enumerate(us)]
        X = [ARS[g][:C] + LMkV[i][:C] for g, i in enumerate(us)]
        Z = [cast(_dot(Tb[i], bd(cast(X[g])))) for g, i in enumerate(us)]
        for g, i in enumerate(us):
            y[i] = ARS[g][C:] + LMkV[i][C:] + _dot(Mrb[i], bd(Z[g]))
        dS = [_dot_tn(jnp.concatenate([Z[g], vb[i]], axis=0), BKp[i]) for g, i in enumerate(us)]
        S = [S[g] * jnp.exp(total[i]) + jnp.where(same_head, dS[g], 0.0) for g, i in enumerate(us)]
    s_ref[...] = jnp.stack(S, axis=0)

    def assemble(xs):
        return jnp.concatenate([jnp.concatenate(xs[c * ng:(c + 1) * ng], axis=1) for c in range(n_chunks)], axis=0)

    if final:
        inv = 1.0 / RWKV_HEAD
        wkv = [y[i] + tile(y0_ref, u) for i, u in enumerate(units)]
        cen = [x - s * inv for x, s in zip(wkv, _ones_dot_rows(head_ones, wkv, 3))]
        var = [s * inv for s in _ones_dot_rows(head_ones, [x * x for x in cen], 2)]
        outs = []
        for i, u in enumerate(units):
            normed = cen[i] * lax.rsqrt(var[i] + GN_EPS) * lane(lng_ref, u) + lane(lnb_ref, u)
            outs.append(((normed + bonus[i] + tile(b0_ref, u)) * tile(g_ref, u)).astype(o_ref.dtype))
        o_ref[...] = assemble(outs)
    else:
        y_ref[...] = assemble(y)
        bon_ref[...] = assemble(bonus)


def _wkv(r, k, v, wpre, apre, k_k, k_a, r_k, lay, reverse, final_args=None, lw=2048, chunks_per_step=2):
    M, D = r.shape
    C = WKV_CHUNK
    rows = chunks_per_step * C
    assert C == RWKV_HEAD and lay.nc % rows == 0 and lay.n % rows == 0 and D % WKV_GROUP == 0
    lw = _pick(D, lw)
    assert lw % WKV_GROUP == 0
    ncc, ncl = lay.nc // rows, lay.n // rows
    ctx_blk0 = lay.lat_rows // rows

    def row_blk(b, j):
        if reverse:
            return jnp.where(j < ncc, ctx_blk0 + b * ncc + (ncc - 1 - j), b * ncl + (ncl - 1 - (j - ncc)))
        return jnp.where(j < ncc, ctx_blk0 + b * ncc + j, b * ncl + (j - ncc))

    tile = lambda: pl.BlockSpec((rows, lw), lambda b, h, j: (row_blk(b, j), h))
    vec = lambda: pl.BlockSpec((1, lw), lambda b, h, j: (0, h))
    in_specs = [tile() for _ in range(5)] + [vec() for _ in range(3)]
    args = [r, k, v, wpre, apre, k_k.reshape(1, D), k_a.reshape(1, D), r_k.reshape(1, D)]
    final = final_args is not None
    if final:
        y0, b0, g, lng, lnb = final_args
        in_specs += [tile(), tile(), tile(), vec(), vec()]
        args += [y0, b0, g, lng.reshape(1, D), lnb.reshape(1, D)]
        out_specs = tile()
        out_shape = jax.ShapeDtypeStruct((M, D), _BF16)
    else:
        out_specs = [tile(), tile()]
        out_shape = [jax.ShapeDtypeStruct((M, D), _F32), jax.ShapeDtypeStruct((M, D), _F32)]
    return pl.pallas_call(
        functools.partial(_wkv_kernel, reverse, final, C, lw),
        grid=(lay.B, D // lw, ncc + ncl),
        in_specs=in_specs,
        out_specs=out_specs,
        out_shape=out_shape,
        scratch_shapes=[pltpu.VMEM((lw // WKV_GROUP, WKV_GROUP, WKV_GROUP), _F32)],
        compiler_params=_cparams("parallel", "parallel", "arbitrary"),
        name="wkv7_scan_rev" if reverse else "wkv7_scan_fwd",
    )(*args)


def _rope_tables(lay):
    n = lay.n
    t = jnp.arange(n, dtype=jnp.int32)
    row = (t // GRID_W).astype(_F32)
    col = (t % GRID_W).astype(_F32)
    axis_dim = HEAD_DIM // 2
    inv_freq = ROPE_THETA ** (-jnp.arange(0, axis_dim, 2, dtype=_F32) / axis_dim)
    ang = jnp.concatenate([row[:, None] * inv_freq, col[:, None] * inv_freq], axis=-1)
    cos, sin = jnp.cos(ang), jnp.sin(ang)
    cos_l = jnp.concatenate([cos, cos], axis=-1)
    sin_l = jnp.concatenate([-sin, sin], axis=-1)
    n_ctx_rows = lay.B * lay.nc
    cos_f = jnp.concatenate([jnp.tile(cos_l, (lay.B, 1)), jnp.ones((n_ctx_rows, HEAD_DIM), _F32)], axis=0)
    sin_f = jnp.concatenate([jnp.tile(sin_l, (lay.B, 1)), jnp.zeros((n_ctx_rows, HEAD_DIM), _F32)], axis=0)
    return cos_f, sin_f


def _deinterleave_heads(w, n_heads):
    lead = w.shape[:-1]
    w = w.reshape(*lead, n_heads, HEAD_DIM // 2, 2)
    w = jnp.swapaxes(w, -1, -2)
    return w.reshape(*lead, n_heads * HEAD_DIM)


def _pad_cols(w, mult=LANES):
    pad = (-w.shape[-1]) % mult
    return jnp.pad(w, [(0, 0)] * (w.ndim - 1) + [(0, pad)]) if pad else w


def _pad_rows(w, mult=LANES):
    pad = (-w.shape[-2]) % mult
    return jnp.pad(w, [(0, 0)] * (w.ndim - 2) + [(0, pad), (0, 0)]) if pad else w


def kernel(x, c, ctx, c_ctx, w_mod, b_mod, norm1_g, norm2_g, attn_w_in, attn_w_out, q_norm_g, k_norm_g, rwkv_mu, rwkv_w_r, rwkv_w_k, rwkv_w_v, rwkv_w_o, rwkv_decay_w0, rwkv_decay_w1, rwkv_decay_w2, rwkv_iclr_a0, rwkv_iclr_a1, rwkv_iclr_a2, rwkv_gate_g1, rwkv_gate_g2, rwkv_k_k, rwkv_k_a, rwkv_r_k, rwkv_lnx_g, rwkv_lnx_b, rwkv_vres_v0, rwkv_vres_v1, rwkv_vres_v2, ffn_w_up, ffn_conv_w, ffn_conv_b, ffn_w_down, final_norm_g):
    B, n, D = x.shape
    nc = ctx.shape[1]
    depth = w_mod.shape[0]
    lay = Layout(B, n, nc, B * n, B * (n + nc))
    attn_w = N_ATTN_HEADS * HEAD_DIM
    kv_w = N_KV_HEADS * HEAD_DIM
    fw = D - attn_w
    bf = lambda w: w.astype(_BF16)

    xs = jnp.concatenate([x.reshape(B * n, D), ctx.reshape(B * nc, D)], axis=0)

    cvec = jnp.concatenate([c, c_ctx[None, :], jnp.zeros((-(B + 1) % BF16_ROWS, D), _F32)], axis=0)
    mods = _modulation(cvec, w_mod, b_mod)[:, :B + 1]
    mods = mods.reshape(depth, B + 1, 6, 1, D)
    cos_f, sin_f = _rope_tables(lay)
    qg_scale = HEAD_DIM ** -0.5

    v_first = None
    for i in range(depth):
        last = i == depth - 1
        jl = i // 2
        m = [mods[i, :, t] for t in range(6)]
        g1 = norm1_g[i].reshape(1, D)
        if i % 2 == 0:
            w_in = bf(attn_w_in[jl])
            w_in = jnp.concatenate([
                _deinterleave_heads(w_in[:, :attn_w], N_ATTN_HEADS),
                _deinterleave_heads(w_in[:, attn_w:attn_w + kv_w], N_KV_HEADS),
                w_in[:, attn_w + kv_w:]], axis=1)
            qg = (_deinterleave_heads(q_norm_g[jl], 1) * qg_scale).reshape(1, HEAD_DIM)
            kg = _deinterleave_heads(k_norm_g[jl], 1).reshape(1, HEAD_DIM)
            u = _attn_in_proj(xs, g1, m[0], m[1], w_in, qg, kg, cos_f, sin_f, lay)
            o = _attention(u, lay)
            fm = jnp.concatenate([_fourier_mix(u, lay, fw), _fourier_mix(u, lay, fw, ctx=True)], axis=0)
            w_out = bf(attn_w_out[jl])
            xs = _out_proj([o, fm], [w_out[:attn_w], w_out[attn_w:]], xs, m[2], lay)
        else:
            loras = [(1, _lora_w1(list(rwkv_decay_w1[jl])), "tanh"), (4, _lora_w1(list(rwkv_iclr_a1[jl])), None),
                     (5, _lora_w1([rwkv_gate_g1[jl]]), "sigmoid")]
            if jl > 0:
                loras.append((3, _lora_w1([rwkv_vres_v1[jl - 1]]), None))
            (x_r, x_k, x_v), acts = _shift_mix(xs, g1, m[0], m[1], rwkv_mu[jl], (0, 2, 3), loras, lay)
            r = _linear(x_r, bf(rwkv_w_r[jl]), name="rwkv_r")
            k = _linear(x_k, bf(rwkv_w_k[jl]), name="rwkv_k")
            if jl == 0:
                v = v_first = _linear(x_v, bf(rwkv_w_v[jl]), name="rwkv_v")
            else:
                v = _linear(x_v, bf(rwkv_w_v[jl]), name="rwkv_v_vres",
                            vres=(acts[3], rwkv_vres_v2[jl - 1], rwkv_vres_v0[jl - 1], v_first))
            wpre = _lora_out(acts[0], list(rwkv_decay_w2[jl]), list(rwkv_decay_w0[jl]), name="rwkv_decay")
            apre = _lora_out(acts[1], list(rwkv_iclr_a2[jl]), list(rwkv_iclr_a0[jl]), name="rwkv_iclr")
            (g,) = _lora_out(acts[2], [rwkv_gate_g2[jl]], name="rwkv_gate")
            kk_, ka_, rk_ = rwkv_k_k[jl], rwkv_k_a[jl], rwkv_r_k[jl].reshape(D)
            y0, bon0 = _wkv(r, k, v, wpre[0], apre[0], kk_, ka_, rk_, lay, reverse=False)
            z = _wkv(r, k, v, wpre[1], apre[1], kk_, ka_, rk_, lay, reverse=True,
                     final_args=(y0, bon0, g, rwkv_lnx_g[jl], rwkv_lnx_b[jl]))
            xs = _out_proj([z], [bf(rwkv_w_o[jl])], xs, m[2], lay)
        n_rows = lay.lat_rows if last else lay.rows
        xs = _ffn(xs, norm2_g[i].reshape(1, D), m[3], m[4], m[5], bf(ffn_w_up[i]), ffn_conv_w[i],
                  ffn_conv_b[i].reshape(1, -1), bf(ffn_w_down[i]), final_norm_g, lay, n_rows, final_norm=last)
    return xs.reshape(B, n, D)
```

```python
import collections
import functools
import math

import jax
import jax.numpy as jnp
from jax import lax
from jax.experimental import pallas as pl
from jax.experimental.pallas import tpu as pltpu

HEAD_DIM = 128
N_ATTN_HEADS = 12
N_KV_HEADS = 4
GRID_W = 64
ROPE_THETA = 10000.0
N_FOURIER_GROUPS = 4
RWKV_HEAD = 64
CONV_W = 3
NORM_EPS = 1e-6
GN_EPS = 64e-5

LANES = 128
MXU_TILE = 256
BF16_ROWS = 16
VMEM_LIMIT = 56 * 1024 * 1024
WKV_CHUNK = 64
WKV_GROUP = MXU_TILE

_F32 = jnp.float32
_BF16 = jnp.bfloat16

Layout = collections.namedtuple("Layout", "B n nc lat_rows rows")


def _cparams(*sem):
    return pltpu.CompilerParams(dimension_semantics=sem, vmem_limit_bytes=VMEM_LIMIT)


def _pick(total, pref):
    t = min(pref, total)
    while total % t:
        t -= 1
    return t


def _row_tile(lay, pref):
    return _pick(lay.B * lay.nc, _pick(lay.n, pref))


def _seg_index(lay, tm):
    def f(i):
        r0 = i * tm
        return jnp.where(r0 < lay.lat_rows, r0 // lay.n, lay.B)
    return f


def _seg_masks(i, tm, shape, lay):
    t = lax.broadcasted_iota(jnp.int32, shape, 0)
    r0 = i * tm
    is_lat = r0 < lay.lat_rows
    if tm >= lay.nc:
        pc = lax.rem(t, lay.nc)
    else:
        pc = t + lax.rem(r0 - lay.lat_rows, lay.nc)
    pos = jnp.where(is_lat, t + lax.rem(r0, lay.n), pc)
    last = jnp.where(is_lat, lay.n - 1, lay.nc - 1)
    return pos != 0, pos != last


def _normmod(x, g, shift, scale):
    ms = jnp.mean(x * x, axis=-1, keepdims=True)
    return (x * lax.rsqrt(ms + NORM_EPS)) * g * (1.0 + scale) + shift


def _sigmoid(x):
    return 1.0 / (1.0 + jnp.exp(-x))


def _dot(a, b, prec=None):
    return jnp.dot(a, b, preferred_element_type=_F32, precision=prec)


def _dot_nt(a, b):
    return lax.dot_general(a, b, (((1,), (1,)), ((), ())), preferred_element_type=_F32)


def _dot_tn(a, b):
    return lax.dot_general(a, b, (((0,), (0,)), ((), ())), preferred_element_type=_F32)


def _bf16_pieces(x, parts):
    pieces = []
    rem = x
    for i in range(parts):
        piece = rem.astype(_BF16)
        pieces.append(piece)
        if i + 1 < parts:
            rem = rem - piece.astype(_F32)
    return pieces


def _ones_dot_left(ones, x, parts):
    acc = None
    for piece in _bf16_pieces(x, parts):
        d = _dot(ones, piece)
        acc = d if acc is None else acc + d
    return acc


def _ones_dot_rows(ones, xs, parts):
    rows = xs[0].shape[0]
    stacked = jnp.concatenate([p for x in xs for p in _bf16_pieces(x, parts)], axis=0)
    res = _dot(stacked, ones)
    outs = []
    for n in range(len(xs)):
        base = n * parts * rows
        acc = res[base:base + rows]
        for i in range(1, parts):
            acc = acc + res[base + i * rows:base + (i + 1) * rows]
        outs.append(acc)
    return outs


def _mod_kernel(c_ref, w_ref, b_ref, o_ref):
    c = c_ref[...]
    s = c * _sigmoid(c)
    rows = s.shape[0]
    res = _dot(jnp.concatenate(_bf16_pieces(s, 3), axis=0), w_ref[...].astype(_BF16))
    o_ref[...] = res[:rows] + res[rows:2 * rows] + res[2 * rows:] + b_ref[...]


def _modulation(cvec, w_mod, b_mod):
    depth, D, N = w_mod.shape
    rows = cvec.shape[0]
    tn = _pick(N, 512)
    return pl.pallas_call(
        _mod_kernel,
        grid=(depth, N // tn),
        in_specs=[
            pl.BlockSpec((rows, D), lambda l, j: (0, 0)),
            pl.BlockSpec((None, D, tn), lambda l, j: (l, 0, j)),
            pl.BlockSpec((None, 1, tn), lambda l, j: (l, 0, j)),
        ],
        out_specs=pl.BlockSpec((None, rows, tn), lambda l, j: (l, 0, j)),
        out_shape=jax.ShapeDtypeStruct((depth, rows, N), _F32),
        compiler_params=_cparams("parallel", "parallel"),
        name="adaln_modulation",
    )(cvec, w_mod, b_mod.reshape(depth, 1, N))


def _linear_kernel(has_vres, x_ref, w_ref, *refs):
    o_ref = refs[-1]
    acc = _dot(x_ref[...], w_ref[...])
    if has_vres:
        t_ref, w2_ref, b_ref, vf_ref = refs[:4]
        gate = _sigmoid(_dot(t_ref[...], w2_ref[...]) + b_ref[...])
        acc = acc + (vf_ref[...].astype(_F32) - acc) * gate
    o_ref[...] = acc.astype(o_ref.dtype)


def _linear(x, w, vres=None, out_dtype=_BF16, tm=512, name="linear"):
    M, K = x.shape
    N = w.shape[1]
    tm = _pick(M, tm)
    in_specs = [pl.BlockSpec((tm, K), lambda i: (i, 0)), pl.BlockSpec((K, N), lambda i: (0, 0))]
    args = [x, w]
    if vres is not None:
        t, v2, v0, v_first = vres
        w2 = _pad_rows(v2).astype(_BF16)
        R = t.shape[1]
        assert w2.shape[0] == R
        in_specs += [pl.BlockSpec((tm, R), lambda i: (i, 0)), pl.BlockSpec((R, N), lambda i: (0, 0)),
                     pl.BlockSpec((1, N), lambda i: (0, 0)), pl.BlockSpec((tm, N), lambda i: (i, 0))]
        args += [t, w2, v0.reshape(1, N), v_first]
    return pl.pallas_call(
        functools.partial(_linear_kernel, vres is not None),
        grid=(M // tm,),
        in_specs=in_specs,
        out_specs=pl.BlockSpec((tm, N), lambda i: (i, 0)),
        out_shape=jax.ShapeDtypeStruct((M, N), out_dtype),
        compiler_params=_cparams("parallel"),
        name=name,
    )(*args)


def _lora_w1(w1s):
    return jnp.concatenate([_pad_cols(w) for w in w1s], axis=1).astype(_BF16)


def _lora_out_kernel(ranks, has_bias, t_ref, *refs):
    n_out = len(ranks)
    w2_refs = refs[:n_out]
    b_refs = refs[n_out:2 * n_out] if has_bias else ()
    o_refs = refs[-n_out:]
    off = 0
    for d in range(n_out):
        y = _dot(t_ref[:, off:off + ranks[d]], w2_refs[d][...])
        if has_bias:
            y = y + b_refs[d][...]
        o_refs[d][...] = y
        off += ranks[d]


def _lora_out(t, w2s, biases=None, tm=512, name="lora_out"):
    M, R = t.shape
    N = w2s[0].shape[1]
    tm = _pick(M, tm)
    w2p = [_pad_rows(w).astype(_BF16) for w in w2s]
    ranks = tuple(w.shape[0] for w in w2p)
    assert sum(ranks) == R
    in_specs = [pl.BlockSpec((tm, R), lambda i: (i, 0))]
    in_specs += [pl.BlockSpec(w.shape, lambda i: (0, 0)) for w in w2p]
    args = [t, *w2p]
    if biases is not None:
        in_specs += [pl.BlockSpec((1, N), lambda i: (0, 0)) for _ in biases]
        args += [b.reshape(1, N) for b in biases]
    return pl.pallas_call(
        functools.partial(_lora_out_kernel, ranks, biases is not None),
        grid=(M // tm,),
        in_specs=in_specs,
        out_specs=[pl.BlockSpec((tm, N), lambda i: (i, 0)) for _ in w2p],
        out_shape=[jax.ShapeDtypeStruct((M, N), _F32) for _ in w2p],
        compiler_params=_cparams("parallel"),
        name=name,
    )(*args)


def _attn_in_kernel(sw, n_q, n_qk, x_ref, g_ref, sh_ref, sc_ref, w_ref, qg_ref, kg_ref, cos_ref, sin_ref, o_ref):
    h = _normmod(x_ref[...], g_ref[...], sh_ref[...], sc_ref[...]).astype(_BF16)
    cos, sin = cos_ref[...], sin_ref[...]
    n_blocks = w_ref.shape[1] // sw

    def finish(s, acc):
        cols = slice(s * sw, (s + 1) * sw)
        if s >= n_qk:
            o_ref[:, cols] = acc.astype(o_ref.dtype)
            return
        gain = qg_ref[...] if s < n_q else kg_ref[...]
        for hd in range(sw // HEAD_DIM):
            a = acc[:, hd * HEAD_DIM:(hd + 1) * HEAD_DIM]
            y = a * lax.rsqrt(jnp.mean(a * a, axis=-1, keepdims=True) + NORM_EPS) * gain
            y = y * cos + pltpu.roll(y, HEAD_DIM // 2, 1) * sin
            o_ref[:, s * sw + hd * HEAD_DIM:s * sw + (hd + 1) * HEAD_DIM] = y.astype(o_ref.dtype)

    acc = _dot(h, w_ref[:, 0:sw])
    for s in range(n_blocks):
        nxt = _dot(h, w_ref[:, (s + 1) * sw:(s + 2) * sw]) if s + 1 < n_blocks else None
        finish(s, acc)
        acc = nxt


def _attn_in_proj(x, g, shift, scale, w, qg, kg, cos, sin, lay, tm=512):
    M, D = x.shape
    N = w.shape[1]
    tm = _row_tile(lay, tm)
    attn_w = N_ATTN_HEADS * HEAD_DIM
    kv_w = N_KV_HEADS * HEAD_DIM
    sw = math.gcd(math.gcd(attn_w, kv_w), MXU_TILE)
    seg = _seg_index(lay, tm)
    vec = lambda: pl.BlockSpec((None, 1, D), lambda i: (seg(i), 0, 0))
    return pl.pallas_call(
        functools.partial(_attn_in_kernel, sw, attn_w // sw, (attn_w + kv_w) // sw),
        grid=(M // tm,),
        in_specs=[
            pl.BlockSpec((tm, D), lambda i: (i, 0)),
            pl.BlockSpec((1, D), lambda i: (0, 0)),
            vec(), vec(),
            pl.BlockSpec((D, N), lambda i: (0, 0)),
            pl.BlockSpec((1, HEAD_DIM), lambda i: (0, 0)),
            pl.BlockSpec((1, HEAD_DIM), lambda i: (0, 0)),
            pl.BlockSpec((tm, HEAD_DIM), lambda i: (i, 0)),
            pl.BlockSpec((tm, HEAD_DIM), lambda i: (i, 0)),
        ],
        out_specs=pl.BlockSpec((tm, N), lambda i: (i, 0)),
        out_shape=jax.ShapeDtypeStruct((M, N), _BF16),
        compiler_params=_cparams("parallel"),
        name="attn_in_proj",
    )(x, g, shift, scale, w, qg, kg, cos, sin)


def _attn_kernel(n_lat_tiles, g_per_kv, kb, q_ref, kc_ref, vc_ref, kl_ref, vl_ref, o_ref):
    qi = pl.program_id(2)
    tq = q_ref.shape[0]
    q = jnp.concatenate([q_ref[:, g * HEAD_DIM:(g + 1) * HEAD_DIM] for g in range(g_per_kv)], axis=0)

    def finish(acc, l):
        o = acc / l
        for g in range(g_per_kv):
            o_ref[:, g * HEAD_DIM:(g + 1) * HEAD_DIM] = o[g * tq:(g + 1) * tq].astype(o_ref.dtype)

    s_c = _dot_nt(q, kc_ref[...])
    m = jnp.max(s_c, axis=-1, keepdims=True)
    p = jnp.exp(s_c - m)
    l = jnp.sum(p, axis=-1, keepdims=True)
    acc = _dot(p.astype(_BF16), vc_ref[...])

    @pl.when(qi < n_lat_tiles)
    def _():
        n_blocks = kl_ref.shape[0] // kb
        mj, lj, accj = m, l, acc
        s_next = _dot_nt(q, kl_ref[0:kb, :])
        for j in range(n_blocks):
            s = s_next
            if j + 1 < n_blocks:
                s_next = _dot_nt(q, kl_ref[(j + 1) * kb:(j + 2) * kb, :])
            m_new = jnp.maximum(mj, jnp.max(s, axis=-1, keepdims=True))
            alpha = jnp.exp(mj - m_new)
            pj = jnp.exp(s - m_new)
            lj = lj * alpha + jnp.sum(pj, axis=-1, keepdims=True)
            accj = accj * alpha + _dot(pj.astype(_BF16), vl_ref[j * kb:(j + 1) * kb, :])
            mj = m_new
        finish(accj, lj)

    @pl.when(qi >= n_lat_tiles)
    def _():
        finish(acc, l)


def _attention(u, lay, tq=256, kb=1024):
    M = u.shape[0]
    g_per_kv = N_ATTN_HEADS // N_KV_HEADS
    qw = g_per_kv * HEAD_DIM
    attn_w = N_ATTN_HEADS * HEAD_DIM
    tq = _pick(lay.nc, tq)
    nlt, nct = lay.n // tq, lay.nc // tq
    k_blk = attn_w // HEAD_DIM
    v_blk = k_blk + N_KV_HEADS
    ctx_blk0 = lay.lat_rows // lay.nc

    def q_map(b, kv, qi):
        row = jnp.where(qi < nlt, b * nlt + qi, lay.lat_rows // tq + b * nct + (qi - nlt))
        return (row, kv)

    return pl.pallas_call(
        functools.partial(_attn_kernel, nlt, g_per_kv, _pick(lay.n, kb)),
        grid=(lay.B, N_KV_HEADS, nlt + nct),
        in_specs=[
            pl.BlockSpec((tq, qw), q_map),
            pl.BlockSpec((lay.nc, HEAD_DIM), lambda b, kv, qi: (ctx_blk0 + b, k_blk + kv)),
            pl.BlockSpec((lay.nc, HEAD_DIM), lambda b, kv, qi: (ctx_blk0 + b, v_blk + kv)),
            pl.BlockSpec((lay.n, HEAD_DIM), lambda b, kv, qi: (b, k_blk + kv)),
            pl.BlockSpec((lay.n, HEAD_DIM), lambda b, kv, qi: (b, v_blk + kv)),
        ],
        out_specs=pl.BlockSpec((tq, qw), q_map),
        out_shape=jax.ShapeDtypeStruct((M, attn_w), _BF16),
        compiler_params=_cparams("parallel", "parallel", "arbitrary"),
        name="gqa_attention",
    )(u, u, u, u, u)


def _dft_kernel(f_ref, cc_ref, sc_ref, cn_ref, sn_ref, o_ref, a_ref, b_ref):
    @pl.when(pl.program_id(1) == 0)
    def _():
        f = f_ref[...]
        a_ref[...] = _dot(f, cc_ref[...]).astype(_BF16)
        b_ref[...] = _dot(f, sc_ref[...]).astype(_BF16)

    o_ref[...] = (_dot(cn_ref[...], a_ref[...]) - _dot(sn_ref[...], b_ref[...])).astype(o_ref.dtype)


def _cos_sin_matrix(m, split=64):
    j = jnp.arange(m, dtype=jnp.int32)[:, None]
    w = 2.0 * math.pi / m

    def cs(k):
        ang = w * ((j * k[None, :]) % m).astype(_F32)
        return jnp.cos(ang), jnp.sin(ang)

    if m % split or m <= 8 * split:
        return cs(jnp.arange(m, dtype=jnp.int32))
    ca, sa = cs(split * jnp.arange(m // split, dtype=jnp.int32))
    cb, sb = cs(jnp.arange(split, dtype=jnp.int32))
    cos = ca[:, :, None] * cb[:, None, :] - sa[:, :, None] * sb[:, None, :]
    sin = sa[:, :, None] * cb[:, None, :] + ca[:, :, None] * sb[:, None, :]
    return cos.reshape(m, m), sin.reshape(m, m)


def _dft_tables(n, group, n_groups):
    cn, sn = _cos_sin_matrix(n)
    scale = 1.0 / math.sqrt(n * group)
    cg, sg = _cos_sin_matrix(group)
    eye = jnp.eye(n_groups, dtype=_F32)
    return ((cn * scale).astype(_BF16), (sn * scale).astype(_BF16),
            jnp.kron(eye, cg).astype(_BF16), jnp.kron(eye, sg).astype(_BF16))


def _fourier_mix(u, lay, fw, ctx=False, tm=512):
    M, N = u.shape
    n = lay.nc if ctx else lay.n
    tm = _pick(n, tm)
    nt = n // tm
    cn, sn, cc, sc = _dft_tables(n, fw // N_FOURIER_GROUPS, N_FOURIER_GROUPS)
    f_blk = N // fw - 1
    seg0 = lay.lat_rows // n if ctx else 0
    return pl.pallas_call(
        _dft_kernel,
        grid=(lay.B, nt),
        in_specs=[
            pl.BlockSpec((n, fw), lambda b, i: (seg0 + b, f_blk)),
            pl.BlockSpec((fw, fw), lambda b, i: (0, 0)),
            pl.BlockSpec((fw, fw), lambda b, i: (0, 0)),
            pl.BlockSpec((tm, n), lambda b, i: (i, 0)),
            pl.BlockSpec((tm, n), lambda b, i: (i, 0)),
        ],
        out_specs=pl.BlockSpec((tm, fw), lambda b, i: (b * nt + i, 0)),
        out_shape=jax.ShapeDtypeStruct((lay.B * n, fw), _BF16),
        scratch_shapes=[pltpu.VMEM((n, fw), _BF16), pltpu.VMEM((n, fw), _BF16)],
        compiler_params=_cparams("parallel", "arbitrary"),
        name="fourier_mix_ctx" if ctx else "fourier_mix_lat",
    )(u, cc, sc, cn, sn)


def _out_proj_kernel(n_lhs, *refs):
    lhs = refs[:n_lhs]
    ws = refs[n_lhs:2 * n_lhs]
    x_ref, gate_ref, o_ref = refs[2 * n_lhs], refs[2 * n_lhs + 1], refs[2 * n_lhs + 2]
    acc = _dot(lhs[0][...], ws[0][...])
    for a, w in zip(lhs[1:], ws[1:]):
        acc = acc + _dot(a[...], w[...])
    o_ref[...] = x_ref[...] + gate_ref[...] * acc


def _out_proj(lhs, ws, x, gate, lay, tm=512):
    M, D = x.shape
    tm = _row_tile(lay, tm)
    seg = _seg_index(lay, tm)
    in_specs = [pl.BlockSpec((tm, a.shape[1]), lambda i: (i, 0)) for a in lhs]
    in_specs += [pl.BlockSpec(w.shape, lambda i: (0, 0)) for w in ws]
    in_specs += [pl.BlockSpec((tm, D), lambda i: (i, 0)), pl.BlockSpec((None, 1, D), lambda i: (seg(i), 0, 0))]
    return pl.pallas_call(
        functools.partial(_out_proj_kernel, len(lhs)),
        grid=(M // tm,),
        in_specs=in_specs,
        out_specs=pl.BlockSpec((tm, D), lambda i: (i, 0)),
        out_shape=jax.ShapeDtypeStruct((M, D), _F32),
        compiler_params=_cparams("parallel"),
        name="out_proj_residual",
    )(*lhs, *ws, x, gate)


def _ffn_kernel(n_lat_tiles, n_ctx_tiles, tiles_per_seg, nc, tm, halo, n_sub, final_norm, x_ref, xp_ref, xn_ref,
                g_ref, sh_ref, sc_ref, gate_ref, wg_ref, wv_ref, cwg_ref, cwv_ref, cbg_ref, cbv_ref, wd_ref, fg_ref,
                o_ref, h_ref, acc_ref, u_ref):
    i, j = pl.program_id(0), pl.program_id(1)
    ext = tm + 2 * halo
    is_lat = i < n_lat_tiles

    @pl.when(j == 0)
    def _():
        g, sh, sc = g_ref[...], sh_ref[...], sc_ref[...]
        pos = lax.rem(i, tiles_per_seg)
        keep_prev = jnp.logical_and(is_lat, pos != 0)
        keep_next = jnp.logical_and(is_lat, pos != tiles_per_seg - 1)
        hp = _normmod(xp_ref[...], g, sh, sc)
        hn = _normmod(xn_ref[...], g, sh, sc)
        h_ref[0:halo, :] = jnp.where(keep_prev, hp, 0.0).astype(_BF16)
        h_ref[halo:halo + tm, :] = _normmod(x_ref[...], g, sh, sc).astype(_BF16)
        h_ref[halo + tm:ext, :] = jnp.where(keep_next, hn, 0.0).astype(_BF16)
        acc_ref[...] = jnp.zeros_like(acc_ref)

    def step(inner_boundaries):
        h = h_ref[...]
        tf = wg_ref.shape[1]
        ts = tf // n_sub
        subs = [slice(s * ts, (s + 1) * ts) for s in range(n_sub)]

        def conv(u_ref, cw, cb):
            up = u_ref[halo - 1:halo - 1 + tm, :]
            un = u_ref[halo + 1:halo + 1 + tm, :]
            if inner_boundaries:
                t = lax.rem(lax.broadcasted_iota(jnp.int32, up.shape, 0), nc)
                up = jnp.where(t != 0, up, 0.0)
                un = jnp.where(t != nc - 1, un, 0.0)
            return up * cw[0:1] + u_ref[halo:halo + tm, :] * cw[1:2] + un * cw[2:3] + cb

        def up_proj(s):
            u_ref[s, 0] = _dot(h, wg_ref[:, subs[s]])
            u_ref[s, 1] = _dot(h, wv_ref[:, subs[s]])

        up_proj(0)
        for s, sl in enumerate(subs):
            if s + 1 < n_sub:
                up_proj(s + 1)
            gt = conv(u_ref.at[s, 0], cwg_ref[:, sl], cbg_ref[:, sl])
            vl = conv(u_ref.at[s, 1], cwv_ref[:, sl], cbv_ref[:, sl])
            act = (gt * _sigmoid(gt) * vl).astype(_BF16)
            acc_ref[...] += _dot(act, wd_ref[sl, :])

    if n_ctx_tiles == 0:
        step(False)
    else:
        pl.when(is_lat)(functools.partial(step, False))
        pl.when(jnp.logical_not(is_lat))(functools.partial(step, tm > nc))

    @pl.when(j == pl.num_programs(1) - 1)
    def _():
        y = x_ref[...] + gate_ref[...] * acc_ref[...]
        if final_norm:
            y = y * lax.rsqrt(jnp.mean(y * y, axis=-1, keepdims=True) + NORM_EPS) * fg_ref[...]
        o_ref[...] = y


def _ffn(x, g, shift, scale, gate, w_up, conv_w, conv_b, w_down, final_g, lay, n_rows, final_norm, tm=512, tf=512,
         sub=MXU_TILE):
    M, D = x.shape
    F = w_down.shape[0]
    tm = _row_tile(lay, tm)
    assert tm % lay.nc == 0, "context tiles must hold whole segments"
    tiles_per_seg = lay.n // tm
    n_lat_tiles = lay.lat_rows // tm
    n_tiles = n_rows // tm
    tf = _pick(F, tf)
    nf = F // tf
    halo = BF16_ROWS
    per = tm // halo
    seg = _seg_index(lay, tm)
    vec = lambda: pl.BlockSpec((None, 1, D), lambda i, j: (seg(i), 0, 0))
    n_sub = tf // _pick(tf, sub)
    return pl.pallas_call(
        functools.partial(_ffn_kernel, n_lat_tiles, n_tiles - n_lat_tiles, tiles_per_seg, lay.nc, tm, halo, n_sub,
                          final_norm),
        grid=(n_tiles, nf),
        in_specs=[
            pl.BlockSpec((tm, D), lambda i, j: (i, 0)),
            pl.BlockSpec((halo, D), lambda i, j: (jnp.maximum(i * per - 1, 0), 0)),
            pl.BlockSpec((halo, D), lambda i, j: (jnp.minimum((i + 1) * per, M // halo - 1), 0)),
            pl.BlockSpec((1, D), lambda i, j: (0, 0)),
            vec(), vec(), vec(),
            pl.BlockSpec((D, tf), lambda i, j: (0, j)),
            pl.BlockSpec((D, tf), lambda i, j: (0, nf + j)),
            pl.BlockSpec((CONV_W, tf), lambda i, j: (0, j)),
            pl.BlockSpec((CONV_W, tf), lambda i, j: (0, nf + j)),
            pl.BlockSpec((1, tf), lambda i, j: (0, j)),
            pl.BlockSpec((1, tf), lambda i, j: (0, nf + j)),
            pl.BlockSpec((tf, D), lambda i, j: (j, 0)),
            pl.BlockSpec((1, D), lambda i, j: (0, 0)),
        ],
        out_specs=pl.BlockSpec((tm, D), lambda i, j: (i, 0)),
        out_shape=jax.ShapeDtypeStruct((n_rows, D), _F32),
        scratch_shapes=[pltpu.VMEM((tm + 2 * halo, D), _BF16), pltpu.VMEM((tm, D), _F32),
                        pltpu.VMEM((n_sub, 2, tm + 2 * halo, tf // n_sub), _F32)],
        compiler_params=_cparams("parallel", "arbitrary"),
        name="conv_gated_ffn",
    )(x, x, x, g, shift, scale, gate, w_up, w_up, conv_w, conv_w, conv_b, conv_b, w_down, final_g.reshape(1, D))


def _shift_mix_kernel(lay, tm, halo, full, loras, x_ref, xp_ref, xn_ref, g_ref, sh_ref, sc_ref, mu_ref, *refs):
    w_refs = refs[:len(loras)]
    o_refs = refs[len(loras):]
    i = pl.program_id(0)
    g, sh, sc = g_ref[...], sh_ref[...], sc_ref[...]
    h = _normmod(x_ref[...], g, sh, sc)
    hp = _normmod(xp_ref[...], g, sh, sc)
    hn = _normmod(xn_ref[...], g, sh, sc)
    ext = jnp.concatenate([hp, h, hn], axis=0)
    n_ext = tm + 2 * halo
    has_prev, has_next = _seg_masks(i, tm, h.shape, lay)
    prev = jnp.where(has_prev, pltpu.roll(ext, 1, 0)[halo:halo + tm], 0.0)
    nxt = jnp.where(has_next, pltpu.roll(ext, n_ext - 1, 0)[halo:halo + tm], 0.0)
    dx = 0.5 * (prev + nxt) - h
    mu = mu_ref[...]
    mixes = {}

    def mix(m):
        if m not in mixes:
            mixes[m] = (h + dx * mu[m:m + 1]).astype(_BF16)
        return mixes[m]

    for m, o_ref in zip(full, o_refs):
        o_ref[...] = mix(m)
    for (m, act), w_ref, o_ref in zip(loras, w_refs, o_refs[len(full):]):
        t = _dot(mix(m), w_ref[...])
        if act == "tanh":
            t = jnp.tanh(t)
        elif act == "sigmoid":
            t = _sigmoid(t)
        o_ref[...] = t.astype(o_ref.dtype)


def _shift_mix(x, g, shift, scale, mu, full, loras, lay, tm=256):
    M, D = x.shape
    tm = _row_tile(lay, tm)
    halo = 8
    per = tm // halo
    seg = _seg_index(lay, tm)
    vec = lambda: pl.BlockSpec((None, 1, D), lambda i: (seg(i), 0, 0))
    w1s = [w for _, w, _ in loras]
    outs = pl.pallas_call(
        functools.partial(_shift_mix_kernel, lay, tm, halo, tuple(full), tuple((m, act) for m, _, act in loras)),
        grid=(M // tm,),
        in_specs=[
            pl.BlockSpec((tm, D), lambda i: (i, 0)),
            pl.BlockSpec((halo, D), lambda i: (jnp.maximum(i * per - 1, 0), 0)),
            pl.BlockSpec((halo, D), lambda i: (jnp.minimum((i + 1) * per, M // halo - 1), 0)),
            pl.BlockSpec((1, D), lambda i: (0, 0)),
            vec(), vec(),
            pl.BlockSpec(mu.shape, lambda i: (0, 0)),
        ] + [pl.BlockSpec(w.shape, lambda i: (0, 0)) for w in w1s],
        out_specs=[pl.BlockSpec((tm, D), lambda i: (i, 0)) for _ in full]
        + [pl.BlockSpec((tm, w.shape[1]), lambda i: (i, 0)) for w in w1s],
        out_shape=[jax.ShapeDtypeStruct((M, D), _BF16) for _ in full]
        + [jax.ShapeDtypeStruct((M, w.shape[1]), _BF16) for w in w1s],
        compiler_params=_cparams("parallel"),
        name="rwkv_shift_mix",
    )(x, x, x, g, shift, scale, mu, *w1s)
    return outs[:len(full)], outs[len(full):]


def _wkv_kernel(reverse, final, C, LW, r_ref, k_ref, v_ref, wp_ref, ap_ref, kk_ref, ka_ref, rk_ref, *refs):
    if final:
        y0_ref, b0_ref, g_ref, lng_ref, lnb_ref, o_ref, s_ref = refs
    else:
        y_ref, bon_ref, s_ref = refs
    j = pl.program_id(2)

    @pl.when(j == 0)
    def _():
        s_ref[...] = jnp.zeros_like(s_ref)

    hpg = WKV_GROUP // RWKV_HEAD
    groups = range(LW // WKV_GROUP)
    ng = len(groups)
    sls = [slice(g * WKV_GROUP, (g + 1) * WKV_GROUP) for g in groups]
    row = lax.broadcasted_iota(jnp.int32, (C, C), 0)
    col = lax.broadcasted_iota(jnp.int32, (C, C), 1)
    tri = jnp.where((col >= row) if reverse else (col <= row), 1.0, 0.0).astype(_BF16)
    rc = lax.broadcasted_iota(jnp.int32, (C, WKV_GROUP), 0)
    cc = lax.rem(lax.broadcasted_iota(jnp.int32, (C, WKV_GROUP), 1), C)
    strict = (cc > rc) if reverse else (cc < rc)
    incl = (cc >= rc) if reverse else (cc <= rc)
    eye_cat = jnp.where(cc == rc, 1.0, 0.0).astype(_F32)
    br = lax.broadcasted_iota(jnp.int32, (WKV_GROUP, WKV_GROUP), 0) // RWKV_HEAD
    bc = lax.broadcasted_iota(jnp.int32, (WKV_GROUP, WKV_GROUP), 1) // RWKV_HEAD
    same_head = br == bc
    head_ones = jnp.where(same_head, 1.0, 0.0).astype(_BF16)

    def bd(x):
        return jnp.where(same_head, jnp.concatenate([x] * hpg, axis=0), jnp.zeros((), x.dtype))

    cast = lambda x: x.astype(_BF16)

    n_chunks = r_ref.shape[0] // C
    units = [(c, g) for c in range(n_chunks) for g in groups]
    nu = len(units)
    tile = lambda ref, u: ref[u[0] * C:(u[0] + 1) * C, sls[u[1]]]
    lane = lambda ref, u: ref[:, sls[u[1]]]

    v = [tile(v_ref, u) for u in units]
    lw = [-math.exp(-0.5) * _sigmoid(tile(wp_ref, u)) for u in units]
    cl = [_ones_dot_left(tri, x, 3) for x in lw]
    p_total = [jnp.exp(c[0:1] if reverse else c[C - 1:C]) for c in cl]

    a = [_sigmoid(tile(ap_ref, u)) for u in units]
    kk = [tile(k_ref, u) * lane(kk_ref, u) for u in units]
    kd = [tile(k_ref, u) * (1.0 + (a[i] - 1.0) * lane(ka_ref, u)) for i, u in enumerate(units)]
    rkd = [tile(r_ref, u) * kd[i] * lane(rk_ref, u) for i, u in enumerate(units)]
    sums = _ones_dot_rows(head_ones, [x * x for x in kk] + rkd, 2)
    kk = [x * lax.rsqrt(jnp.maximum(n, 1e-24)) for x, n in zip(kk, sums[:nu])]
    bonus = [s * x for s, x in zip(sums[nu:], v)]
    b = [kk[i] * a[i] for i in range(nu)]

    AR = [jnp.concatenate([cast(-kk[i] * jnp.exp(cl[i] - lw[i])), cast(tile(r_ref, u) * jnp.exp(cl[i]))], axis=0)
          for i, u in enumerate(units)]
    BKbd, BKp = [], []
    for i in range(nu):
        e_neg = jnp.exp(-cl[i])
        e_rem = e_neg * p_total[i]
        BKbd.append(jnp.concatenate([bd(cast(b[i] * e_neg)), bd(cast(kd[i] * e_neg))], axis=0))
        BKp.append(cast(jnp.concatenate([b[i] * e_rem, kd[i] * e_rem], axis=0)))
    vb = [cast(x) for x in v]
    Vbd = [bd(x) for x in vb]

    M = [_dot_nt(AR[i], BKbd[i]) for i in range(nu)]
    Lab = [jnp.where(strict, m[:C, :WKV_GROUP], 0.0) for m in M]
    LMk = [cast(jnp.concatenate([jnp.where(strict, m[:C, WKV_GROUP:], 0.0),
                                 jnp.where(incl, m[C:, WKV_GROUP:], 0.0)], axis=0)) for m in M]
    Mrb = [cast(jnp.where(incl, m[C:, :WKV_GROUP], 0.0)) for m in M]
    LMkV = [_dot(LMk[i], Vbd[i]) for i in range(nu)]

    Pw = [cast(x) for x in Lab]
    T = [eye_cat + x for x in Lab]
    Pw = [cast(_dot(x, bd(x))) for x in Pw]
    for _ in range(int(math.log2(C)) - 2):
        both = [_dot(jnp.concatenate([cast(t), x], axis=0), bd(x)) for t, x in zip(T, Pw)]
        T = [t + bt[:C] for t, bt in zip(T, both)]
        Pw = [cast(bt[C:]) for bt in both]
    Tb = [cast(t + _dot(cast(t), bd(x))) for t, x in zip(T, Pw)]

    S = [s_ref[g] for g in groups]
    y = [None] * nu
    for c in (reversed(range(n_chunks)) if reverse else range(n_chunks)):
        us = [c * ng + g for g in groups]
        ARS = [_dot_nt(AR[i], cast(S[g])) for g, i in enumerate(us)]
        X = [ARS[g][:C] + LMkV[i][:C] for g, i in enumerate(us)]
        Z = [cast(_dot(Tb[i], bd(cast(X[g])))) for g, i in enumerate(us)]
        for g, i in enumerate(us):
            y[i] = ARS[g][C:] + LMkV[i][C:] + _dot(Mrb[i], bd(Z[g]))
        dS = [_dot_tn(jnp.concatenate([Z[g], vb[i]], axis=0), BKp[i]) for g, i in enumerate(us)]
        S = [S[g] * p_total[i] + jnp.where(same_head, dS[g], 0.0) for g, i in enumerate(us)]
    s_ref[...] = jnp.stack(S, axis=0)

    def assemble(xs):
        return jnp.concatenate([jnp.concatenate(xs[c * ng:(c + 1) * ng], axis=1) for c in range(n_chunks)], axis=0)

    if final:
        inv = 1.0 / RWKV_HEAD
        wkv = [y[i] + tile(y0_ref, u) for i, u in enumerate(units)]
        cen = [x - s * inv for x, s in zip(wkv, _ones_dot_rows(head_ones, wkv, 3))]
        var = [s * inv for s in _ones_dot_rows(head_ones, [x * x for x in cen], 2)]
        outs = []
        for i, u in enumerate(units):
            normed = cen[i] * lax.rsqrt(var[i] + GN_EPS) * lane(lng_ref, u) + lane(lnb_ref, u)
            outs.append(((normed + bonus[i] + tile(b0_ref, u)) * tile(g_ref, u)).astype(o_ref.dtype))
        o_ref[...] = assemble(outs)
    else:
        y_ref[...] = assemble(y)
        bon_ref[...] = assemble(bonus)


def _wkv(r, k, v, wpre, apre, k_k, k_a, r_k, lay, reverse, final_args=None, lw=2048, chunks_per_step=2):
    M, D = r.shape
    C = WKV_CHUNK
    rows = chunks_per_step * C
    assert C == RWKV_HEAD and lay.nc % rows == 0 and lay.n % rows == 0 and D % WKV_GROUP == 0
    lw = _pick(D, lw)
    assert lw % WKV_GROUP == 0
    ncc, ncl = lay.nc // rows, lay.n // rows
    ctx_blk0 = lay.lat_rows // rows

    def row_blk(b, j):
        if reverse:
            return jnp.where(j < ncc, ctx_blk0 + b * ncc + (ncc - 1 - j), b * ncl + (ncl - 1 - (j - ncc)))
        return jnp.where(j < ncc, ctx_blk0 + b * ncc + j, b * ncl + (j - ncc))

    tile = lambda: pl.BlockSpec((rows, lw), lambda b, h, j: (row_blk(b, j), h))
    vec = lambda: pl.BlockSpec((1, lw), lambda b, h, j: (0, h))
    in_specs = [tile() for _ in range(5)] + [vec() for _ in range(3)]
    args = [r, k, v, wpre, apre, k_k.reshape(1, D), k_a.reshape(1, D), r_k.reshape(1, D)]
    final = final_args is not None
    if final:
        y0, b0, g, lng, lnb = final_args
        in_specs += [tile(), tile(), tile(), vec(), vec()]
        args += [y0, b0, g, lng.reshape(1, D), lnb.reshape(1, D)]
        out_specs = tile()
        out_shape = jax.ShapeDtypeStruct((M, D), _BF16)
    else:
        out_specs = [tile(), tile()]
        out_shape = [jax.ShapeDtypeStruct((M, D), _F32), jax.ShapeDtypeStruct((M, D), _F32)]
    return pl.pallas_call(
        functools.partial(_wkv_kernel, reverse, final, C, lw),
        grid=(lay.B, D // lw, ncc + ncl),
        in_specs=in_specs,
        out_specs=out_specs,
        out_shape=out_shape,
        scratch_shapes=[pltpu.VMEM((lw // WKV_GROUP, WKV_GROUP, WKV_GROUP), _F32)],
        compiler_params=_cparams("parallel", "parallel", "arbitrary"),
        name="wkv7_scan_rev" if reverse else "wkv7_scan_fwd",
    )(*args)


def _rope_tables(lay):
    n = lay.n
    t = jnp.arange(n, dtype=jnp.int32)
    row = (t // GRID_W).astype(_F32)
    col = (t % GRID_W).astype(_F32)
    axis_dim = HEAD_DIM // 2
    inv_freq = ROPE_THETA ** (-jnp.arange(0, axis_dim, 2, dtype=_F32) / axis_dim)
    ang = jnp.concatenate([row[:, None] * inv_freq, col[:, None] * inv_freq], axis=-1)
    cos, sin = jnp.cos(ang), jnp.sin(ang)
    cos_l = jnp.concatenate([cos, cos], axis=-1)
    sin_l = jnp.concatenate([-sin, sin], axis=-1)
    n_ctx_rows = lay.B * lay.nc
    cos_f = jnp.concatenate([jnp.tile(cos_l, (lay.B, 1)), jnp.ones((n_ctx_rows, HEAD_DIM), _F32)], axis=0)
    sin_f = jnp.concatenate([jnp.tile(sin_l, (lay.B, 1)), jnp.zeros((n_ctx_rows, HEAD_DIM), _F32)], axis=0)
    return cos_f, sin_f


def _deinterleave_heads(w, n_heads):
    lead = w.shape[:-1]
    w = w.reshape(*lead, n_heads, HEAD_DIM // 2, 2)
    w = jnp.swapaxes(w, -1, -2)
    return w.reshape(*lead, n_heads * HEAD_DIM)


def _pad_cols(w, mult=LANES):
    pad = (-w.shape[-1]) % mult
    return jnp.pad(w, [(0, 0)] * (w.ndim - 1) + [(0, pad)]) if pad else w


def _pad_rows(w, mult=LANES):
    pad = (-w.shape[-2]) % mult
    return jnp.pad(w, [(0, 0)] * (w.ndim - 2) + [(0, pad), (0, 0)]) if pad else w


def kernel(x, c, ctx, c_ctx, w_mod, b_mod, norm1_g, norm2_g, attn_w_in, attn_w_out, q_norm_g, k_norm_g, rwkv_mu, rwkv_w_r, rwkv_w_k, rwkv_w_v, rwkv_w_o, rwkv_decay_w0, rwkv_decay_w1, rwkv_decay_w2, rwkv_iclr_a0, rwkv_iclr_a1, rwkv_iclr_a2, rwkv_gate_g1, rwkv_gate_g2, rwkv_k_k, rwkv_k_a, rwkv_r_k, rwkv_lnx_g, rwkv_lnx_b, rwkv_vres_v0, rwkv_vres_v1, rwkv_vres_v2, ffn_w_up, ffn_conv_w, ffn_conv_b, ffn_w_down, final_norm_g):
    B, n, D = x.shape
    nc = ctx.shape[1]
    depth = w_mod.shape[0]
    lay = Layout(B, n, nc, B * n, B * (n + nc))
    attn_w = N_ATTN_HEADS * HEAD_DIM
    kv_w = N_KV_HEADS * HEAD_DIM
    fw = D - attn_w
    bf = lambda w: w.astype(_BF16)

    xs = jnp.concatenate([x.reshape(B * n, D), ctx.reshape(B * nc, D)], axis=0)

    cvec = jnp.concatenate([c, c_ctx[None, :], jnp.zeros((-(B + 1) % BF16_ROWS, D), _F32)], axis=0)
    mods = _modulation(cvec, w_mod, b_mod)[:, :B + 1]
    mods = mods.reshape(depth, B + 1, 6, 1, D)
    cos_f, sin_f = _rope_tables(lay)
    qg_scale = HEAD_DIM ** -0.5

    v_first = None
    for i in range(depth):
        last = i == depth - 1
        jl = i // 2
        m = [mods[i, :, t] for t in range(6)]
        g1 = norm1_g[i].reshape(1, D)
        if i % 2 == 0:
            w_in = bf(attn_w_in[jl])
            w_in = jnp.concatenate([
                _deinterleave_heads(w_in[:, :attn_w], N_ATTN_HEADS),
                _deinterleave_heads(w_in[:, attn_w:attn_w + kv_w], N_KV_HEADS),
                w_in[:, attn_w + kv_w:]], axis=1)
            qg = (_deinterleave_heads(q_norm_g[jl], 1) * qg_scale).reshape(1, HEAD_DIM)
            kg = _deinterleave_heads(k_norm_g[jl], 1).reshape(1, HEAD_DIM)
            u = _attn_in_proj(xs, g1, m[0], m[1], w_in, qg, kg, cos_f, sin_f, lay)
            o = _attention(u, lay)
            fm = jnp.concatenate([_fourier_mix(u, lay, fw), _fourier_mix(u, lay, fw, ctx=True)], axis=0)
            w_out = bf(attn_w_out[jl])
            xs = _out_proj([o, fm], [w_out[:attn_w], w_out[attn_w:]], xs, m[2], lay)
        else:
            loras = [(1, _lora_w1(list(rwkv_decay_w1[jl])), "tanh"), (4, _lora_w1(list(rwkv_iclr_a1[jl])), None),
                     (5, _lora_w1([rwkv_gate_g1[jl]]), "sigmoid")]
            if jl > 0:
                loras.append((3, _lora_w1([rwkv_vres_v1[jl - 1]]), None))
            (x_r, x_k, x_v), acts = _shift_mix(xs, g1, m[0], m[1], rwkv_mu[jl], (0, 2, 3), loras, lay)
            r = _linear(x_r, bf(rwkv_w_r[jl]), name="rwkv_r")
            k = _linear(x_k, bf(rwkv_w_k[jl]), name="rwkv_k")
            if jl == 0:
                v = v_first = _linear(x_v, bf(rwkv_w_v[jl]), name="rwkv_v")
            else:
                v = _linear(x_v, bf(rwkv_w_v[jl]), name="rwkv_v_vres",
                            vres=(acts[3], rwkv_vres_v2[jl - 1], rwkv_vres_v0[jl - 1], v_first))
            wpre = _lora_out(acts[0], list(rwkv_decay_w2[jl]), list(rwkv_decay_w0[jl]), name="rwkv_decay")
            apre = _lora_out(acts[1], list(rwkv_iclr_a2[jl]), list(rwkv_iclr_a0[jl]), name="rwkv_iclr")
            (g,) = _lora_out(acts[2], [rwkv_gate_g2[jl]], name="rwkv_gate")
            kk_, ka_, rk_ = rwkv_k_k[jl], rwkv_k_a[jl], rwkv_r_k[jl].reshape(D)
            y0, bon0 = _wkv(r, k, v, wpre[0], apre[0], kk_, ka_, rk_, lay, reverse=False)
            z = _wkv(r, k, v, wpre[1], apre[1], kk_, ka_, rk_, lay, reverse=True,
                     final_args=(y0, bon0, g, rwkv_lnx_g[jl], rwkv_lnx_b[jl]))
            xs = _out_proj([z], [bf(rwkv_w_o[jl])], xs, m[2], lay)
        n_rows = lay.lat_rows if last else lay.rows
        xs = _ffn(xs, norm2_g[i].reshape(1, D), m[3], m[4], m[5], bf(ffn_w_up[i]), ffn_conv_w[i],
                  ffn_conv_b[i].reshape(1, -1), bf(ffn_w_down[i]), final_norm_g, lay, n_rows, final_norm=last)
    return xs.reshape(B, n, D)
```

```python
import collections
import functools
import math

import jax
import jax.numpy as jnp
from jax import lax
from jax.experimental import pallas as pl
from jax.experimental.pallas import tpu as pltpu

HEAD_DIM = 128
N_ATTN_HEADS = 12
N_KV_HEADS = 4
GRID_W = 64
ROPE_THETA = 10000.0
N_FOURIER_GROUPS = 4
RWKV_HEAD = 64
CONV_W = 3
NORM_EPS = 1e-6
GN_EPS = 64e-5

LANES = 128
MXU_TILE = 256
F32_ROWS = 8
BF16_ROWS = 16
VMEM_LIMIT = 56 * 1024 * 1024
WKV_CHUNK = 64
WKV_GROUP = MXU_TILE

_F32 = jnp.float32
_BF16 = jnp.bfloat16

Layout = collections.namedtuple("Layout", "B n nc lat_rows rows")


def _cparams(*sem):
    return pltpu.CompilerParams(dimension_semantics=sem, vmem_limit_bytes=VMEM_LIMIT)


def _pick(total, pref):
    t = min(pref, total)
    while total % t:
        t -= 1
    return t


def _row_tile(lay, pref):
    return _pick(lay.B * lay.nc, _pick(lay.n, pref))


def _seg_index(lay, tm):
    def f(i):
        r0 = i * tm
        return jnp.where(r0 < lay.lat_rows, r0 // lay.n, lay.B)
    return f


def _seg_masks(i, tm, shape, lay):
    t = lax.broadcasted_iota(jnp.int32, shape, 0)
    r0 = i * tm
    is_lat = r0 < lay.lat_rows
    if tm >= lay.nc:
        pc = lax.rem(t, lay.nc)
    else:
        pc = t + lax.rem(r0 - lay.lat_rows, lay.nc)
    pos = jnp.where(is_lat, t + lax.rem(r0, lay.n), pc)
    last = jnp.where(is_lat, lay.n - 1, lay.nc - 1)
    return pos != 0, pos != last


def _normmod(x, g, shift, scale):
    ms = jnp.mean(x * x, axis=-1, keepdims=True)
    return (x * lax.rsqrt(ms + NORM_EPS)) * g * (1.0 + scale) + shift


def _sigmoid(x):
    return 1.0 / (1.0 + jnp.exp(-x))


def _dot(a, b):
    return jnp.dot(a, b, preferred_element_type=_F32)


def _dot_nt(a, b):
    return lax.dot_general(a, b, (((1,), (1,)), ((), ())), preferred_element_type=_F32)


def _dot_tn(a, b):
    return lax.dot_general(a, b, (((0,), (0,)), ((), ())), preferred_element_type=_F32)


def _bf16_pieces(x, parts):
    pieces = []
    rem = x
    for i in range(parts):
        piece = rem.astype(_BF16)
        pieces.append(piece)
        if i + 1 < parts:
            rem = rem - piece.astype(_F32)
    return pieces


def _ones_dot_left(ones, x, parts):
    acc = None
    for piece in _bf16_pieces(x, parts):
        d = _dot(ones, piece)
        acc = d if acc is None else acc + d
    return acc


def _ones_dot_rows(ones, xs, parts):
    rows = xs[0].shape[0]
    stacked = jnp.concatenate([p for x in xs for p in _bf16_pieces(x, parts)], axis=0)
    res = _dot(stacked, ones)
    outs = []
    for n in range(len(xs)):
        base = n * parts * rows
        acc = res[base:base + rows]
        for i in range(1, parts):
            acc = acc + res[base + i * rows:base + (i + 1) * rows]
        outs.append(acc)
    return outs


def _mod_kernel(c_ref, w_ref, b_ref, o_ref):
    c = c_ref[...]
    s = c * _sigmoid(c)
    rows = s.shape[0]
    res = _dot(jnp.concatenate(_bf16_pieces(s, 3), axis=0), w_ref[...].astype(_BF16))
    o_ref[...] = res[:rows] + res[rows:2 * rows] + res[2 * rows:] + b_ref[...]


def _modulation(cvec, w_mod, b_mod):
    depth, D, N = w_mod.shape
    rows = cvec.shape[0]
    tn = _pick(N, 512)
    return pl.pallas_call(
        _mod_kernel,
        grid=(depth, N // tn),
        in_specs=[
            pl.BlockSpec((rows, D), lambda l, j: (0, 0)),
            pl.BlockSpec((None, D, tn), lambda l, j: (l, 0, j)),
            pl.BlockSpec((None, 1, tn), lambda l, j: (l, 0, j)),
        ],
        out_specs=pl.BlockSpec((None, rows, tn), lambda l, j: (l, 0, j)),
        out_shape=jax.ShapeDtypeStruct((depth, rows, N), _F32),
        compiler_params=_cparams("parallel", "parallel"),
        name="adaln_modulation",
    )(cvec, w_mod, b_mod.reshape(depth, 1, N))


def _linear_kernel(has_vres, x_ref, w_ref, *refs):
    o_ref = refs[-1]
    acc = _dot(x_ref[...], w_ref[...])
    if has_vres:
        t_ref, w2_ref, b_ref, vf_ref = refs[:4]
        gate = _sigmoid(_dot(t_ref[...], w2_ref[...]) + b_ref[...])
        acc = acc + (vf_ref[...].astype(_F32) - acc) * gate
    o_ref[...] = acc.astype(o_ref.dtype)


def _linear(x, w, vres=None, out_dtype=_BF16, tm=512, name="linear"):
    M, K = x.shape
    N = w.shape[1]
    tm = _pick(M, tm)
    in_specs = [pl.BlockSpec((tm, K), lambda i: (i, 0)), pl.BlockSpec((K, N), lambda i: (0, 0))]
    args = [x, w]
    if vres is not None:
        t, v2, v0, v_first = vres
        w2 = _pad_rows(v2).astype(_BF16)
        R = t.shape[1]
        assert w2.shape[0] == R
        in_specs += [pl.BlockSpec((tm, R), lambda i: (i, 0)), pl.BlockSpec((R, N), lambda i: (0, 0)),
                     pl.BlockSpec((1, N), lambda i: (0, 0)), pl.BlockSpec((tm, N), lambda i: (i, 0))]
        args += [t, w2, v0.reshape(1, N), v_first]
    return pl.pallas_call(
        functools.partial(_linear_kernel, vres is not None),
        grid=(M // tm,),
        in_specs=in_specs,
        out_specs=pl.BlockSpec((tm, N), lambda i: (i, 0)),
        out_shape=jax.ShapeDtypeStruct((M, N), out_dtype),
        compiler_params=_cparams("parallel"),
        name=name,
    )(*args)


def _lora_w1(w1s):
    return jnp.concatenate([_pad_cols(w) for w in w1s], axis=1).astype(_BF16)


def _lora_out_kernel(ranks, has_bias, t_ref, *refs):
    n_out = len(ranks)
    w2_refs = refs[:n_out]
    b_refs = refs[n_out:2 * n_out] if has_bias else ()
    o_refs = refs[-n_out:]
    off = 0
    for d in range(n_out):
        y = _dot(t_ref[:, off:off + ranks[d]], w2_refs[d][...])
        if has_bias:
            y = y + b_refs[d][...]
        o_refs[d][...] = y
        off += ranks[d]


def _lora_out(t, w2s, biases=None, tm=512, name="lora_out"):
    M, R = t.shape
    N = w2s[0].shape[1]
    tm = _pick(M, tm)
    w2p = [_pad_rows(w).astype(_BF16) for w in w2s]
    ranks = tuple(w.shape[0] for w in w2p)
    assert sum(ranks) == R
    in_specs = [pl.BlockSpec((tm, R), lambda i: (i, 0))]
    in_specs += [pl.BlockSpec(w.shape, lambda i: (0, 0)) for w in w2p]
    args = [t, *w2p]
    if biases is not None:
        in_specs += [pl.BlockSpec((1, N), lambda i: (0, 0)) for _ in biases]
        args += [b.reshape(1, N) for b in biases]
    return pl.pallas_call(
        functools.partial(_lora_out_kernel, ranks, biases is not None),
        grid=(M // tm,),
        in_specs=in_specs,
        out_specs=[pl.BlockSpec((tm, N), lambda i: (i, 0)) for _ in w2p],
        out_shape=[jax.ShapeDtypeStruct((M, N), _F32) for _ in w2p],
        compiler_params=_cparams("parallel"),
        name=name,
    )(*args)


def _attn_in_kernel(sw, n_q, n_qk, x_ref, g_ref, sh_ref, sc_ref, w_ref, qg_ref, kg_ref, cos_ref, sin_ref, o_ref):
    h = _normmod(x_ref[...], g_ref[...], sh_ref[...], sc_ref[...]).astype(_BF16)
    cos, sin = cos_ref[...], sin_ref[...]
    n_blocks = w_ref.shape[1] // sw

    def finish(s, acc):
        cols = slice(s * sw, (s + 1) * sw)
        if s >= n_qk:
            o_ref[:, cols] = acc.astype(o_ref.dtype)
            return
        gain = qg_ref[...] if s < n_q else kg_ref[...]
        for hd in range(sw // HEAD_DIM):
            a = acc[:, hd * HEAD_DIM:(hd + 1) * HEAD_DIM]
            y = a * lax.rsqrt(jnp.mean(a * a, axis=-1, keepdims=True) + NORM_EPS) * gain
            y = y * cos + pltpu.roll(y, HEAD_DIM // 2, 1) * sin
            o_ref[:, s * sw + hd * HEAD_DIM:s * sw + (hd + 1) * HEAD_DIM] = y.astype(o_ref.dtype)

    acc = _dot(h, w_ref[:, 0:sw])
    for s in range(n_blocks):
        nxt = _dot(h, w_ref[:, (s + 1) * sw:(s + 2) * sw]) if s + 1 < n_blocks else None
        finish(s, acc)
        acc = nxt


def _attn_in_proj(x, g, shift, scale, w, qg, kg, cos, sin, lay, tm=512):
    M, D = x.shape
    N = w.shape[1]
    tm = _row_tile(lay, tm)
    attn_w = N_ATTN_HEADS * HEAD_DIM
    kv_w = N_KV_HEADS * HEAD_DIM
    sw = math.gcd(math.gcd(attn_w, kv_w), MXU_TILE)
    seg = _seg_index(lay, tm)
    vec = lambda: pl.BlockSpec((None, 1, D), lambda i: (seg(i), 0, 0))
    return pl.pallas_call(
        functools.partial(_attn_in_kernel, sw, attn_w // sw, (attn_w + kv_w) // sw),
        grid=(M // tm,),
        in_specs=[
            pl.BlockSpec((tm, D), lambda i: (i, 0)),
            pl.BlockSpec((1, D), lambda i: (0, 0)),
            vec(), vec(),
            pl.BlockSpec((D, N), lambda i: (0, 0)),
            pl.BlockSpec((1, HEAD_DIM), lambda i: (0, 0)),
            pl.BlockSpec((1, HEAD_DIM), lambda i: (0, 0)),
            pl.BlockSpec((tm, HEAD_DIM), lambda i: (i, 0)),
            pl.BlockSpec((tm, HEAD_DIM), lambda i: (i, 0)),
        ],
        out_specs=pl.BlockSpec((tm, N), lambda i: (i, 0)),
        out_shape=jax.ShapeDtypeStruct((M, N), _BF16),
        compiler_params=_cparams("parallel"),
        name="attn_in_proj",
    )(x, g, shift, scale, w, qg, kg, cos, sin)


def _attn_kernel(n_lat_tiles, g_per_kv, kb, q_ref, kc_ref, vc_ref, kl_ref, vl_ref, o_ref):
    qi = pl.program_id(2)
    tq = q_ref.shape[0]
    q = jnp.concatenate([q_ref[:, g * HEAD_DIM:(g + 1) * HEAD_DIM] for g in range(g_per_kv)], axis=0)

    def finish(acc, l):
        o = acc / l
        for g in range(g_per_kv):
            o_ref[:, g * HEAD_DIM:(g + 1) * HEAD_DIM] = o[g * tq:(g + 1) * tq].astype(o_ref.dtype)

    s_c = _dot_nt(q, kc_ref[...])
    m = jnp.max(s_c, axis=-1, keepdims=True)
    p = jnp.exp(s_c - m)
    l = jnp.sum(p, axis=-1, keepdims=True)
    acc = _dot(p.astype(_BF16), vc_ref[...])

    @pl.when(qi < n_lat_tiles)
    def _():
        n_blocks = kl_ref.shape[0] // kb
        mj, lj, accj = m, l, acc
        s_next = _dot_nt(q, kl_ref[0:kb, :])
        for j in range(n_blocks):
            s = s_next
            if j + 1 < n_blocks:
                s_next = _dot_nt(q, kl_ref[(j + 1) * kb:(j + 2) * kb, :])
            m_new = jnp.maximum(mj, jnp.max(s, axis=-1, keepdims=True))
            alpha = jnp.exp(mj - m_new)
            pj = jnp.exp(s - m_new)
            lj = lj * alpha + jnp.sum(pj, axis=-1, keepdims=True)
            accj = accj * alpha + _dot(pj.astype(_BF16), vl_ref[j * kb:(j + 1) * kb, :])
            mj = m_new
        finish(accj, lj)

    @pl.when(qi >= n_lat_tiles)
    def _():
        finish(acc, l)


def _attention(u, lay, tq=256, kb=1024):
    M = u.shape[0]
    g_per_kv = N_ATTN_HEADS // N_KV_HEADS
    qw = g_per_kv * HEAD_DIM
    attn_w = N_ATTN_HEADS * HEAD_DIM
    tq = _pick(lay.nc, tq)
    nlt, nct = lay.n // tq, lay.nc // tq
    k_blk = attn_w // HEAD_DIM
    v_blk = k_blk + N_KV_HEADS
    ctx_blk0 = lay.lat_rows // lay.nc

    def q_map(b, kv, qi):
        row = jnp.where(qi < nlt, b * nlt + qi, lay.lat_rows // tq + b * nct + (qi - nlt))
        return (row, kv)

    return pl.pallas_call(
        functools.partial(_attn_kernel, nlt, g_per_kv, _pick(lay.n, kb)),
        grid=(lay.B, N_KV_HEADS, nlt + nct),
        in_specs=[
            pl.BlockSpec((tq, qw), q_map),
            pl.BlockSpec((lay.nc, HEAD_DIM), lambda b, kv, qi: (ctx_blk0 + b, k_blk + kv)),
            pl.BlockSpec((lay.nc, HEAD_DIM), lambda b, kv, qi: (ctx_blk0 + b, v_blk + kv)),
            pl.BlockSpec((lay.n, HEAD_DIM), lambda b, kv, qi: (b, k_blk + kv)),
            pl.BlockSpec((lay.n, HEAD_DIM), lambda b, kv, qi: (b, v_blk + kv)),
        ],
        out_specs=pl.BlockSpec((tq, qw), q_map),
        out_shape=jax.ShapeDtypeStruct((M, attn_w), _BF16),
        compiler_params=_cparams("parallel", "parallel", "arbitrary"),
        name="gqa_attention",
    )(u, u, u, u, u)


def _dft_kernel(f_ref, cc_ref, sc_ref, cn_ref, sn_ref, o_ref, a_ref, b_ref):
    @pl.when(pl.program_id(1) == 0)
    def _():
        f = f_ref[...]
        a_ref[...] = _dot(f, cc_ref[...]).astype(_BF16)
        b_ref[...] = _dot(f, sc_ref[...]).astype(_BF16)

    o_ref[...] = (_dot(cn_ref[...], a_ref[...]) - _dot(sn_ref[...], b_ref[...])).astype(o_ref.dtype)


def _cos_sin_matrix(m, split=64):
    j = jnp.arange(m, dtype=jnp.int32)[:, None]
    w = 2.0 * math.pi / m

    def cs(k):
        ang = w * ((j * k[None, :]) % m).astype(_F32)
        return jnp.cos(ang), jnp.sin(ang)

    if m % split or m <= 8 * split:
        return cs(jnp.arange(m, dtype=jnp.int32))
    ca, sa = cs(split * jnp.arange(m // split, dtype=jnp.int32))
    cb, sb = cs(jnp.arange(split, dtype=jnp.int32))
    cos = ca[:, :, None] * cb[:, None, :] - sa[:, :, None] * sb[:, None, :]
    sin = sa[:, :, None] * cb[:, None, :] + ca[:, :, None] * sb[:, None, :]
    return cos.reshape(m, m), sin.reshape(m, m)


def _dft_tables(n, group, n_groups):
    cn, sn = _cos_sin_matrix(n)
    scale = 1.0 / math.sqrt(n * group)
    cg, sg = _cos_sin_matrix(group)
    eye = jnp.eye(n_groups, dtype=_F32)
    return ((cn * scale).astype(_BF16), (sn * scale).astype(_BF16),
            jnp.kron(eye, cg).astype(_BF16), jnp.kron(eye, sg).astype(_BF16))


def _fourier_mix(u, lay, fw, ctx=False, tm=512):
    M, N = u.shape
    n = lay.nc if ctx else lay.n
    tm = _pick(n, tm)
    nt = n // tm
    cn, sn, cc, sc = _dft_tables(n, fw // N_FOURIER_GROUPS, N_FOURIER_GROUPS)
    f_blk = N // fw - 1
    seg0 = lay.lat_rows // n if ctx else 0
    return pl.pallas_call(
        _dft_kernel,
        grid=(lay.B, nt),
        in_specs=[
            pl.BlockSpec((n, fw), lambda b, i: (seg0 + b, f_blk)),
            pl.BlockSpec((fw, fw), lambda b, i: (0, 0)),
            pl.BlockSpec((fw, fw), lambda b, i: (0, 0)),
            pl.BlockSpec((tm, n), lambda b, i: (i, 0)),
            pl.BlockSpec((tm, n), lambda b, i: (i, 0)),
        ],
        out_specs=pl.BlockSpec((tm, fw), lambda b, i: (b * nt + i, 0)),
        out_shape=jax.ShapeDtypeStruct((lay.B * n, fw), _BF16),
        scratch_shapes=[pltpu.VMEM((n, fw), _BF16), pltpu.VMEM((n, fw), _BF16)],
        compiler_params=_cparams("parallel", "arbitrary"),
        name="fourier_mix_ctx" if ctx else "fourier_mix_lat",
    )(u, cc, sc, cn, sn)


def _out_proj_kernel(n_lhs, *refs):
    lhs = refs[:n_lhs]
    ws = refs[n_lhs:2 * n_lhs]
    x_ref, gate_ref, o_ref = refs[2 * n_lhs], refs[2 * n_lhs + 1], refs[2 * n_lhs + 2]
    acc = _dot(lhs[0][...], ws[0][...])
    for a, w in zip(lhs[1:], ws[1:]):
        acc = acc + _dot(a[...], w[...])
    o_ref[...] = x_ref[...] + gate_ref[...] * acc


def _out_proj(lhs, ws, x, gate, lay, tm=512):
    M, D = x.shape
    tm = _row_tile(lay, tm)
    seg = _seg_index(lay, tm)
    in_specs = [pl.BlockSpec((tm, a.shape[1]), lambda i: (i, 0)) for a in lhs]
    in_specs += [pl.BlockSpec(w.shape, lambda i: (0, 0)) for w in ws]
    in_specs += [pl.BlockSpec((tm, D), lambda i: (i, 0)), pl.BlockSpec((None, 1, D), lambda i: (seg(i), 0, 0))]
    return pl.pallas_call(
        functools.partial(_out_proj_kernel, len(lhs)),
        grid=(M // tm,),
        in_specs=in_specs,
        out_specs=pl.BlockSpec((tm, D), lambda i: (i, 0)),
        out_shape=jax.ShapeDtypeStruct((M, D), _F32),
        compiler_params=_cparams("parallel"),
        name="out_proj_residual",
    )(*lhs, *ws, x, gate)


def _ffn_kernel(n_lat_tiles, n_ctx_tiles, tiles_per_seg, nc, tm, halo, n_sub, final_norm, x_ref, xp_ref, xn_ref,
                g_ref, sh_ref, sc_ref, gate_ref, wg_ref, wv_ref, cwg_ref, cwv_ref, cbg_ref, cbv_ref, wd_ref, fg_ref,
                o_ref, h_ref, acc_ref, u_ref):
    i, j = pl.program_id(0), pl.program_id(1)
    ext = tm + 2 * halo
    is_lat = i < n_lat_tiles

    @pl.when(j == 0)
    def _():
        g, sh, sc = g_ref[...], sh_ref[...], sc_ref[...]
        pos = lax.rem(i, tiles_per_seg)
        keep_prev = jnp.logical_and(is_lat, pos != 0)
        keep_next = jnp.logical_and(is_lat, pos != tiles_per_seg - 1)
        hp = _normmod(xp_ref[...], g, sh, sc)
        hn = _normmod(xn_ref[...], g, sh, sc)
        h_ref[0:halo, :] = jnp.where(keep_prev, hp, 0.0).astype(_BF16)
        h_ref[halo:halo + tm, :] = _normmod(x_ref[...], g, sh, sc).astype(_BF16)
        h_ref[halo + tm:ext, :] = jnp.where(keep_next, hn, 0.0).astype(_BF16)
        acc_ref[...] = jnp.zeros_like(acc_ref)

    def step(inner_boundaries):
        h = h_ref[...]
        tf = wg_ref.shape[1]
        ts = tf // n_sub
        subs = [slice(s * ts, (s + 1) * ts) for s in range(n_sub)]

        def conv(u_ref, cw, cb):
            up = u_ref[halo - 1:halo - 1 + tm, :]
            un = u_ref[halo + 1:halo + 1 + tm, :]
            if inner_boundaries:
                t = lax.rem(lax.broadcasted_iota(jnp.int32, up.shape, 0), nc)
                up = jnp.where(t != 0, up, 0.0)
                un = jnp.where(t != nc - 1, un, 0.0)
            return up * cw[0:1] + u_ref[halo:halo + tm, :] * cw[1:2] + un * cw[2:3] + cb

        def up_proj(s):
            u_ref[s, 0] = _dot(h, wg_ref[:, subs[s]])
            u_ref[s, 1] = _dot(h, wv_ref[:, subs[s]])

        up_proj(0)
        for s, sl in enumerate(subs):
            if s + 1 < n_sub:
                up_proj(s + 1)
            gt = conv(u_ref.at[s, 0], cwg_ref[:, sl], cbg_ref[:, sl])
            vl = conv(u_ref.at[s, 1], cwv_ref[:, sl], cbv_ref[:, sl])
            act = (gt * _sigmoid(gt) * vl).astype(_BF16)
            acc_ref[...] += _dot(act, wd_ref[sl, :])

    if n_ctx_tiles == 0:
        step(False)
    else:
        pl.when(is_lat)(functools.partial(step, False))
        pl.when(jnp.logical_not(is_lat))(functools.partial(step, tm > nc))

    @pl.when(j == pl.num_programs(1) - 1)
    def _():
        y = x_ref[...] + gate_ref[...] * acc_ref[...]
        if final_norm:
            y = y * lax.rsqrt(jnp.mean(y * y, axis=-1, keepdims=True) + NORM_EPS) * fg_ref[...]
        o_ref[...] = y


def _ffn(x, g, shift, scale, gate, w_up, conv_w, conv_b, w_down, final_g, lay, n_rows, final_norm, tm=512, tf=512,
         sub=MXU_TILE):
    M, D = x.shape
    F = w_down.shape[0]
    tm = _row_tile(lay, tm)
    assert tm % lay.nc == 0, "context tiles must hold whole segments"
    tiles_per_seg = lay.n // tm
    n_lat_tiles = lay.lat_rows // tm
    n_tiles = n_rows // tm
    tf = _pick(F, tf)
    nf = F // tf
    halo = BF16_ROWS
    per = tm // halo
    seg = _seg_index(lay, tm)
    vec = lambda: pl.BlockSpec((None, 1, D), lambda i, j: (seg(i), 0, 0))
    n_sub = tf // _pick(tf, sub)
    return pl.pallas_call(
        functools.partial(_ffn_kernel, n_lat_tiles, n_tiles - n_lat_tiles, tiles_per_seg, lay.nc, tm, halo, n_sub,
                          final_norm),
        grid=(n_tiles, nf),
        in_specs=[
            pl.BlockSpec((tm, D), lambda i, j: (i, 0)),
            pl.BlockSpec((halo, D), lambda i, j: (jnp.maximum(i * per - 1, 0), 0)),
            pl.BlockSpec((halo, D), lambda i, j: (jnp.minimum((i + 1) * per, M // halo - 1), 0)),
            pl.BlockSpec((1, D), lambda i, j: (0, 0)),
            vec(), vec(), vec(),
            pl.BlockSpec((D, tf), lambda i, j: (0, j)),
            pl.BlockSpec((D, tf), lambda i, j: (0, nf + j)),
            pl.BlockSpec((CONV_W, tf), lambda i, j: (0, j)),
            pl.BlockSpec((CONV_W, tf), lambda i, j: (0, nf + j)),
            pl.BlockSpec((1, tf), lambda i, j: (0, j)),
            pl.BlockSpec((1, tf), lambda i, j: (0, nf + j)),
            pl.BlockSpec((tf, D), lambda i, j: (j, 0)),
            pl.BlockSpec((1, D), lambda i, j: (0, 0)),
        ],
        out_specs=pl.BlockSpec((tm, D), lambda i, j: (i, 0)),
        out_shape=jax.ShapeDtypeStruct((n_rows, D), _F32),
        scratch_shapes=[pltpu.VMEM((tm + 2 * halo, D), _BF16), pltpu.VMEM((tm, D), _F32),
                        pltpu.VMEM((n_sub, 2, tm + 2 * halo, tf // n_sub), _F32)],
        compiler_params=_cparams("parallel", "arbitrary"),
        name="conv_gated_ffn",
    )(x, x, x, g, shift, scale, gate, w_up, w_up, conv_w, conv_w, conv_b, conv_b, w_down, final_g.reshape(1, D))


def _shift_mix_kernel(lay, tm, halo, full, loras, x_ref, xp_ref, xn_ref, g_ref, sh_ref, sc_ref, mu_ref, *refs):
    w_refs = refs[:len(loras)]
    o_refs = refs[len(loras):]
    i = pl.program_id(0)
    g, sh, sc = g_ref[...], sh_ref[...], sc_ref[...]
    h = _normmod(x_ref[...], g, sh, sc)
    hp = _normmod(xp_ref[...], g, sh, sc)
    hn = _normmod(xn_ref[...], g, sh, sc)
    ext = jnp.concatenate([hp, h, hn], axis=0)
    n_ext = tm + 2 * halo
    has_prev, has_next = _seg_masks(i, tm, h.shape, lay)
    prev = jnp.where(has_prev, pltpu.roll(ext, 1, 0)[halo:halo + tm], 0.0)
    nxt = jnp.where(has_next, pltpu.roll(ext, n_ext - 1, 0)[halo:halo + tm], 0.0)
    dx = 0.5 * (prev + nxt) - h
    mu = mu_ref[...]
    mixes = {}

    def mix(m):
        if m not in mixes:
            mixes[m] = (h + dx * mu[m:m + 1]).astype(_BF16)
        return mixes[m]

    for m, o_ref in zip(full, o_refs):
        o_ref[...] = mix(m)
    for (m, act), w_ref, o_ref in zip(loras, w_refs, o_refs[len(full):]):
        t = _dot(mix(m), w_ref[...])
        if act == "tanh":
            t = jnp.tanh(t)
        elif act == "sigmoid":
            t = _sigmoid(t)
        o_ref[...] = t.astype(o_ref.dtype)


def _shift_mix(x, g, shift, scale, mu, full, loras, lay, tm=256):
    M, D = x.shape
    tm = _row_tile(lay, tm)
    halo = F32_ROWS
    per = tm // halo
    seg = _seg_index(lay, tm)
    vec = lambda: pl.BlockSpec((None, 1, D), lambda i: (seg(i), 0, 0))
    w1s = [w for _, w, _ in loras]
    outs = pl.pallas_call(
        functools.partial(_shift_mix_kernel, lay, tm, halo, tuple(full), tuple((m, act) for m, _, act in loras)),
        grid=(M // tm,),
        in_specs=[
            pl.BlockSpec((tm, D), lambda i: (i, 0)),
            pl.BlockSpec((halo, D), lambda i: (jnp.maximum(i * per - 1, 0), 0)),
            pl.BlockSpec((halo, D), lambda i: (jnp.minimum((i + 1) * per, M // halo - 1), 0)),
            pl.BlockSpec((1, D), lambda i: (0, 0)),
            vec(), vec(),
            pl.BlockSpec(mu.shape, lambda i: (0, 0)),
        ] + [pl.BlockSpec(w.shape, lambda i: (0, 0)) for w in w1s],
        out_specs=[pl.BlockSpec((tm, D), lambda i: (i, 0)) for _ in full]
        + [pl.BlockSpec((tm, w.shape[1]), lambda i: (i, 0)) for w in w1s],
        out_shape=[jax.ShapeDtypeStruct((M, D), _BF16) for _ in full]
        + [jax.ShapeDtypeStruct((M, w.shape[1]), _BF16) for w in w1s],
        compiler_params=_cparams("parallel"),
        name="rwkv_shift_mix",
    )(x, x, x, g, shift, scale, mu, *w1s)
    return outs[:len(full)], outs[len(full):]


def _wkv_kernel(reverse, final, C, LW, r_ref, k_ref, v_ref, wp_ref, ap_ref, kk_ref, ka_ref, rk_ref, *refs):
    if final:
        y0_ref, b0_ref, g_ref, lng_ref, lnb_ref, o_ref, s_ref = refs
    else:
        y_ref, bon_ref, s_ref = refs
    j = pl.program_id(2)

    @pl.when(j == 0)
    def _():
        s_ref[...] = jnp.zeros_like(s_ref)

    hpg = WKV_GROUP // RWKV_HEAD
    groups = range(LW // WKV_GROUP)
    ng = len(groups)
    sls = [slice(g * WKV_GROUP, (g + 1) * WKV_GROUP) for g in groups]
    row = lax.broadcasted_iota(jnp.int32, (C, C), 0)
    col = lax.broadcasted_iota(jnp.int32, (C, C), 1)
    tri = jnp.where((col >= row) if reverse else (col <= row), 1.0, 0.0).astype(_BF16)
    rc = lax.broadcasted_iota(jnp.int32, (C, WKV_GROUP), 0)
    cc = lax.rem(lax.broadcasted_iota(jnp.int32, (C, WKV_GROUP), 1), C)
    strict = (cc > rc) if reverse else (cc < rc)
    incl = (cc >= rc) if reverse else (cc <= rc)
    eye_cat = jnp.where(cc == rc, 1.0, 0.0).astype(_F32)
    br = lax.broadcasted_iota(jnp.int32, (WKV_GROUP, WKV_GROUP), 0) // RWKV_HEAD
    bc = lax.broadcasted_iota(jnp.int32, (WKV_GROUP, WKV_GROUP), 1) // RWKV_HEAD
    same_head = br == bc
    head_ones = jnp.where(same_head, 1.0, 0.0).astype(_BF16)

    def bd(x):
        return jnp.where(same_head, jnp.concatenate([x] * hpg, axis=0), jnp.zeros((), x.dtype))

    cast = lambda x: x.astype(_BF16)

    n_chunks = r_ref.shape[0] // C
    units = [(c, g) for c in range(n_chunks) for g in groups]
    nu = len(units)
    tile = lambda ref, u: ref[u[0] * C:(u[0] + 1) * C, sls[u[1]]]
    lane = lambda ref, u: ref[:, sls[u[1]]]

    v = [tile(v_ref, u) for u in units]
    lw = [-math.exp(-0.5) * _sigmoid(tile(wp_ref, u)) for u in units]
    cl = [_ones_dot_left(tri, x, 2) for x in lw]
    p_total = [jnp.exp(c[0:1] if reverse else c[C - 1:C]) for c in cl]

    a = [_sigmoid(tile(ap_ref, u)) for u in units]
    kk = [tile(k_ref, u) * lane(kk_ref, u) for u in units]
    kd = [tile(k_ref, u) * (1.0 + (a[i] - 1.0) * lane(ka_ref, u)) for i, u in enumerate(units)]
    rkd = [tile(r_ref, u) * kd[i] * lane(rk_ref, u) for i, u in enumerate(units)]
    sums = _ones_dot_rows(head_ones, [x * x for x in kk] + rkd, 2)
    kk = [x * lax.rsqrt(jnp.maximum(n, 1e-24)) for x, n in zip(kk, sums[:nu])]
    bonus = [s * x for s, x in zip(sums[nu:], v)]
    b = [kk[i] * a[i] for i in range(nu)]

    AR = [jnp.concatenate([cast(-kk[i] * jnp.exp(cl[i] - lw[i])), cast(tile(r_ref, u) * jnp.exp(cl[i]))], axis=0)
          for i, u in enumerate(units)]
    BKbd, BKp = [], []
    for i in range(nu):
        e_neg = jnp.exp(-cl[i])
        e_rem = e_neg * p_total[i]
        BKbd.append(jnp.concatenate([bd(cast(b[i] * e_neg)), bd(cast(kd[i] * e_neg))], axis=0))
        BKp.append(cast(jnp.concatenate([b[i] * e_rem, kd[i] * e_rem], axis=0)))
    vb = [cast(x) for x in v]
    Vbd = [bd(x) for x in vb]

    M = [_dot_nt(AR[i], BKbd[i]) for i in range(nu)]
    Lab = [jnp.where(strict, m[:C, :WKV_GROUP], 0.0) for m in M]
    LMk = [cast(jnp.concatenate([jnp.where(strict, m[:C, WKV_GROUP:], 0.0),
                                 jnp.where(incl, m[C:, WKV_GROUP:], 0.0)], axis=0)) for m in M]
    Mrb = [cast(jnp.where(incl, m[C:, :WKV_GROUP], 0.0)) for m in M]
    LMkV = [_dot(LMk[i], Vbd[i]) for i in range(nu)]

    Pw = [cast(x) for x in Lab]
    T = [eye_cat + x for x in Lab]
    Pw = [cast(_dot(x, bd(x))) for x in Pw]
    for _ in range(int(math.log2(C)) - 2):
        both = [_dot(jnp.concatenate([cast(t), x], axis=0), bd(x)) for t, x in zip(T, Pw)]
        T = [t + bt[:C] for t, bt in zip(T, both)]
        Pw = [cast(bt[C:]) for bt in both]
    Tb = [cast(t + _dot(cast(t), bd(x))) for t, x in zip(T, Pw)]

    S = [s_ref[g] for g in groups]
    y = [None] * nu
    for c in (reversed(range(n_chunks)) if reverse else range(n_chunks)):
        us = [c * ng + g for g in groups]
        ARS = [_dot_nt(AR[i], cast(S[g])) for g, i in enumerate(us)]
        X = [ARS[g][:C] + LMkV[i][:C] for g, i in enumerate(us)]
        Z = [cast(_dot(Tb[i], bd(cast(X[g])))) for g, i in enumerate(us)]
        for g, i in enumerate(us):
            y[i] = ARS[g][C:] + LMkV[i][C:] + _dot(Mrb[i], bd(Z[g]))
        dS = [_dot_tn(jnp.concatenate([Z[g], vb[i]], axis=0), BKp[i]) for g, i in enumerate(us)]
        S = [S[g] * p_total[i] + jnp.where(same_head, dS[g], 0.0) for g, i in enumerate(us)]
    s_ref[...] = jnp.stack(S, axis=0)

    def assemble(xs):
        return jnp.concatenate([jnp.concatenate(xs[c * ng:(c + 1) * ng], axis=1) for c in range(n_chunks)], axis=0)

    if final:
        inv = 1.0 / RWKV_HEAD
        wkv = [y[i] + tile(y0_ref, u) for i, u in enumerate(units)]
        cen = [x - s * inv for x, s in zip(wkv, _ones_dot_rows(head_ones, wkv, 2))]
        var = [s * inv for s in _ones_dot_rows(head_ones, [x * x for x in cen], 2)]
        outs = []
        for i, u in enumerate(units):
            normed = cen[i] * lax.rsqrt(var[i] + GN_EPS) * lane(lng_ref, u) + lane(lnb_ref, u)
            outs.append(((normed + bonus[i] + tile(b0_ref, u)) * tile(g_ref, u)).astype(o_ref.dtype))
        o_ref[...] = assemble(outs)
    else:
        y_ref[...] = assemble(y)
        bon_ref[...] = assemble(bonus)


def _wkv(r, k, v, wpre, apre, k_k, k_a, r_k, lay, reverse, final_args=None, lw=2048, chunks_per_step=2):
    M, D = r.shape
    C = WKV_CHUNK
    rows = chunks_per_step * C
    assert C == RWKV_HEAD and lay.nc % rows == 0 and lay.n % rows == 0 and D % WKV_GROUP == 0
    lw = _pick(D, lw)
    assert lw % WKV_GROUP == 0
    ncc, ncl = lay.nc // rows, lay.n // rows
    ctx_blk0 = lay.lat_rows // rows

    def row_blk(b, j):
        if reverse:
            return jnp.where(j < ncc, ctx_blk0 + b * ncc + (ncc - 1 - j), b * ncl + (ncl - 1 - (j - ncc)))
        return jnp.where(j < ncc, ctx_blk0 + b * ncc + j, b * ncl + (j - ncc))

    tile = lambda: pl.BlockSpec((rows, lw), lambda b, h, j: (row_blk(b, j), h))
    vec = lambda: pl.BlockSpec((1, lw), lambda b, h, j: (0, h))
    in_specs = [tile() for _ in range(5)] + [vec() for _ in range(3)]
    args = [r, k, v, wpre, apre, k_k.reshape(1, D), k_a.reshape(1, D), r_k.reshape(1, D)]
    final = final_args is not None
    if final:
        y0, b0, g, lng, lnb = final_args
        in_specs += [tile(), tile(), tile(), vec(), vec()]
        args += [y0, b0, g, lng.reshape(1, D), lnb.reshape(1, D)]
        out_specs = tile()
        out_shape = jax.ShapeDtypeStruct((M, D), _BF16)
    else:
        out_specs = [tile(), tile()]
        out_shape = [jax.ShapeDtypeStruct((M, D), _F32), jax.ShapeDtypeStruct((M, D), _F32)]
    return pl.pallas_call(
        functools.partial(_wkv_kernel, reverse, final, C, lw),
        grid=(lay.B, D // lw, ncc + ncl),
        in_specs=in_specs,
        out_specs=out_specs,
        out_shape=out_shape,
        scratch_shapes=[pltpu.VMEM((lw // WKV_GROUP, WKV_GROUP, WKV_GROUP), _F32)],
        compiler_params=_cparams("parallel", "parallel", "arbitrary"),
        name="wkv7_scan_rev" if reverse else "wkv7_scan_fwd",
    )(*args)


def _rope_tables(lay):
    n = lay.n
    t = jnp.arange(n, dtype=jnp.int32)
    row = (t // GRID_W).astype(_F32)
    col = (t % GRID_W).astype(_F32)
    axis_dim = HEAD_DIM // 2
    inv_freq = ROPE_THETA ** (-jnp.arange(0, axis_dim, 2, dtype=_F32) / axis_dim)
    ang = jnp.concatenate([row[:, None] * inv_freq, col[:, None] * inv_freq], axis=-1)
    cos, sin = jnp.cos(ang), jnp.sin(ang)
    cos_l = jnp.concatenate([cos, cos], axis=-1)
    sin_l = jnp.concatenate([-sin, sin], axis=-1)
    n_ctx_rows = lay.B * lay.nc
    cos_f = jnp.concatenate([jnp.tile(cos_l, (lay.B, 1)), jnp.ones((n_ctx_rows, HEAD_DIM), _F32)], axis=0)
    sin_f = jnp.concatenate([jnp.tile(sin_l, (lay.B, 1)), jnp.zeros((n_ctx_rows, HEAD_DIM), _F32)], axis=0)
    return cos_f, sin_f


def _deinterleave_heads(w, n_heads):
    lead = w.shape[:-1]
    w = w.reshape(*lead, n_heads, HEAD_DIM // 2, 2)
    w = jnp.swapaxes(w, -1, -2)
    return w.reshape(*lead, n_heads * HEAD_DIM)


def _pad_cols(w, mult=LANES):
    pad = (-w.shape[-1]) % mult
    return jnp.pad(w, [(0, 0)] * (w.ndim - 1) + [(0, pad)]) if pad else w


def _pad_rows(w, mult=LANES):
    pad = (-w.shape[-2]) % mult
    return jnp.pad(w, [(0, 0)] * (w.ndim - 2) + [(0, pad), (0, 0)]) if pad else w


def kernel(x, c, ctx, c_ctx, w_mod, b_mod, norm1_g, norm2_g, attn_w_in, attn_w_out, q_norm_g, k_norm_g, rwkv_mu, rwkv_w_r, rwkv_w_k, rwkv_w_v, rwkv_w_o, rwkv_decay_w0, rwkv_decay_w1, rwkv_decay_w2, rwkv_iclr_a0, rwkv_iclr_a1, rwkv_iclr_a2, rwkv_gate_g1, rwkv_gate_g2, rwkv_k_k, rwkv_k_a, rwkv_r_k, rwkv_lnx_g, rwkv_lnx_b, rwkv_vres_v0, rwkv_vres_v1, rwkv_vres_v2, ffn_w_up, ffn_conv_w, ffn_conv_b, ffn_w_down, final_norm_g):
    B, n, D = x.shape
    nc = ctx.shape[1]
    depth = w_mod.shape[0]
    lay = Layout(B, n, nc, B * n, B * (n + nc))
    attn_w = N_ATTN_HEADS * HEAD_DIM
    kv_w = N_KV_HEADS * HEAD_DIM
    fw = D - attn_w
    bf = lambda w: w.astype(_BF16)

    xs = jnp.concatenate([x.reshape(B * n, D), ctx.reshape(B * nc, D)], axis=0)

    cvec = jnp.concatenate([c, c_ctx[None, :], jnp.zeros((-(B + 1) % BF16_ROWS, D), _F32)], axis=0)
    mods = _modulation(cvec, w_mod, b_mod)[:, :B + 1]
    mods = mods.reshape(depth, B + 1, 6, 1, D)
    cos_f, sin_f = _rope_tables(lay)
    qg_scale = HEAD_DIM ** -0.5

    v_first = None
    for i in range(depth):
        last = i == depth - 1
        jl = i // 2
        m = [mods[i, :, t] for t in range(6)]
        g1 = norm1_g[i].reshape(1, D)
        if i % 2 == 0:
            w_in = bf(attn_w_in[jl])
            w_in = jnp.concatenate([
                _deinterleave_heads(w_in[:, :attn_w], N_ATTN_HEADS),
                _deinterleave_heads(w_in[:, attn_w:attn_w + kv_w], N_KV_HEADS),
                w_in[:, attn_w + kv_w:]], axis=1)
            qg = (_deinterleave_heads(q_norm_g[jl], 1) * qg_scale).reshape(1, HEAD_DIM)
            kg = _deinterleave_heads(k_norm_g[jl], 1).reshape(1, HEAD_DIM)
            u = _attn_in_proj(xs, g1, m[0], m[1], w_in, qg, kg, cos_f, sin_f, lay)
            o = _attention(u, lay)
            fm = jnp.concatenate([_fourier_mix(u, lay, fw), _fourier_mix(u, lay, fw, ctx=True)], axis=0)
            w_out = bf(attn_w_out[jl])
            xs = _out_proj([o, fm], [w_out[:attn_w], w_out[attn_w:]], xs, m[2], lay)
        else:
            loras = [(1, _lora_w1(list(rwkv_decay_w1[jl])), "tanh"), (4, _lora_w1(list(rwkv_iclr_a1[jl])), None),
                     (5, _lora_w1([rwkv_gate_g1[jl]]), "sigmoid")]
            if jl > 0:
                loras.append((3, _lora_w1([rwkv_vres_v1[jl - 1]]), None))
            (x_r, x_k, x_v), acts = _shift_mix(xs, g1, m[0], m[1], rwkv_mu[jl], (0, 2, 3), loras, lay)
            r = _linear(x_r, bf(rwkv_w_r[jl]), name="rwkv_r")
            k = _linear(x_k, bf(rwkv_w_k[jl]), name="rwkv_k")
            if jl == 0:
                v = v_first = _linear(x_v, bf(rwkv_w_v[jl]), name="rwkv_v")
            else:
                v = _linear(x_v, bf(rwkv_w_v[jl]), name="rwkv_v_vres",
                            vres=(acts[3], rwkv_vres_v2[jl - 1], rwkv_vres_v0[jl - 1], v_first))
            wpre = _lora_out(acts[0], list(rwkv_decay_w2[jl]), list(rwkv_decay_w0[jl]), name="rwkv_decay")
            apre = _lora_out(acts[1], list(rwkv_iclr_a2[jl]), list(rwkv_iclr_a0[jl]), name="rwkv_iclr")
            (g,) = _lora_out(acts[2], [rwkv_gate_g2[jl]], name="rwkv_gate")
            kk_, ka_, rk_ = rwkv_k_k[jl], rwkv_k_a[jl], rwkv_r_k[jl].reshape(D)
            y0, bon0 = _wkv(r, k, v, wpre[0], apre[0], kk_, ka_, rk_, lay, reverse=False)
            z = _wkv(r, k, v, wpre[1], apre[1], kk_, ka_, rk_, lay, reverse=True,
                     final_args=(y0, bon0, g, rwkv_lnx_g[jl], rwkv_lnx_b[jl]))
            xs = _out_proj([z], [bf(rwkv_w_o[jl])], xs, m[2], lay)
        n_rows = lay.lat_rows if last else lay.rows
        xs = _ffn(xs, norm2_g[i].reshape(1, D), m[3], m[4], m[5], bf(ffn_w_up[i]), ffn_conv_w[i],
                  ffn_conv_b[i].reshape(1, -1), bf(ffn_w_down[i]), final_norm_g, lay, n_rows, final_norm=last)
    return xs.reshape(B, n, D)
```
